```python
import jax, jax.numpy as jnp
from jax import lax
import numpy as np

D_MODEL = 2048
BATCH = 4
SEQ = 4096
DEPTH = 2

HEAD_DIM = 128
A_HEADS = D_MODEL // (2 * HEAD_DIM)
A_WIDTH = A_HEADS * HEAD_DIM
A_PATTERNS = ((128, 1), (512, 4), (2048, 16))
A_BLOCK = 128
B_KEY_DIM = 128
B_VAL_DIM = 128
B_HEADS = D_MODEL // (2 * B_VAL_DIM)
B_KW = B_HEADS * B_KEY_DIM
B_VW = B_HEADS * B_VAL_DIM
B_CHUNK = 64
C_HEADS = D_MODEL // HEAD_DIM
C_WIDTH = C_HEADS * HEAD_DIM
C_BLOCK = 128
REL_BUCKETS = 32
REL_MAX_DIST = 2048
N_EXPERTS = 64
TOP_K = 8
N_GROUPS = 8
TOPK_GROUPS = 4
EXPERT_DIM = 512
SHARED_DIM = 512
ROUTED_SCALE = 2.5
LN_EPS = 1e-5
RMS_EPS = 1e-6
DN_ALPHA = (2 * DEPTH) ** 0.25
DN_BETA = (8 * DEPTH) ** -0.25
NEG_INF = -1e30
IN0_SIZES = (A_WIDTH, A_WIDTH, A_WIDTH, B_KW, B_KW, B_VW, B_VW)
IN0_WIDTH = sum(IN0_SIZES)
IN1_WIDTH = 3 * C_WIDTH + C_HEADS

kernel_name = 'hybrid_dilated_hgrn2_fox_moe_deepnorm'


def _layer_norm(x, g, b):
    xf = x.astype(jnp.float32)
    mu = jnp.mean(xf, axis=-1, keepdims=True)
    var = jnp.mean(jnp.square(xf - mu), axis=-1, keepdims=True)
    y = (xf - mu) * lax.rsqrt(var + LN_EPS) * g.astype(jnp.float32) + b.astype(jnp.float32)
    return y.astype(x.dtype)


def _t5_bucket(dist):
    max_exact = REL_BUCKETS // 2
    d = np.maximum(dist, 1).astype(np.float64)
    large = max_exact + (np.log(d / max_exact) / np.log(REL_MAX_DIST / max_exact)
                         * (REL_BUCKETS - max_exact)).astype(np.int64)
    large = np.minimum(large, REL_BUCKETS - 1)
    return np.where(dist < max_exact, dist, large).astype(np.int32)


def _dilated_branch(q, k, v, rel_bias, window, dilation):
    bsz, seq, heads, hd = q.shape
    L = seq // dilation
    wd = window // dilation
    nb = -(-L // A_BLOCK)
    Lp = nb * A_BLOCK

    def to_blocks(t):
        t = t.reshape(bsz, L, dilation, heads, hd).transpose(0, 2, 3, 1, 4)
        t = jnp.pad(t, ((0, 0), (0, 0), (0, 0), (0, Lp - L), (0, 0)))
        return t.reshape(bsz, dilation, heads, nb, A_BLOCK, hd)

    def with_prev(t):
        prev = jnp.pad(t, ((0, 0), (0, 0), (0, 0), (1, 0), (0, 0), (0, 0)))[:, :, :, :nb]
        return jnp.concatenate([prev, t], axis=4)

    qb = to_blocks(q)
    kb = with_prev(to_blocks(k))
    vb = with_prev(to_blocks(v))
    qi = np.arange(A_BLOCK)[:, None]
    ki = np.arange(2 * A_BLOCK)[None, :]
    rel = qi + A_BLOCK - ki
    band = (rel >= 0) & (rel <= wd)
    valid = band[None] & ((np.arange(nb)[:, None, None] > 0) | (ki[None] >= A_BLOCK))
    bucket = _t5_bucket(np.clip(rel, 0, None) * dilation)
    bias = rel_bias.astype(jnp.float32)[bucket].transpose(2, 0, 1)[:, None]
    s = jnp.einsum('bdhnqe,bdhnke->bdhnqk', qb, kb).astype(jnp.float32) * HEAD_DIM ** -0.5 + bias
    s = jnp.where(valid, s, NEG_INF)
    m = jnp.max(s, axis=-1, keepdims=True)
    p = jnp.exp(s - m)
    l = jnp.sum(p, axis=-1)
    o = jnp.einsum('bdhnqk,bdhnke->bdhnqe', p, vb.astype(jnp.float32)) / l[..., None]
    lse = m[..., 0] + jnp.log(l)
    o = o.reshape(bsz, dilation, heads, Lp, hd)[:, :, :, :L].transpose(0, 3, 1, 2, 4).reshape(bsz, seq, heads, hd)
    lse = lse.reshape(bsz, dilation, heads, Lp)[..., :L].transpose(0, 3, 1, 2).reshape(bsz, seq, heads)
    return o, lse


def _dilated_mixture(q, k, v, rel_bias):
    outs, lses = [], []
    for window, dilation in A_PATTERNS:
        o, lse = _dilated_branch(q, k, v, rel_bias, window, dilation)
        outs.append(o)
        lses.append(lse)
    w = jax.nn.softmax(jnp.stack(lses, axis=0), axis=0)
    return jnp.sum(w[..., None] * jnp.stack(outs, axis=0), axis=0)


def _hgrn2(q, f_logit, i, g, gn_g, lb):
    f32 = jnp.float32
    bsz, seq, _ = q.shape
    H, dk, dv, C = B_HEADS, B_KEY_DIM, B_VAL_DIM, B_CHUNK
    n = seq // C
    lb = lb.astype(f32).reshape(H, dk)
    fq = jax.nn.silu(q.astype(f32)).reshape(bsz, seq, H, dk)
    forget = lb + (1.0 - lb) * jax.nn.sigmoid(f_logit.astype(f32).reshape(bsz, seq, H, dk))
    log_f = jnp.log(forget)
    key = 1.0 - forget
    val = i.astype(f32).reshape(bsz, seq, H, dv)

    def chunks(t):
        return t.reshape(bsz, n, C, H, t.shape[-1]).transpose(1, 0, 3, 2, 4)

    qc, kc, vc = chunks(fq), chunks(key), chunks(val)
    ac = jnp.cumsum(chunks(log_f), axis=3)
    causal = np.tril(np.ones((C, C), dtype=bool))

    def step(state, xs):
        q_c, k_c, v_c, a_c = xs
        inter = jnp.einsum('bhtk,bhkv->bhtv', q_c * jnp.exp(a_c), state)
        diff = a_c[:, :, :, None, :] - a_c[:, :, None, :, :]
        decay = jnp.exp(jnp.where(causal[:, :, None], diff, NEG_INF))
        scores = jnp.einsum('bhtk,bhsk,bhtsk->bhts', q_c, k_c, decay)
        intra = jnp.einsum('bhts,bhsv->bhtv', scores, v_c)
        a_last = a_c[:, :, -1:, :]
        k_dec = k_c * jnp.exp(a_last - a_c)
        state = jnp.exp(a_last[:, :, 0, :, None]) * state + jnp.einsum('bhsk,bhsv->bhkv', k_dec, v_c)
        return state, inter + intra

    state0 = jnp.zeros((bsz, H, dk, dv), f32)
    _, o = lax.scan(step, state0, (qc, kc, vc, ac))
    o = o.transpose(1, 0, 3, 2, 4).reshape(bsz, seq, H, dv)
    o = o * lax.rsqrt(jnp.mean(o * o, axis=-1, keepdims=True) + RMS_EPS) * gn_g.astype(f32).reshape(H, dv)
    o = o.reshape(bsz, seq, H * dv) * jax.nn.silu(g.astype(f32))
    return o.astype(q.dtype)


def _even_mixer(h, w_in, gn_g, w_out, rel_bias, lb):
    bsz, seq, _ = h.shape
    proj = h @ w_in
    qa, ka, va, qb, fb, ib, gb = jnp.split(proj, list(np.cumsum(IN0_SIZES)[:-1]), axis=-1)
    shp = (bsz, seq, A_HEADS, HEAD_DIM)
    o_a = _dilated_mixture(qa.reshape(shp), ka.reshape(shp), va.reshape(shp), rel_bias)
    o_a = o_a.reshape(bsz, seq, A_WIDTH).astype(h.dtype)
    o_b = _hgrn2(qb, fb, ib, gb, gn_g, lb)
    return jnp.concatenate([o_a, o_b], axis=-1) @ w_out


def _fox_attention(q, k, v, f_logit):
    bsz, seq, heads, hd = q.shape
    nq = seq // C_BLOCK
    c = jnp.cumsum(jax.nn.log_sigmoid(f_logit.astype(jnp.float32)), axis=1)
    c_k = c.transpose(0, 2, 1)[:, :, None, :]
    qb = q.reshape(bsz, nq, C_BLOCK, heads, hd).transpose(1, 0, 2, 3, 4)
    cb = c.reshape(bsz, nq, C_BLOCK, heads).transpose(1, 0, 2, 3)
    qpos = jnp.arange(seq).reshape(nq, C_BLOCK)
    kpos = jnp.arange(seq)

    def block(xs):
        q_blk, c_blk, pos = xs
        s = jnp.einsum('bqhe,bkhe->bhqk', q_blk, k).astype(jnp.float32) * hd ** -0.5
        s = s + c_blk.transpose(0, 2, 1)[..., None] - c_k
        s = jnp.where(kpos[None, :] <= pos[:, None], s, NEG_INF)
        p = jax.nn.softmax(s, axis=-1)
        return jnp.einsum('bhqk,bkhe->bqhe', p.astype(v.dtype), v)

    o = lax.map(block, (qb, cb, qpos))
    return o.transpose(1, 0, 2, 3, 4).reshape(bsz, seq, heads, hd)


def _odd_mixer(h, w_in, b_f, w_out):
    bsz, seq, _ = h.shape
    proj = h @ w_in
    q, k, v, f = jnp.split(proj, [C_WIDTH, 2 * C_WIDTH, 3 * C_WIDTH], axis=-1)
    shp = (bsz, seq, C_HEADS, HEAD_DIM)
    o = _fox_attention(q.reshape(shp), k.reshape(shp), v.reshape(shp), f + b_f)
    return o.reshape(bsz, seq, C_WIDTH) @ w_out


def _moe(h, router_w, router_bias, w_gate, w_up, w_down, s_gate, s_up, s_down):
    bsz, seq, d = h.shape
    t = h.reshape(-1, d)
    n_tok = t.shape[0]
    scores = jax.nn.sigmoid((t @ router_w).astype(jnp.float32))
    sel = scores + router_bias.astype(jnp.float32)
    grouped = sel.reshape(n_tok, N_GROUPS, N_EXPERTS // N_GROUPS)
    group_score = jnp.sum(lax.top_k(grouped, 2)[0], axis=-1)
    _, gidx = lax.top_k(group_score, TOPK_GROUPS)
    gmask = jnp.sum(jax.nn.one_hot(gidx, N_GROUPS, dtype=jnp.float32), axis=1)
    emask = jnp.repeat(gmask, N_EXPERTS // N_GROUPS, axis=1)
    _, eidx = lax.top_k(jnp.where(emask > 0, sel, NEG_INF), TOP_K)
    w = jnp.take_along_axis(scores, eidx, axis=1)
    w = w / jnp.sum(w, axis=-1, keepdims=True) * ROUTED_SCALE
    gate = jnp.sum(jax.nn.one_hot(eidx, N_EXPERTS, dtype=jnp.float32) * w[..., None], axis=1)
    shared = (jax.nn.silu(t @ s_gate) * (t @ s_up)) @ s_down

    def expert_step(acc, xs):
        wg, wu, wd, g_e = xs
        y = (jax.nn.silu(t @ wg) * (t @ wu)) @ wd
        return acc + g_e[:, None].astype(t.dtype) * y, None

    routed, _ = lax.scan(expert_step, jnp.zeros_like(t), (w_gate, w_up, w_down, gate.T))
    return (shared + routed).reshape(bsz, seq, d)


def setup_inputs(seed: int = 0) -> dict:
    key = jax.random.key(seed)
    ks = jax.random.split(key, 21)
    f32 = jnp.float32
    n_even, n_odd = (DEPTH + 1) // 2, DEPTH // 2

    def nrm(k, shape, scale):
        return jax.random.normal(k, shape, f32) * scale

    return {
        'x': nrm(ks[0], (BATCH, SEQ, D_MODEL), 1.0),
        'rel_bias': nrm(ks[1], (REL_BUCKETS, A_HEADS), 0.5),
        'hgrn_lb_logits': nrm(ks[2], (DEPTH + 1, B_KW), 0.5),
        'a_w_in': nrm(ks[3], (n_even, D_MODEL, IN0_WIDTH), D_MODEL ** -0.5),
        'a_gn_g': 1.0 + nrm(ks[4], (n_even, B_VW), 0.02),
        'a_w_out': nrm(ks[5], (n_even, A_WIDTH + B_VW, D_MODEL), (A_WIDTH + B_VW) ** -0.5 * DN_BETA),
        'c_w_in': nrm(ks[6], (n_odd, D_MODEL, IN1_WIDTH), D_MODEL ** -0.5),
        'c_b_f': 1.0 + nrm(ks[7], (n_odd, C_HEADS), 0.1),
        'c_w_out': nrm(ks[8], (n_odd, C_WIDTH, D_MODEL), C_WIDTH ** -0.5 * DN_BETA),
        'ln_mix_g': 1.0 + nrm(ks[9], (DEPTH, D_MODEL), 0.02),
        'ln_mix_b': nrm(ks[10], (DEPTH, D_MODEL), 0.02),
        'ln_ffn_g': 1.0 + nrm(ks[11], (DEPTH, D_MODEL), 0.02),
        'ln_ffn_b': nrm(ks[12], (DEPTH, D_MODEL), 0.02),
        'router_w': nrm(ks[13], (DEPTH, D_MODEL, N_EXPERTS), D_MODEL ** -0.5),
        'router_bias': nrm(ks[14], (DEPTH, N_EXPERTS), 0.01),
        'exp_w_gate': nrm(ks[15], (DEPTH, N_EXPERTS, D_MODEL, EXPERT_DIM), D_MODEL ** -0.5),
        'exp_w_up': nrm(ks[16], (DEPTH, N_EXPERTS, D_MODEL, EXPERT_DIM), D_MODEL ** -0.5),
        'exp_w_down': nrm(ks[17], (DEPTH, N_EXPERTS, EXPERT_DIM, D_MODEL), EXPERT_DIM ** -0.5 * DN_BETA),
        'sh_w_gate': nrm(ks[18], (DEPTH, D_MODEL, SHARED_DIM), D_MODEL ** -0.5),
        'sh_w_up': nrm(ks[19], (DEPTH, D_MODEL, SHARED_DIM), D_MODEL ** -0.5),
        'sh_w_down': nrm(ks[20], (DEPTH, SHARED_DIM, D_MODEL), SHARED_DIM ** -0.5 * DN_BETA),
    }


def reference(x, rel_bias, hgrn_lb_logits, a_w_in, a_gn_g, a_w_out, c_w_in, c_b_f, c_w_out,
              ln_mix_g, ln_mix_b, ln_ffn_g, ln_ffn_b, router_w, router_bias,
              exp_w_gate, exp_w_up, exp_w_down, sh_w_gate, sh_w_up, sh_w_down):
    lb_all = jnp.cumsum(jax.nn.softmax(hgrn_lb_logits.astype(jnp.float32), axis=0), axis=0)
    for layer in range(DEPTH):
        j = layer // 2
        if layer % 2 == 0:
            mix = _even_mixer(x, a_w_in[j], a_gn_g[j], a_w_out[j], rel_bias, lb_all[layer])
        else:
            mix = _odd_mixer(x, c_w_in[j], c_b_f[j], c_w_out[j])
        x = _layer_norm(DN_ALPHA * x + mix, ln_mix_g[layer], ln_mix_b[layer])
        ffn = _moe(x, router_w[layer], router_bias[layer], exp_w_gate[layer], exp_w_up[layer],
                   exp_w_down[layer], sh_w_gate[layer], sh_w_up[layer], sh_w_down[layer])
        x = _layer_norm(DN_ALPHA * x + ffn, ln_ffn_g[layer], ln_ffn_b[layer])
    return x
```

```python
import functools

import numpy as np
import jax
import jax.numpy as jnp
from jax import lax
from jax.experimental import pallas as pl
from jax.experimental.pallas import tpu as pltpu

F32 = jnp.float32
BF16 = jnp.bfloat16

HEAD_DIM = 128
A_PATTERNS = ((128, 1), (512, 4), (2048, 16))
A_BLOCK = 128
REL_BUCKETS = 32
REL_MAX_DIST = 2048
N_EXPERTS = 64
TOP_K = 8
N_GROUPS = 8
GROUP_SIZE = N_EXPERTS // N_GROUPS
TOPK_GROUPS = 4
ROUTED_SCALE = 2.5
LN_EPS = 1e-5
RMS_EPS = 1e-6
DEPTH = 2
DN_ALPHA = (2 * DEPTH) ** 0.25
NEG_INF = -1e30

HGRN_CHUNK = 128
FOX_TILE = 256
ROUTER_TILE = 512
DISPATCH_TILE = 256
EXPERT_TILE = 512
COMBINE_TILE = 128
VMEM_LIMIT = 56 * 1024 * 1024

_NT = (((1,), (1,)), ((), ()))
_TN = (((0,), (0,)), ((), ()))


def _dot(a, b):
    return jnp.dot(a, b, preferred_element_type=F32)


def _dot_nt(a, b):
    return lax.dot_general(a, b, _NT, preferred_element_type=F32)


def _dot_tn(a, b):
    return lax.dot_general(a, b, _TN, preferred_element_type=F32)


def _sigmoid(x):
    return 1.0 / (1.0 + jnp.exp(-x))


def _silu(x):
    return x * _sigmoid(x)


def _split3(x, axis):
    hi = x.astype(BF16)
    r1 = x - hi.astype(F32)
    mid = r1.astype(BF16)
    lo = (r1 - mid.astype(F32)).astype(BF16)
    return jnp.concatenate([hi, mid, lo], axis=axis)


def _layer_norm_rows(y, g, b):
    mu = jnp.mean(y, axis=-1, keepdims=True)
    yc = y - mu
    var = jnp.mean(yc * yc, axis=-1, keepdims=True)
    return yc * lax.rsqrt(var + LN_EPS) * g + b


def _params(*sem):
    return pltpu.CompilerParams(dimension_semantics=sem, vmem_limit_bytes=VMEM_LIMIT)


def _mm_kernel(x_ref, w_ref, o_ref):
    o_ref[...] = _dot(x_ref[...], w_ref[...]).astype(o_ref.dtype)


def _matmul(x, w, out_dtype, tm=1024, tn=1024):
    m, k = x.shape
    nw = w.shape[1]
    tm, tn = min(tm, m), min(tn, nw)
    assert m % tm == 0 and nw % tn == 0
    return pl.pallas_call(
        _mm_kernel,
        grid=(nw // tn, m // tm),
        in_specs=[pl.BlockSpec((tm, k), lambda j, i: (i, 0)),
                  pl.BlockSpec((k, tn), lambda j, i: (0, j))],
        out_specs=pl.BlockSpec((tm, tn), lambda j, i: (i, j)),
        out_shape=jax.ShapeDtypeStruct((m, nw), out_dtype),
        compiler_params=_params("parallel", "parallel"),
        name="proj_matmul",
    )(x, w)


def _proj_ln_kernel(*refs, n_in):
    a_refs, w_refs = refs[:n_in], refs[n_in:2 * n_in]
    r_ref, g_ref, b_ref, o_ref, obf_ref = refs[2 * n_in:]
    acc = _dot(a_refs[0][...], w_refs[0][...])
    for a_ref, w_ref in zip(a_refs[1:], w_refs[1:]):
        acc = acc + _dot(a_ref[...], w_ref[...])
    y = _layer_norm_rows(DN_ALPHA * r_ref[...] + acc, g_ref[...], b_ref[...])
    o_ref[...] = y
    obf_ref[...] = y.astype(BF16)


def _proj_residual_ln(acts, weights, resid, g, b, tm=256):
    m, d = resid.shape
    tm = min(tm, m)
    n_in = len(acts)
    in_specs = [pl.BlockSpec((tm, a.shape[1]), lambda i: (i, 0)) for a in acts]
    in_specs += [pl.BlockSpec(w.shape, lambda i: (0, 0)) for w in weights]
    in_specs += [pl.BlockSpec((tm, d), lambda i: (i, 0)),
                 pl.BlockSpec((1, d), lambda i: (0, 0)),
                 pl.BlockSpec((1, d), lambda i: (0, 0))]
    return pl.pallas_call(
        functools.partial(_proj_ln_kernel, n_in=n_in),
        grid=(m // tm,),
        in_specs=in_specs,
        out_specs=[pl.BlockSpec((tm, d), lambda i: (i, 0)), pl.BlockSpec((tm, d), lambda i: (i, 0))],
        out_shape=[jax.ShapeDtypeStruct((m, d), F32), jax.ShapeDtypeStruct((m, d), BF16)],
        compiler_params=_params("parallel"),
        name="out_proj_ln",
    )(*acts, *weights, resid, g.reshape(1, d), b.reshape(1, d))


def _t5_bucket(dist):
    max_exact = REL_BUCKETS // 2
    d = np.maximum(dist, 1).astype(np.float64)
    large = max_exact + (np.log(d / max_exact) / np.log(REL_MAX_DIST / max_exact)
                         * (REL_BUCKETS - max_exact)).astype(np.int64)
    large = np.minimum(large, REL_BUCKETS - 1)
    return np.where(dist < max_exact, dist, large).astype(np.int32)


def _dilated_branch_kernel(q_ref, k_ref, v_ref, bias_ref, o_ref, lse_ref, *, n_blocks, band):
    blk = A_BLOCK
    qi = lax.broadcasted_iota(jnp.int32, (blk, 2 * blk), 0)
    ki = lax.broadcasted_iota(jnp.int32, (blk, 2 * blk), 1)
    rel = qi + blk - ki
    valid = jnp.abs(2 * rel - band) <= band
    bias = bias_ref[0, 0]
    scale = HEAD_DIM ** -0.5

    def attend(start, kk, vv, bias_blk, valid_blk):
        q = q_ref[0, pl.ds(start, blk), :].astype(BF16)
        s = _dot_nt(q, kk) * scale + bias_blk
        s = jnp.where(valid_blk, s, NEG_INF)
        m = jnp.max(s, axis=-1, keepdims=True)
        p = jnp.exp(s - m)
        l = jnp.sum(p, axis=-1, keepdims=True)
        o = _dot(p.astype(BF16), vv) / l
        o_ref[0, pl.ds(start, blk), :] = o
        lse_ref[0, pl.ds(start, blk), :] = jnp.broadcast_to(m + jnp.log(l), (blk, HEAD_DIM))

    attend(0, k_ref[0, 0:blk, :].astype(BF16), v_ref[0, 0:blk, :].astype(BF16),
           bias[:, blk:], valid[:, blk:])

    def body(n, carry):
        prev = pl.multiple_of((n - 1) * blk, blk)
        kk = k_ref[0, pl.ds(prev, 2 * blk), :].astype(BF16)
        vv = v_ref[0, pl.ds(prev, 2 * blk), :].astype(BF16)
        attend(pl.multiple_of(n * blk, blk), kk, vv, bias, valid)
        return carry

    lax.fori_loop(1, n_blocks, body, 0)


def _dilated_branch(proj, bias, bsz, seq, heads, proj_width, window, dilation):
    L = seq // dilation
    band = window // dilation
    assert L % A_BLOCK == 0 and band <= A_BLOCK
    cb = proj_width // HEAD_DIM
    view = proj.reshape(bsz, L, dilation * proj_width)

    def in_spec(group):
        return pl.BlockSpec((1, L, HEAD_DIM), lambda b, h, r, g=group: (b, 0, r * cb + g * heads + h))

    out_spec = pl.BlockSpec((1, L, HEAD_DIM), lambda b, h, r: (b, 0, r * heads + h))
    out_sds = jax.ShapeDtypeStruct((bsz, L, dilation * heads * HEAD_DIM), F32)
    o, lse = pl.pallas_call(
        functools.partial(_dilated_branch_kernel, n_blocks=L // A_BLOCK, band=band),
        grid=(bsz, heads, dilation),
        in_specs=[in_spec(0), in_spec(1), in_spec(2),
                  pl.BlockSpec((1, 1, A_BLOCK, 2 * A_BLOCK), lambda b, h, r: (0, h, 0, 0))],
        out_specs=[out_spec, out_spec],
        out_shape=[out_sds, out_sds],
        compiler_params=_params("parallel", "parallel", "parallel"),
        name=f"dilated_attn_d{dilation}",
    )(view, view, view, bias)
    width = heads * HEAD_DIM
    return o.reshape(bsz * seq, width), lse.reshape(bsz * seq, width)


def _branch_merge_kernel(*refs):
    n = (len(refs) - 1) // 2
    o_refs, l_refs, out_ref = refs[:n], refs[n:2 * n], refs[-1]
    lses = [r[...] for r in l_refs]
    m = functools.reduce(jnp.maximum, lses)
    es = [jnp.exp(l - m) for l in lses]
    den = functools.reduce(lambda a, b: a + b, es)
    num = functools.reduce(lambda a, b: a + b, [e * r[...] for e, r in zip(es, o_refs)])
    out_ref[...] = (num / den).astype(out_ref.dtype)


def _branch_merge(outs, lses, tm=512):
    m, w = outs[0].shape
    tm = min(tm, m)
    spec = pl.BlockSpec((tm, w), lambda i: (i, 0))
    return pl.pallas_call(
        _branch_merge_kernel,
        grid=(m // tm,),
        in_specs=[spec] * (2 * len(outs)),
        out_specs=spec,
        out_shape=jax.ShapeDtypeStruct((m, w), BF16),
        compiler_params=_params("parallel"),
        name="dilated_merge",
    )(*outs, *lses)


def _dilated_bias_tables(rel_bias):
    qi = np.arange(A_BLOCK)[:, None]
    ki = np.arange(2 * A_BLOCK)[None, :]
    rel = qi + A_BLOCK - ki
    tabs = []
    for _, dilation in A_PATTERNS:
        bucket = _t5_bucket(np.clip(rel, 0, None) * dilation)
        tabs.append(rel_bias.astype(F32)[bucket].transpose(2, 0, 1))
    return jnp.stack(tabs, axis=0)


def _hgrn_levels(t):
    cs = []
    c = t // 2
    while c >= 1:
        cs.append(c)
        c //= 2
    return cs


def _hgrn_consts(t):
    idx = np.arange(t)
    ltri = (idx[:, None] >= idx[None, :]).astype(np.float32)
    sels = []
    for c in _hgrn_levels(t):
        ref = (idx // (2 * c)) * (2 * c) + c - 1
        sels.append((idx[None, :] == ref[:, None]).astype(np.float32))
    return jnp.asarray(ltri, BF16), jnp.asarray(np.concatenate(sels, axis=0), BF16)


def _hgrn_kernel(q_ref, f_ref, i_ref, g_ref, lb_ref, gn_ref, ltri_ref, sel_ref, o_ref, state_ref,
                 *, n_chunks):
    t = HGRN_CHUNK
    dk = HEAD_DIM
    levels = _hgrn_levels(t)
    lb = lb_ref[0]
    gn = gn_ref[0]
    ltri = ltri_ref[...]
    sel = sel_ref[...]
    row = lax.broadcasted_iota(jnp.int32, (t, dk), 0)
    ti = lax.broadcasted_iota(jnp.int32, (t, t), 0)
    si = lax.broadcasted_iota(jnp.int32, (t, t), 1)
    state_ref[...] = jnp.zeros_like(state_ref)

    def sum3(y):
        return y[:, :dk] + y[:, dk:2 * dk] + y[:, 2 * dk:]

    def chunk(n, carry):
        st = pl.multiple_of(n * t, t)
        fq = _silu(q_ref[0, pl.ds(st, t), :])
        forget = lb + (1.0 - lb) * _sigmoid(f_ref[0, pl.ds(st, t), :])
        logf = jnp.log(forget)
        key = 1.0 - forget
        val = i_ref[0, pl.ds(st, t), :].astype(BF16)
        a = sum3(_dot(ltri, _split3(logf, 1)))
        aref_all = sum3(_dot(sel, _split3(a, 1)))
        fq_b = fq.astype(BF16)
        key_b = key.astype(BF16)
        scores = jnp.where(ti == si, _dot_nt(fq_b, key_b), 0.0)
        for lvl, c in enumerate(levels):
            aref = aref_all[lvl * t:(lvl + 1) * t, :]
            upper = (row & (2 * c - 1)) >= c
            ql = jnp.where(upper, fq * jnp.exp(jnp.minimum(a - aref, 0.0)), 0.0).astype(BF16)
            kl = jnp.where(upper, 0.0, key * jnp.exp(jnp.minimum(aref - a, 0.0))).astype(BF16)
            s_l = _dot_nt(ql, kl)
            if 2 * c < t:
                sh = (2 * c).bit_length() - 1
                s_l = jnp.where((ti >> sh) == (si >> sh), s_l, 0.0)
            scores = scores + s_l
        state_t = state_ref[...]
        inter = _dot_nt((fq * jnp.exp(a)).astype(BF16), state_t.astype(BF16))
        o = inter + _dot(scores.astype(BF16), val)
        a_last = a[t - 1:t, :]
        k_dec = (key * jnp.exp(a_last - a)).astype(BF16)
        state_ref[...] = state_t * jnp.exp(a_last) + _dot_tn(val, k_dec)
        o = o * lax.rsqrt(jnp.mean(o * o, axis=-1, keepdims=True) + RMS_EPS) * gn
        o = o * _silu(g_ref[0, pl.ds(st, t), :])
        o_ref[0, pl.ds(st, t), :] = o.astype(o_ref.dtype)
        return carry

    lax.fori_loop(0, n_chunks, chunk, 0)


def _hgrn2(proj, lb, gn_g, bsz, seq, heads, proj_width, first_group):
    assert seq % HGRN_CHUNK == 0
    view = proj.reshape(bsz, seq, proj_width)
    ltri, sel = _hgrn_consts(HGRN_CHUNK)

    def in_spec(group):
        return pl.BlockSpec((1, seq, HEAD_DIM), lambda b, h, g=group: (b, 0, (first_group + g) * heads + h))

    head_vec = pl.BlockSpec((1, 1, HEAD_DIM), lambda b, h: (h, 0, 0))
    o = pl.pallas_call(
        functools.partial(_hgrn_kernel, n_chunks=seq // HGRN_CHUNK),
        grid=(bsz, heads),
        in_specs=[in_spec(0), in_spec(1), in_spec(2), in_spec(3), head_vec, head_vec,
                  pl.BlockSpec(ltri.shape, lambda b, h: (0, 0)),
                  pl.BlockSpec(sel.shape, lambda b, h: (0, 0))],
        out_specs=pl.BlockSpec((1, seq, HEAD_DIM), lambda b, h: (b, 0, h)),
        out_shape=jax.ShapeDtypeStruct((bsz, seq, heads * HEAD_DIM), BF16),
        scratch_shapes=[pltpu.VMEM((HEAD_DIM, HEAD_DIM), F32)],
        compiler_params=_params("parallel", "parallel"),
        name="hgrn2",
    )(view, view, view, view, lb.reshape(heads, 1, HEAD_DIM).astype(F32),
      gn_g.reshape(heads, 1, HEAD_DIM).astype(F32), ltri, sel)
    return o.reshape(bsz * seq, heads * HEAD_DIM)


def _fox_gate_kernel(x_ref, w_ref, wt_ref, b_ref, bt_ref, ltri_ref, utri_ref, cq_ref, ck_ref,
                     carry_q, carry_k):
    heads = w_ref.shape[1]

    @pl.when(pl.program_id(1) == 0)
    def _():
        carry_q[...] = jnp.zeros_like(carry_q)
        carry_k[...] = jnp.zeros_like(carry_k)

    def log_sigmoid(z):
        return jnp.minimum(z, 0.0) - jnp.log(1.0 + jnp.exp(-jnp.abs(z)))

    x = x_ref[0]
    xh = x.astype(BF16)
    xl = (x - xh.astype(F32)).astype(BF16)
    w = w_ref[...]
    wh = w.astype(BF16)
    wl = (w - wh.astype(F32)).astype(BF16)
    wt = wt_ref[...]
    wth = wt.astype(BF16)
    wtl = (wt - wth.astype(F32)).astype(BF16)
    f_col = _dot(xh, wh) + _dot(xh, wl) + _dot(xl, wh) + b_ref[...]
    f_row = _dot_nt(wth, xh) + _dot_nt(wtl, xh) + _dot_nt(wth, xl) + bt_ref[...]
    ls_col = log_sigmoid(f_col)
    ls_row = log_sigmoid(f_row)
    c_col = _dot(ltri_ref[...], _split3(ls_col, 1))
    c_col = c_col[:, :heads] + c_col[:, heads:2 * heads] + c_col[:, 2 * heads:] + carry_q[...]
    c_row = _dot(_split3(ls_row, 0), utri_ref[...])
    c_row = c_row[:heads] + c_row[heads:2 * heads] + c_row[2 * heads:] + carry_k[...]
    cq_ref[0] = c_col
    ck_ref[0] = c_row
    ts = c_col.shape[0]
    carry_q[...] = c_col[ts - 1:ts, :]
    carry_k[...] = c_row[:, ts - 1:ts]


def _fox_gate_cumsum(x, w_f, b_f, bsz, seq, ts=512):
    d = x.shape[-1]
    heads = w_f.shape[1]
    ts = min(ts, seq)
    idx = np.arange(ts)
    ltri = jnp.asarray(idx[:, None] >= idx[None, :], BF16)
    utri = jnp.asarray(idx[:, None] <= idx[None, :], BF16)
    const = lambda shape: pl.BlockSpec(shape, lambda b, i: (0,) * len(shape))
    cq, ck = pl.pallas_call(
        _fox_gate_kernel,
        grid=(bsz, seq // ts),
        in_specs=[pl.BlockSpec((1, ts, d), lambda b, i: (b, i, 0)),
                  const((d, heads)), const((heads, d)), const((1, heads)), const((heads, 1)),
                  const((ts, ts)), const((ts, ts))],
        out_specs=[pl.BlockSpec((1, ts, heads), lambda b, i: (b, i, 0)),
                   pl.BlockSpec((1, heads, ts), lambda b, i: (b, 0, i))],
        out_shape=[jax.ShapeDtypeStruct((bsz, seq, heads), F32),
                   jax.ShapeDtypeStruct((bsz, heads, seq), F32)],
        scratch_shapes=[pltpu.VMEM((1, heads), F32), pltpu.VMEM((heads, 1), F32)],
        compiler_params=_params("parallel", "arbitrary"),
        name="fox_gate_cumsum",
    )(x.reshape(bsz, seq, d), w_f.astype(F32), w_f.astype(F32).T, b_f.astype(F32).reshape(1, heads),
      b_f.astype(F32).reshape(heads, 1), ltri, utri)
    return cq, ck.reshape(bsz, heads, 1, seq)


def _fox_kernel(q_ref, k_ref, v_ref, cq_ref, ck_ref, o_ref):
    tq = FOX_TILE
    h = pl.program_id(1)
    i = pl.program_id(2)
    q = q_ref[0]
    cq_all = cq_ref[0]
    lane = lax.broadcasted_iota(jnp.int32, cq_all.shape, 1)
    cq = jnp.sum(jnp.where(lane == h, cq_all, 0.0), axis=1, keepdims=True)
    scale = HEAD_DIM ** -0.5

    def step(j, carry, diagonal):
        m, l, acc = carry
        st = pl.multiple_of(j * tq, tq)
        kk = k_ref[0, pl.ds(st, tq), :]
        vv = v_ref[0, pl.ds(st, tq), :]
        ck = ck_ref[0, 0, :, pl.ds(st, tq)]
        s = _dot_nt(q, kk) * scale + (cq - ck)
        if diagonal:
            r = lax.broadcasted_iota(jnp.int32, (tq, tq), 0)
            c = lax.broadcasted_iota(jnp.int32, (tq, tq), 1)
            s = jnp.where(c <= r, s, NEG_INF)
        m_new = jnp.maximum(m, jnp.max(s, axis=-1, keepdims=True))
        alpha = jnp.exp(m - m_new)
        p = jnp.exp(s - m_new)
        l = alpha * l + jnp.sum(p, axis=-1, keepdims=True)
        acc = alpha * acc + _dot(p.astype(BF16), vv)
        return m_new, l, acc

    init = (jnp.full((tq, 1), NEG_INF, F32), jnp.zeros((tq, 1), F32), jnp.zeros((tq, HEAD_DIM), F32))
    carry = lax.fori_loop(0, i, lambda j, c: step(j, c, False), init)
    _, l, acc = step(i, carry, True)
    o_ref[0] = (acc / l).astype(o_ref.dtype)


def _fox_attention(qkv, cq, ck, bsz, seq, heads):
    assert seq % FOX_TILE == 0
    width = heads * HEAD_DIM
    view = qkv.reshape(bsz, seq, 3 * width)
    o = pl.pallas_call(
        _fox_kernel,
        grid=(bsz, heads, seq // FOX_TILE),
        in_specs=[pl.BlockSpec((1, FOX_TILE, HEAD_DIM), lambda b, h, i: (b, i, h)),
                  pl.BlockSpec((1, seq, HEAD_DIM), lambda b, h, i: (b, 0, heads + h)),
                  pl.BlockSpec((1, seq, HEAD_DIM), lambda b, h, i: (b, 0, 2 * heads + h)),
                  pl.BlockSpec((1, FOX_TILE, heads), lambda b, h, i: (b, i, 0)),
                  pl.BlockSpec((1, 1, 1, seq), lambda b, h, i: (b, h, 0, 0))],
        out_specs=pl.BlockSpec((1, FOX_TILE, HEAD_DIM), lambda b, h, i: (b, i, h)),
        out_shape=jax.ShapeDtypeStruct((bsz, seq, width), BF16),
        compiler_params=_params("parallel", "parallel", "parallel"),
        name="fox_attention",
    )(view, view, view, cq, ck)
    return o.reshape(bsz * seq, width)


def _router_kernel(x_ref, wt_ref, bias_ref, utri_ref, eidx_ref, rank_ref, w_ref, cnt_ref, carry):
    gsz = GROUP_SIZE
    tt = x_ref.shape[0]

    @pl.when(pl.program_id(0) == 0)
    def _():
        carry[...] = jnp.zeros_like(carry)

    x = x_ref[...]
    xh = x.astype(BF16)
    xl = (x - xh.astype(F32)).astype(BF16)
    wt = wt_ref[...]
    wh = wt.astype(BF16)
    wl = (wt - wh.astype(F32)).astype(BF16)
    logits = _dot_nt(wh, xh) + _dot_nt(wh, xl) + _dot_nt(wl, xh)
    scores = _sigmoid(logits)
    sel = scores + bias_ref[...]
    sub = lax.broadcasted_iota(jnp.int32, (gsz, tt), 0)
    neg = -jnp.inf

    group_score = []
    for g in range(N_GROUPS):
        blk = sel[g * gsz:(g + 1) * gsz, :]
        m1 = jnp.max(blk, axis=0, keepdims=True)
        first = jnp.min(jnp.where(blk == m1, sub, gsz), axis=0, keepdims=True)
        m2 = jnp.max(jnp.where(sub == first, neg, blk), axis=0, keepdims=True)
        group_score.append(m1 + m2)

    chosen = [jnp.zeros((1, tt), F32) for _ in range(N_GROUPS)]
    for _ in range(TOPK_GROUPS):
        m = functools.reduce(jnp.maximum, group_score)
        gi = jnp.full((1, tt), N_GROUPS, jnp.int32)
        for g in reversed(range(N_GROUPS)):
            gi = jnp.where(group_score[g] == m, g, gi)
        for g in range(N_GROUPS):
            hit = gi == g
            chosen[g] = jnp.where(hit, 1.0, chosen[g])
            group_score[g] = jnp.where(hit, neg, group_score[g])

    cand = [jnp.where(jnp.broadcast_to(chosen[g], (gsz, tt)) > 0.5, sel[g * gsz:(g + 1) * gsz, :], NEG_INF)
            for g in range(N_GROUPS)]
    score_g = [scores[g * gsz:(g + 1) * gsz, :] for g in range(N_GROUPS)]
    eiota = [sub + g * gsz for g in range(N_GROUPS)]
    picked = [jnp.zeros((gsz, tt), F32) for _ in range(N_GROUPS)]
    top_idx, top_score = [], []
    for _ in range(TOP_K):
        m = functools.reduce(jnp.maximum, [jnp.max(c, axis=0, keepdims=True) for c in cand])
        ei = functools.reduce(jnp.minimum, [
            jnp.min(jnp.where(cand[g] == m, eiota[g], N_EXPERTS), axis=0, keepdims=True)
            for g in range(N_GROUPS)])
        sc = jnp.zeros((1, tt), F32)
        for g in range(N_GROUPS):
            hit = eiota[g] == ei
            picked[g] = jnp.where(hit, 1.0, picked[g])
            cand[g] = jnp.where(hit, neg, cand[g])
            sc = sc + jnp.sum(jnp.where(hit, score_g[g], 0.0), axis=0, keepdims=True)
        top_idx.append(ei)
        top_score.append(sc)

    total = functools.reduce(lambda a, b: a + b, top_score)
    mask = jnp.concatenate(picked, axis=0)
    incl = _dot(mask.astype(BF16), utri_ref[...])
    rank = carry[...] + incl - mask
    for r in range(TOP_K):
        ei = top_idx[r]
        rk = jnp.zeros((1, tt), F32)
        for g in range(N_GROUPS):
            rk = rk + jnp.sum(jnp.where(eiota[g] == ei, rank[g * gsz:(g + 1) * gsz, :], 0.0),
                              axis=0, keepdims=True)
        eidx_ref[r:r + 1, :] = ei
        rank_ref[r:r + 1, :] = rk.astype(jnp.int32)
        w_ref[r:r + 1, :] = top_score[r] / total * ROUTED_SCALE
    new_carry = carry[...] + incl[:, tt - 1:tt]
    carry[...] = new_carry
    cnt_ref[...] = jnp.broadcast_to(new_carry, cnt_ref.shape)


def _router(x, router_w, router_bias):
    n, d = x.shape
    tt = min(ROUTER_TILE, n)
    idx = np.arange(tt)
    utri = jnp.asarray(idx[:, None] <= idx[None, :], BF16)
    eidx, rank, w8, cnt = pl.pallas_call(
        _router_kernel,
        grid=(n // tt,),
        in_specs=[pl.BlockSpec((tt, d), lambda i: (i, 0)),
                  pl.BlockSpec((N_EXPERTS, d), lambda i: (0, 0)),
                  pl.BlockSpec((N_EXPERTS, 1), lambda i: (0, 0)),
                  pl.BlockSpec((tt, tt), lambda i: (0, 0))],
        out_specs=[pl.BlockSpec((TOP_K, tt), lambda i: (0, i)),
                   pl.BlockSpec((TOP_K, tt), lambda i: (0, i)),
                   pl.BlockSpec((TOP_K, tt), lambda i: (0, i)),
                   pl.BlockSpec((N_EXPERTS, 128), lambda i: (0, 0))],
        out_shape=[jax.ShapeDtypeStruct((TOP_K, n), jnp.int32),
                   jax.ShapeDtypeStruct((TOP_K, n), jnp.int32),
                   jax.ShapeDtypeStruct((TOP_K, n), F32),
                   jax.ShapeDtypeStruct((N_EXPERTS, 128), F32)],
        scratch_shapes=[pltpu.VMEM((N_EXPERTS, 1), F32)],
        compiler_params=_params("arbitrary"),
        name="moe_router",
    )(x, router_w.astype(F32).T, router_bias.astype(F32).reshape(N_EXPERTS, 1), utri)
    return eidx, rank, w8, cnt[:, 0].astype(jnp.int32)


def _dispatch_kernel(dest_ref, x_ref, xs_hbm, sem):
    tt = x_ref.shape[0]

    def copy(t, k):
        return pltpu.make_async_copy(x_ref.at[pl.ds(t, 1), :],
                                     xs_hbm.at[pl.ds(dest_ref[0, k, t], 1), :], sem)

    def issue(t, carry):
        for k in range(TOP_K):
            copy(t, k).start()
        return carry

    def drain(t, carry):
        for k in range(TOP_K):
            copy(t, k).wait()
        return carry

    lax.fori_loop(0, tt, issue, 0)
    lax.fori_loop(0, tt, drain, 0)


def _dispatch(x, dest_tiles, n_rows):
    n, d = x.shape
    tt = dest_tiles.shape[2]
    return pl.pallas_call(
        _dispatch_kernel,
        grid=(n // tt,),
        in_specs=[pl.BlockSpec((1, TOP_K, tt), lambda i: (i, 0, 0), memory_space=pltpu.SMEM),
                  pl.BlockSpec((tt, d), lambda i: (i, 0))],
        out_specs=pl.BlockSpec(memory_space=pl.ANY),
        out_shape=jax.ShapeDtypeStruct((n_rows, d), x.dtype),
        scratch_shapes=[pltpu.SemaphoreType.DMA(())],
        compiler_params=_params("arbitrary"),
        name="moe_dispatch",
    )(dest_tiles, x)


def _expert_kernel(te_ref, tv_ref, nt_ref, xs_ref, wg_ref, wu_ref, wd_ref, y_ref):
    i = pl.program_id(0)

    @pl.when(i < nt_ref[0])
    def _():
        tm = xs_ref.shape[0]
        row = lax.broadcasted_iota(jnp.int32, (tm, 1), 0)
        x = jnp.where(row < tv_ref[i], xs_ref[...], 0.0).astype(BF16)
        gate = _dot(x, wg_ref[0])
        up = _dot(x, wu_ref[0])
        hidden = (_silu(gate) * up).astype(BF16)
        y_ref[...] = _dot(hidden, wd_ref[0])


def _expert_ffn(xs, w_gate, w_up, w_down, tile_expert, tile_valid, num_tiles):
    p, d = xs.shape
    tm = EXPERT_TILE
    e_dim = w_gate.shape[2]
    row_map = lambda i, te, tv, nt: (jnp.minimum(i, nt[0] - 1), 0)
    w_map = lambda i, te, tv, nt: (te[i], 0, 0)
    return pl.pallas_call(
        _expert_kernel,
        grid_spec=pltpu.PrefetchScalarGridSpec(
            num_scalar_prefetch=3,
            grid=(p // tm,),
            in_specs=[pl.BlockSpec((tm, d), row_map),
                      pl.BlockSpec((1, d, e_dim), w_map),
                      pl.BlockSpec((1, d, e_dim), w_map),
                      pl.BlockSpec((1, e_dim, d), w_map)],
            out_specs=pl.BlockSpec((tm, d), row_map)),
        out_shape=jax.ShapeDtypeStruct((p, d), F32),
        compiler_params=_params("arbitrary"),
        name="moe_expert_ffn",
    )(tile_expert, tile_valid, num_tiles, xs, w_gate, w_up, w_down)


def _combine_kernel(dest_ref, x_ref, w8_ref, sg_ref, su_ref, sd_ref, g_ref, b_ref, y_hbm,
                    o_ref, obf_ref, ybuf, sem):
    tc = x_ref.shape[0]

    def copy(t, k):
        return pltpu.make_async_copy(y_hbm.at[pl.ds(dest_ref[0, k, t], 1), :],
                                     ybuf.at[k, pl.ds(t, 1), :], sem)

    def issue(t, carry):
        for k in range(TOP_K):
            copy(t, k).start()
        return carry

    def drain(t, carry):
        for k in range(TOP_K):
            copy(t, k).wait()
        return carry

    lax.fori_loop(0, tc, issue, 0)
    x = x_ref[...]
    xb = x.astype(BF16)
    hidden = (_silu(_dot(xb, sg_ref[...])) * _dot(xb, su_ref[...])).astype(BF16)
    acc = _dot(hidden, sd_ref[...])
    lax.fori_loop(0, tc, drain, 0)
    w8 = w8_ref[...]
    for k in range(TOP_K):
        acc = acc + w8[:, k:k + 1] * ybuf[k]
    y = _layer_norm_rows(DN_ALPHA * x + acc, g_ref[...], b_ref[...])
    o_ref[...] = y
    obf_ref[...] = y.astype(BF16)


def _combine(x, w8_rows, dest_tiles, y_sorted, s_gate, s_up, s_down, g, b):
    n, d = x.shape
    tc = dest_tiles.shape[2]
    sdim = s_gate.shape[1]
    const = lambda shape: pl.BlockSpec(shape, lambda i: (0,) * len(shape))
    row = pl.BlockSpec((tc, d), lambda i: (i, 0))
    return pl.pallas_call(
        _combine_kernel,
        grid=(n // tc,),
        in_specs=[pl.BlockSpec((1, TOP_K, tc), lambda i: (i, 0, 0), memory_space=pltpu.SMEM),
                  row, pl.BlockSpec((tc, TOP_K), lambda i: (i, 0)),
                  const((d, sdim)), const((d, sdim)), const((sdim, d)), const((1, d)), const((1, d)),
                  pl.BlockSpec(memory_space=pl.ANY)],
        out_specs=[row, row],
        out_shape=[jax.ShapeDtypeStruct((n, d), F32), jax.ShapeDtypeStruct((n, d), BF16)],
        scratch_shapes=[pltpu.VMEM((TOP_K, tc, d), F32), pltpu.SemaphoreType.DMA(())],
        compiler_params=_params("arbitrary"),
        name="moe_combine",
    )(dest_tiles, x, w8_rows, s_gate, s_up, s_down, g.reshape(1, d), b.reshape(1, d), y_sorted)


def _tile_major(a, tile):
    k, n = a.shape
    return a.reshape(k, n // tile, tile).transpose(1, 0, 2)


def _moe(x, router_w, router_bias, w_gate, w_up, w_down, s_gate, s_up, s_down, ln_g, ln_b):
    n, d = x.shape
    tm = EXPERT_TILE
    eidx, rank, w8, cnt = _router(x, router_w, router_bias)
    tiles_e = (cnt + tm - 1) // tm
    tile_end = jnp.cumsum(tiles_e)
    tile_start = tile_end - tiles_e
    row_start = tile_start * tm
    e_ids = jnp.arange(N_EXPERTS, dtype=jnp.int32)
    dest = jnp.sum(jnp.where(eidx[None] == e_ids[:, None, None], row_start[:, None, None], 0), axis=0) + rank
    max_tiles = (n * TOP_K) // tm + N_EXPERTS
    num_tiles = tile_end[-1]
    tile_ids = jnp.arange(max_tiles, dtype=jnp.int32)
    t_clamped = jnp.minimum(tile_ids, num_tiles - 1)
    tile_expert = jnp.sum(t_clamped[:, None] >= tile_end[None, :], axis=1).astype(jnp.int32)
    tile_valid = jnp.clip(cnt[tile_expert] - (t_clamped - tile_start[tile_expert]) * tm, 0, tm).astype(jnp.int32)

    xs = _dispatch(x, _tile_major(dest, min(DISPATCH_TILE, n)), max_tiles * tm)
    ys = _expert_ffn(xs, w_gate, w_up, w_down, tile_expert, tile_valid,
                     num_tiles.reshape(1).astype(jnp.int32))
    return _combine(x, w8.T, _tile_major(dest, min(COMBINE_TILE, n)), ys, s_gate, s_up, s_down, ln_g, ln_b)


def _even_layer(x, x_bf, bsz, seq, w_in, gn_g, w_out, rel_bias, lb, ln_g, ln_b):
    heads = w_out.shape[0] // (2 * HEAD_DIM)
    width = heads * HEAD_DIM
    proj_width = w_in.shape[1]
    proj = _matmul(x_bf, w_in, F32)
    bias = _dilated_bias_tables(rel_bias)
    outs, lses = [], []
    for bi, (window, dilation) in enumerate(A_PATTERNS):
        o, lse = _dilated_branch(proj, bias[bi:bi + 1], bsz, seq, heads, proj_width, window, dilation)
        outs.append(o)
        lses.append(lse)
    o_a = _branch_merge(outs, lses)
    o_b = _hgrn2(proj, lb, gn_g, bsz, seq, heads, proj_width, first_group=3)
    return _proj_residual_ln([o_a, o_b], [w_out[:width], w_out[width:]], x, ln_g, ln_b)


def _odd_layer(x, x_bf, bsz, seq, w_qkv, w_f, b_f, w_out, ln_g, ln_b):
    heads = w_f.shape[1]
    qkv = _matmul(x_bf, w_qkv, BF16)
    cq, ck = _fox_gate_cumsum(x, w_f, b_f, bsz, seq)
    o = _fox_attention(qkv, cq, ck, bsz, seq, heads)
    return _proj_residual_ln([o], [w_out], x, ln_g, ln_b)


def kernel(x, rel_bias, hgrn_lb_logits, a_w_in, a_gn_g, a_w_out, c_w_in, c_b_f, c_w_out, ln_mix_g, ln_mix_b, ln_ffn_g, ln_ffn_b, router_w, router_bias, exp_w_gate, exp_w_up, exp_w_down, sh_w_gate, sh_w_up, sh_w_down):
    bsz, seq, d = x.shape
    depth = ln_mix_g.shape[0]
    lb_all = jnp.cumsum(jax.nn.softmax(hgrn_lb_logits.astype(F32), axis=0), axis=0)
    h = x.reshape(bsz * seq, d).astype(F32)
    h_bf = h.astype(BF16)
    for layer in range(depth):
        j = layer // 2
        if layer % 2 == 0:
            h, h_bf = _even_layer(h, h_bf, bsz, seq, a_w_in[j].astype(BF16), a_gn_g[j],
                                  a_w_out[j].astype(BF16), rel_bias, lb_all[layer],
                                  ln_mix_g[layer], ln_mix_b[layer])
        else:
            c_width = c_w_out.shape[1]
            w_in = c_w_in[j]
            h, h_bf = _odd_layer(h, h_bf, bsz, seq, w_in[:, :3 * c_width].astype(BF16),
                                 w_in[:, 3 * c_width:], c_b_f[j], c_w_out[j].astype(BF16),
                                 ln_mix_g[layer], ln_mix_b[layer])
        h, h_bf = _moe(h, router_w[layer], router_bias[layer], exp_w_gate[layer].astype(BF16),
                       exp_w_up[layer].astype(BF16), exp_w_down[layer].astype(BF16),
                       sh_w_gate[layer].astype(BF16), sh_w_up[layer].astype(BF16),
                       sh_w_down[layer].astype(BF16), ln_ffn_g[layer], ln_ffn_b[layer])
    return h.reshape(bsz, seq, d)
```

```python
import functools

import numpy as np
import jax
import jax.numpy as jnp
from jax import lax
from jax.experimental import pallas as pl
from jax.experimental.pallas import tpu as pltpu

F32 = jnp.float32
BF16 = jnp.bfloat16

HEAD_DIM = 128
A_PATTERNS = ((128, 1), (512, 4), (2048, 16))
A_BLOCK = 128
REL_BUCKETS = 32
REL_MAX_DIST = 2048
N_EXPERTS = 64
TOP_K = 8
N_GROUPS = 8
GROUP_SIZE = N_EXPERTS // N_GROUPS
TOPK_GROUPS = 4
ROUTED_SCALE = 2.5
LN_EPS = 1e-5
RMS_EPS = 1e-6
DEPTH = 2
DN_ALPHA = (2 * DEPTH) ** 0.25
NEG_INF = -1e30

HGRN_CHUNK = 128
FOX_Q_TILE = 512
FOX_K_TILE = 256
ROUTER_TILE = 512
DISPATCH_TILE = 256
EXPERT_TILE = 512
COMBINE_TILE = 128
VMEM_LIMIT = 56 * 1024 * 1024

_NT = (((1,), (1,)), ((), ()))
_TN = (((0,), (0,)), ((), ()))


def _dot(a, b):
    return jnp.dot(a, b, preferred_element_type=F32)


def _dot_nt(a, b):
    return lax.dot_general(a, b, _NT, preferred_element_type=F32)


def _dot_tn(a, b):
    return lax.dot_general(a, b, _TN, preferred_element_type=F32)


def _sigmoid(x):
    return 1.0 / (1.0 + jnp.exp(-x))


def _silu(x):
    return x * _sigmoid(x)


def _split3(x, axis):
    hi = x.astype(BF16)
    r1 = x - hi.astype(F32)
    mid = r1.astype(BF16)
    lo = (r1 - mid.astype(F32)).astype(BF16)
    return jnp.concatenate([hi, mid, lo], axis=axis)


def _layer_norm_rows(y, g, b):
    mu = jnp.mean(y, axis=-1, keepdims=True)
    yc = y - mu
    var = jnp.mean(yc * yc, axis=-1, keepdims=True)
    return yc * lax.rsqrt(var + LN_EPS) * g + b


def _params(*sem):
    return pltpu.CompilerParams(dimension_semantics=sem, vmem_limit_bytes=VMEM_LIMIT)


def _mm_kernel(x_ref, w_ref, cs_ref, o_ref):
    o_ref[...] = (_dot(x_ref[...], w_ref[...]) * cs_ref[...]).astype(o_ref.dtype)


def _matmul(x, w, col_scale, out_dtype, tm=1024, tn=1024):
    m, k = x.shape
    nw = w.shape[1]
    tm, tn = min(tm, m), min(tn, nw)
    assert m % tm == 0 and nw % tn == 0
    return pl.pallas_call(
        _mm_kernel,
        grid=(nw // tn, m // tm),
        in_specs=[pl.BlockSpec((tm, k), lambda j, i: (i, 0)),
                  pl.BlockSpec((k, tn), lambda j, i: (0, j)),
                  pl.BlockSpec((1, tn), lambda j, i: (0, j))],
        out_specs=pl.BlockSpec((tm, tn), lambda j, i: (i, j)),
        out_shape=jax.ShapeDtypeStruct((m, nw), out_dtype),
        compiler_params=_params("parallel", "parallel"),
        name="proj_matmul",
    )(x, w, col_scale)


def _proj_ln_kernel(*refs, n_in):
    a_refs, w_refs = refs[:n_in], refs[n_in:2 * n_in]
    r_ref, g_ref, b_ref, o_ref, obf_ref = refs[2 * n_in:]
    acc = _dot(a_refs[0][...], w_refs[0][...])
    for a_ref, w_ref in zip(a_refs[1:], w_refs[1:]):
        acc = acc + _dot(a_ref[...], w_ref[...])
    y = _layer_norm_rows(DN_ALPHA * r_ref[...] + acc, g_ref[...], b_ref[...])
    o_ref[...] = y
    obf_ref[...] = y.astype(BF16)


def _proj_residual_ln(acts, weights, resid, g, b, tm=256):
    m, d = resid.shape
    tm = min(tm, m)
    n_in = len(acts)
    in_specs = [pl.BlockSpec((tm, a.shape[1]), lambda i: (i, 0)) for a in acts]
    in_specs += [pl.BlockSpec(w.shape, lambda i: (0, 0)) for w in weights]
    in_specs += [pl.BlockSpec((tm, d), lambda i: (i, 0)),
                 pl.BlockSpec((1, d), lambda i: (0, 0)),
                 pl.BlockSpec((1, d), lambda i: (0, 0))]
    return pl.pallas_call(
        functools.partial(_proj_ln_kernel, n_in=n_in),
        grid=(m // tm,),
        in_specs=in_specs,
        out_specs=[pl.BlockSpec((tm, d), lambda i: (i, 0)), pl.BlockSpec((tm, d), lambda i: (i, 0))],
        out_shape=[jax.ShapeDtypeStruct((m, d), F32), jax.ShapeDtypeStruct((m, d), BF16)],
        compiler_params=_params("parallel"),
        name="out_proj_ln",
    )(*acts, *weights, resid, g.reshape(1, d), b.reshape(1, d))


def _t5_bucket(dist):
    max_exact = REL_BUCKETS // 2
    d = np.maximum(dist, 1).astype(np.float64)
    large = max_exact + (np.log(d / max_exact) / np.log(REL_MAX_DIST / max_exact)
                         * (REL_BUCKETS - max_exact)).astype(np.int64)
    large = np.minimum(large, REL_BUCKETS - 1)
    return np.where(dist < max_exact, dist, large).astype(np.int32)


def _dilated_kernel(q_ref, k_ref, v_ref, bias_ref, out_ref, o_scr, lse_scr):
    blk = A_BLOCK
    seq = q_ref.shape[1]
    qi = lax.broadcasted_iota(jnp.int32, (blk, 2 * blk), 0)
    ki = lax.broadcasted_iota(jnp.int32, (blk, 2 * blk), 1)
    rel = qi + blk - ki

    for bi, (window, dil) in enumerate(A_PATTERNS):
        band = window // dil
        n_blocks = seq // (dil * blk)
        valid = jnp.abs(2 * rel - band) <= band
        bias = bias_ref[bi, 0]

        def attend(q_start, k_start, n_keys, bias_blk, valid_blk, bi=bi, dil=dil):
            q_rows = pl.ds(q_start, blk, stride=dil)
            k_rows = pl.ds(k_start, n_keys, stride=dil)
            q = q_ref[0, q_rows, :].astype(BF16)
            kk = k_ref[0, k_rows, :].astype(BF16)
            vv = v_ref[0, k_rows, :].astype(BF16)
            s = _dot_nt(q, kk) + bias_blk
            s = jnp.where(valid_blk, s, NEG_INF)
            m = jnp.max(s, axis=-1, keepdims=True)
            p = jnp.exp(s - m)
            l = jnp.sum(p, axis=-1, keepdims=True)
            o_scr[bi, q_rows, :] = _dot(p.astype(BF16), vv) / l
            lse_scr[bi, q_rows, :] = jnp.broadcast_to(m + jnp.log(l), (blk, HEAD_DIM))

        def first(r, carry, attend=attend, bias=bias, valid=valid):
            attend(r, r, blk, bias[:, blk:], valid[:, blk:])
            return carry

        lax.fori_loop(0, dil, first, 0)

        def rest(t, carry, attend=attend, bias=bias, valid=valid, dil=dil):
            r = t & (dil - 1)
            n = (t >> (dil.bit_length() - 1)) + 1
            attend(r + n * blk * dil, r + (n - 1) * blk * dil, 2 * blk, bias, valid)
            return carry

        lax.fori_loop(0, dil * (n_blocks - 1), rest, 0)

    chunk = 2 * blk
    nb = len(A_PATTERNS)

    def mix(c, carry):
        rows = pl.ds(pl.multiple_of(c * chunk, chunk), chunk)
        lses = [lse_scr[b, rows, :] for b in range(nb)]
        m = functools.reduce(jnp.maximum, lses)
        es = [jnp.exp(x - m) for x in lses]
        den = functools.reduce(lambda a, b: a + b, es)
        num = functools.reduce(lambda a, b: a + b, [e * o_scr[b, rows, :] for b, e in enumerate(es)])
        out_ref[0, rows, :] = (num / den).astype(out_ref.dtype)
        return carry

    lax.fori_loop(0, seq // chunk, mix, 0)


def _dilated_mixture(proj, bias, bsz, seq, heads, proj_width):
    for window, dilation in A_PATTERNS:
        assert seq % (dilation * A_BLOCK) == 0 and window // dilation <= A_BLOCK
    nb = len(A_PATTERNS)
    view = proj.reshape(bsz, seq, proj_width)

    def in_spec(group):
        return pl.BlockSpec((1, seq, HEAD_DIM), lambda b, h, g=group: (b, 0, g * heads + h))

    o = pl.pallas_call(
        _dilated_kernel,
        grid=(bsz, heads),
        in_specs=[in_spec(0), in_spec(1), in_spec(2),
                  pl.BlockSpec((nb, 1, A_BLOCK, 2 * A_BLOCK), lambda b, h: (0, h, 0, 0))],
        out_specs=pl.BlockSpec((1, seq, HEAD_DIM), lambda b, h: (b, 0, h)),
        out_shape=jax.ShapeDtypeStruct((bsz, seq, heads * HEAD_DIM), BF16),
        scratch_shapes=[pltpu.VMEM((nb, seq, HEAD_DIM), F32), pltpu.VMEM((nb, seq, HEAD_DIM), F32)],
        compiler_params=_params("parallel", "parallel"),
        name="dilated_attention",
    )(view, view, view, bias)
    return o.reshape(bsz * seq, heads * HEAD_DIM)


def _dilated_bias_tables(rel_bias):
    qi = np.arange(A_BLOCK)[:, None]
    ki = np.arange(2 * A_BLOCK)[None, :]
    rel = qi + A_BLOCK - ki
    tabs = []
    for _, dilation in A_PATTERNS:
        bucket = _t5_bucket(np.clip(rel, 0, None) * dilation)
        tabs.append(rel_bias.astype(F32)[bucket].transpose(2, 0, 1))
    return jnp.stack(tabs, axis=0)


def _hgrn_levels(t):
    cs = []
    c = t // 2
    while c >= 1:
        cs.append(c)
        c //= 2
    return cs


def _hgrn_consts(t):
    idx = np.arange(t)
    ltri = (idx[:, None] >= idx[None, :]).astype(np.float32)
    sels = []
    for c in _hgrn_levels(t):
        ref = (idx // (2 * c)) * (2 * c) + c - 1
        sels.append((idx[None, :] == ref[:, None]).astype(np.float32))
    return jnp.asarray(ltri, BF16), jnp.asarray(np.concatenate(sels, axis=0), BF16)


def _hgrn_kernel(q_ref, f_ref, i_ref, g_ref, lb_ref, gn_ref, ltri_ref, sel_ref, o_ref, state_ref,
                 *, n_chunks):
    t = HGRN_CHUNK
    dk = HEAD_DIM
    levels = _hgrn_levels(t)
    lb = lb_ref[0]
    gn = gn_ref[0]
    ltri = ltri_ref[...]
    sel = sel_ref[...]
    row = lax.broadcasted_iota(jnp.int32, (t, dk), 0)
    ti = lax.broadcasted_iota(jnp.int32, (t, t), 0)
    si = lax.broadcasted_iota(jnp.int32, (t, t), 1)
    state_ref[...] = jnp.zeros_like(state_ref)

    def sum3(y):
        return y[:, :dk] + y[:, dk:2 * dk] + y[:, 2 * dk:]

    def chunk(n, carry):
        st = pl.multiple_of(n * t, t)
        fq = _silu(q_ref[0, pl.ds(st, t), :])
        forget = lb + (1.0 - lb) * _sigmoid(f_ref[0, pl.ds(st, t), :])
        logf = jnp.log(forget)
        key = 1.0 - forget
        val = i_ref[0, pl.ds(st, t), :].astype(BF16)
        a = sum3(_dot(ltri, _split3(logf, 1)))
        aref_all = sum3(_dot(sel, _split3(a, 1)))
        fq_b = fq.astype(BF16)
        key_b = key.astype(BF16)
        scores = jnp.where(ti == si, _dot_nt(fq_b, key_b), 0.0)
        for lvl, c in enumerate(levels):
            aref = aref_all[lvl * t:(lvl + 1) * t, :]
            upper = (row & (2 * c - 1)) >= c
            ql = jnp.where(upper, fq * jnp.exp(jnp.minimum(a - aref, 0.0)), 0.0).astype(BF16)
            kl = jnp.where(upper, 0.0, key * jnp.exp(jnp.minimum(aref - a, 0.0))).astype(BF16)
            s_l = _dot_nt(ql, kl)
            if 2 * c < t:
                sh = (2 * c).bit_length() - 1
                s_l = jnp.where((ti >> sh) == (si >> sh), s_l, 0.0)
            scores = scores + s_l
        state_t = state_ref[...]
        inter = _dot_nt((fq * jnp.exp(a)).astype(BF16), state_t.astype(BF16))
        o = inter + _dot(scores.astype(BF16), val)
        a_last = a[t - 1:t, :]
        k_dec = (key * jnp.exp(a_last - a)).astype(BF16)
        state_ref[...] = state_t * jnp.exp(a_last) + _dot_tn(val, k_dec)
        o = o * lax.rsqrt(jnp.mean(o * o, axis=-1, keepdims=True) + RMS_EPS) * gn
        o = o * _silu(g_ref[0, pl.ds(st, t), :])
        o_ref[0, pl.ds(st, t), :] = o.astype(o_ref.dtype)
        return carry

    lax.fori_loop(0, n_chunks, chunk, 0)


def _hgrn2(proj, lb, gn_g, bsz, seq, heads, proj_width, first_group):
    assert seq % HGRN_CHUNK == 0
    view = proj.reshape(bsz, seq, proj_width)
    ltri, sel = _hgrn_consts(HGRN_CHUNK)

    def in_spec(group):
        return pl.BlockSpec((1, seq, HEAD_DIM), lambda b, h, g=group: (b, 0, (first_group + g) * heads + h))

    head_vec = pl.BlockSpec((1, 1, HEAD_DIM), lambda b, h: (h, 0, 0))
    o = pl.pallas_call(
        functools.partial(_hgrn_kernel, n_chunks=seq // HGRN_CHUNK),
        grid=(bsz, heads),
        in_specs=[in_spec(0), in_spec(1), in_spec(2), in_spec(3), head_vec, head_vec,
                  pl.BlockSpec(ltri.shape, lambda b, h: (0, 0)),
                  pl.BlockSpec(sel.shape, lambda b, h: (0, 0))],
        out_specs=pl.BlockSpec((1, seq, HEAD_DIM), lambda b, h: (b, 0, h)),
        out_shape=jax.ShapeDtypeStruct((bsz, seq, heads * HEAD_DIM), BF16),
        scratch_shapes=[pltpu.VMEM((HEAD_DIM, HEAD_DIM), F32)],
        compiler_params=_params("parallel", "parallel"),
        name="hgrn2",
    )(view, view, view, view, lb.reshape(heads, 1, HEAD_DIM).astype(F32),
      gn_g.reshape(heads, 1, HEAD_DIM).astype(F32), ltri, sel)
    return o.reshape(bsz * seq, heads * HEAD_DIM)


def _fox_gate_kernel(x_ref, w_ref, b_ref, ltri_ref, qx_ref, kx_ref, carry):
    heads = w_ref.shape[1]
    ts = x_ref.shape[1]

    @pl.when(pl.program_id(1) == 0)
    def _():
        carry[...] = jnp.zeros_like(carry)

    def log_sigmoid(z):
        return jnp.minimum(z, 0.0) - jnp.log(1.0 + jnp.exp(-jnp.abs(z)))

    x = x_ref[0]
    xh = x.astype(BF16)
    xl = (x - xh.astype(F32)).astype(BF16)
    w = w_ref[...]
    wh = w.astype(BF16)
    wl = (w - wh.astype(F32)).astype(BF16)
    f = _dot(xh, wh) + _dot(xh, wl) + _dot(xl, wh) + b_ref[...]
    c = _dot(ltri_ref[...], _split3(log_sigmoid(f), 1))
    c = c[:, :heads] + c[:, heads:2 * heads] + c[:, 2 * heads:] + carry[...]
    carry[...] = c[ts - 1:ts, :]
    lane = lax.broadcasted_iota(jnp.int32, (ts, HEAD_DIM), 1)
    for h in range(heads):
        ch = c[:, h:h + 1]
        hi = ch.astype(BF16).astype(F32)
        r1 = ch - hi
        mid = r1.astype(BF16).astype(F32)
        lo = r1 - mid
        qx = jnp.where(lane == 0, hi, jnp.where(lane == 1, mid, jnp.where(lane == 2, lo,
                       jnp.where(lane < 6, 1.0, 0.0))))
        kx = jnp.where(lane < 3, 1.0, jnp.where(lane == 3, -hi, jnp.where(lane == 4, -mid,
                       jnp.where(lane == 5, -lo, 0.0))))
        qx_ref[0, :, h * HEAD_DIM:(h + 1) * HEAD_DIM] = qx.astype(BF16)
        kx_ref[0, :, h * HEAD_DIM:(h + 1) * HEAD_DIM] = kx.astype(BF16)


def _fox_gate_terms(x, w_f, b_f, bsz, seq, ts=512):
    d = x.shape[-1]
    heads = w_f.shape[1]
    ts = min(ts, seq)
    idx = np.arange(ts)
    ltri = jnp.asarray(idx[:, None] >= idx[None, :], BF16)
    const = lambda shape: pl.BlockSpec(shape, lambda b, i: (0,) * len(shape))
    out_spec = pl.BlockSpec((1, ts, heads * HEAD_DIM), lambda b, i: (b, i, 0))
    out_sds = jax.ShapeDtypeStruct((bsz, seq, heads * HEAD_DIM), BF16)
    return pl.pallas_call(
        _fox_gate_kernel,
        grid=(bsz, seq // ts),
        in_specs=[pl.BlockSpec((1, ts, d), lambda b, i: (b, i, 0)),
                  const((d, heads)), const((1, heads)), const((ts, ts))],
        out_specs=[out_spec, out_spec],
        out_shape=[out_sds, out_sds],
        scratch_shapes=[pltpu.VMEM((1, heads), F32)],
        compiler_params=_params("parallel", "arbitrary"),
        name="fox_gate_cumsum",
    )(x.reshape(bsz, seq, d), w_f.astype(F32), b_f.astype(F32).reshape(1, heads), ltri)


def _fox_kernel(q_ref, qx_ref, k_ref, kx_ref, vt_ref, o_ref):
    tq, tk = FOX_Q_TILE, FOX_K_TILE
    i = pl.program_id(2)
    q = jnp.concatenate([q_ref[0], qx_ref[0]], axis=1)

    def logits(j):
        st = pl.multiple_of(j * tk, tk)
        kk = jnp.concatenate([k_ref[0, pl.ds(st, tk), :], kx_ref[0, pl.ds(st, tk), :]], axis=1)
        return _dot_nt(kk, q)

    def update(j, s, m, l, acc, diagonal):
        if diagonal:
            key = lax.broadcasted_iota(jnp.int32, (tk, tq), 0) + j * tk
            qry = lax.broadcasted_iota(jnp.int32, (tk, tq), 1) + i * tq
            s = jnp.where(key <= qry, s, NEG_INF)
        m_new = jnp.maximum(m, jnp.max(s, axis=0, keepdims=True))
        alpha = jnp.exp(m - m_new)
        p = jnp.exp(s - m_new)
        l = alpha * l + jnp.sum(p, axis=0, keepdims=True)
        vt = vt_ref[0, :, pl.ds(pl.multiple_of(j * tk, tk), tk)]
        acc = alpha * acc + _dot(vt, p.astype(BF16))
        return m_new, l, acc

    def body(j, carry):
        s, m, l, acc = carry
        s_next = logits(j + 1)
        return (s_next,) + update(j, s, m, l, acc, False)

    per = tq // tk
    first_diag = i * per
    carry = (logits(0), jnp.full((1, tq), NEG_INF, F32), jnp.zeros((1, tq), F32),
             jnp.zeros((HEAD_DIM, tq), F32))
    s, m, l, acc = lax.fori_loop(0, first_diag, body, carry)
    for d in range(per):
        s_next = logits(first_diag + d + 1) if d + 1 < per else None
        m, l, acc = update(first_diag + d, s, m, l, acc, True)
        s = s_next
    o_ref[0] = jnp.transpose(acc / l).astype(o_ref.dtype)


def _fox_attention(qkv, qx, kx, bsz, seq, heads):
    tq = FOX_Q_TILE
    assert seq % tq == 0 and tq % FOX_K_TILE == 0
    width = heads * HEAD_DIM
    view = qkv.reshape(bsz, seq, 3 * width)
    v_t = jnp.swapaxes(view[:, :, 2 * width:], 1, 2)
    q_spec = lambda off: pl.BlockSpec((1, tq, HEAD_DIM), lambda b, h, i: (b, i, off + h))
    kv_spec = lambda off: pl.BlockSpec((1, seq, HEAD_DIM), lambda b, h, i: (b, 0, off + h))
    o = pl.pallas_call(
        _fox_kernel,
        grid=(bsz, heads, seq // tq),
        in_specs=[q_spec(0), q_spec(0), kv_spec(heads), kv_spec(0),
                  pl.BlockSpec((1, HEAD_DIM, seq), lambda b, h, i: (b, h, 0))],
        out_specs=pl.BlockSpec((1, tq, HEAD_DIM), lambda b, h, i: (b, i, h)),
        out_shape=jax.ShapeDtypeStruct((bsz, seq, width), BF16),
        compiler_params=_params("parallel", "parallel", "parallel"),
        name="fox_attention",
    )(view, qx, view, kx, v_t)
    return o.reshape(bsz * seq, width)


def _router_kernel(x_ref, wt_ref, bias_ref, utri_ref, eidx_ref, rank_ref, w_ref, cnt_ref, carry):
    gsz = GROUP_SIZE
    tt = x_ref.shape[0]

    @pl.when(pl.program_id(0) == 0)
    def _():
        carry[...] = jnp.zeros_like(carry)

    x = x_ref[...]
    xh = x.astype(BF16)
    xl = (x - xh.astype(F32)).astype(BF16)
    wt = wt_ref[...]
    wh = wt.astype(BF16)
    wl = (wt - wh.astype(F32)).astype(BF16)
    logits = _dot_nt(wh, xh) + _dot_nt(wh, xl) + _dot_nt(wl, xh)
    scores = _sigmoid(logits)
    sel = scores + bias_ref[...]
    sub = lax.broadcasted_iota(jnp.int32, (gsz, tt), 0)
    neg = -jnp.inf

    group_score = []
    for g in range(N_GROUPS):
        blk = sel[g * gsz:(g + 1) * gsz, :]
        m1 = jnp.max(blk, axis=0, keepdims=True)
        first = jnp.min(jnp.where(blk == m1, sub, gsz), axis=0, keepdims=True)
        m2 = jnp.max(jnp.where(sub == first, neg, blk), axis=0, keepdims=True)
        group_score.append(m1 + m2)

    chosen = [jnp.zeros((1, tt), F32) for _ in range(N_GROUPS)]
    for _ in range(TOPK_GROUPS):
        m = functools.reduce(jnp.maximum, group_score)
        gi = jnp.full((1, tt), N_GROUPS, jnp.int32)
        for g in reversed(range(N_GROUPS)):
            gi = jnp.where(group_score[g] == m, g, gi)
        for g in range(N_GROUPS):
            hit = gi == g
            chosen[g] = jnp.where(hit, 1.0, chosen[g])
            group_score[g] = jnp.where(hit, neg, group_score[g])

    cand = [jnp.where(jnp.broadcast_to(chosen[g], (gsz, tt)) > 0.5, sel[g * gsz:(g + 1) * gsz, :], NEG_INF)
            for g in range(N_GROUPS)]
    score_g = [scores[g * gsz:(g + 1) * gsz, :] for g in range(N_GROUPS)]
    eiota = [sub + g * gsz for g in range(N_GROUPS)]
    picked = [jnp.zeros((gsz, tt), F32) for _ in range(N_GROUPS)]
    top_idx, top_score = [], []
    for _ in range(TOP_K):
        m = functools.reduce(jnp.maximum, [jnp.max(c, axis=0, keepdims=True) for c in cand])
        ei = functools.reduce(jnp.minimum, [
            jnp.min(jnp.where(cand[g] == m, eiota[g], N_EXPERTS), axis=0, keepdims=True)
            for g in range(N_GROUPS)])
        sc = jnp.zeros((1, tt), F32)
        for g in range(N_GROUPS):
            hit = eiota[g] == ei
            picked[g] = jnp.where(hit, 1.0, picked[g])
            cand[g] = jnp.where(hit, neg, cand[g])
            sc = sc + jnp.sum(jnp.where(hit, score_g[g], 0.0), axis=0, keepdims=True)
        top_idx.append(ei)
        top_score.append(sc)

    total = functools.reduce(lambda a, b: a + b, top_score)
    mask = jnp.concatenate(picked, axis=0)
    incl = _dot(mask.astype(BF16), utri_ref[...])
    rank = carry[...] + incl - mask
    for r in range(TOP_K):
        ei = top_idx[r]
        rk = jnp.zeros((1, tt), F32)
        for g in range(N_GROUPS):
            rk = rk + jnp.sum(jnp.where(eiota[g] == ei, rank[g * gsz:(g + 1) * gsz, :], 0.0),
                              axis=0, keepdims=True)
        eidx_ref[r:r + 1, :] = ei
        rank_ref[r:r + 1, :] = rk.astype(jnp.int32)
        w_ref[r:r + 1, :] = top_score[r] / total * ROUTED_SCALE
    new_carry = carry[...] + incl[:, tt - 1:tt]
    carry[...] = new_carry
    cnt_ref[...] = jnp.broadcast_to(new_carry, cnt_ref.shape)


def _router(x, router_w, router_bias):
    n, d = x.shape
    tt = min(ROUTER_TILE, n)
    idx = np.arange(tt)
    utri = jnp.asarray(idx[:, None] <= idx[None, :], BF16)
    eidx, rank, w8, cnt = pl.pallas_call(
        _router_kernel,
        grid=(n // tt,),
        in_specs=[pl.BlockSpec((tt, d), lambda i: (i, 0)),
                  pl.BlockSpec((N_EXPERTS, d), lambda i: (0, 0)),
                  pl.BlockSpec((N_EXPERTS, 1), lambda i: (0, 0)),
                  pl.BlockSpec((tt, tt), lambda i: (0, 0))],
        out_specs=[pl.BlockSpec((TOP_K, tt), lambda i: (0, i)),
                   pl.BlockSpec((TOP_K, tt), lambda i: (0, i)),
                   pl.BlockSpec((TOP_K, tt), lambda i: (0, i)),
                   pl.BlockSpec((N_EXPERTS, 128), lambda i: (0, 0))],
        out_shape=[jax.ShapeDtypeStruct((TOP_K, n), jnp.int32),
                   jax.ShapeDtypeStruct((TOP_K, n), jnp.int32),
                   jax.ShapeDtypeStruct((TOP_K, n), F32),
                   jax.ShapeDtypeStruct((N_EXPERTS, 128), F32)],
        scratch_shapes=[pltpu.VMEM((N_EXPERTS, 1), F32)],
        compiler_params=_params("arbitrary"),
        name="moe_router",
    )(x, router_w.astype(F32).T, router_bias.astype(F32).reshape(N_EXPERTS, 1), utri)
    return eidx, rank, w8, cnt[:, 0].astype(jnp.int32)


def _dispatch_kernel(dest_ref, x_ref, xs_hbm, sem):
    tt = x_ref.shape[0]

    def copy(t, k):
        return pltpu.make_async_copy(x_ref.at[pl.ds(t, 1), :],
                                     xs_hbm.at[pl.ds(dest_ref[0, k, t], 1), :], sem)

    def issue(t, carry):
        for k in range(TOP_K):
            copy(t, k).start()
        return carry

    lax.fori_loop(0, tt, issue, 0)
    rows = xs_hbm.at[pl.ds(0, TOP_K * tt), :]
    pltpu.make_async_copy(rows, rows, sem).wait()


def _dispatch(x, dest_tiles, n_rows):
    n, d = x.shape
    tt = dest_tiles.shape[2]
    return pl.pallas_call(
        _dispatch_kernel,
        grid=(n // tt,),
        in_specs=[pl.BlockSpec((1, TOP_K, tt), lambda i: (i, 0, 0), memory_space=pltpu.SMEM),
                  pl.BlockSpec((tt, d), lambda i: (i, 0))],
        out_specs=pl.BlockSpec(memory_space=pl.ANY),
        out_shape=jax.ShapeDtypeStruct((n_rows, d), x.dtype),
        scratch_shapes=[pltpu.SemaphoreType.DMA(())],
        compiler_params=_params("arbitrary"),
        name="moe_dispatch",
    )(dest_tiles, x)


def _expert_kernel(te_ref, tv_ref, nt_ref, xs_ref, wg_ref, wu_ref, wd_ref, y_ref,
                   wg_bf, wu_bf, wd_bf):
    i = pl.program_id(0)
    live = i < nt_ref[0]
    new_expert = jnp.logical_or(i == 0, te_ref[i] != te_ref[jnp.maximum(i - 1, 0)])

    @pl.when(jnp.logical_and(live, new_expert))
    def _():
        wg_bf[...] = wg_ref[0, 0].astype(BF16)
        wu_bf[...] = wu_ref[0, 0].astype(BF16)
        wd_bf[...] = wd_ref[0, 0].astype(BF16)

    @pl.when(live)
    def _():
        tm = xs_ref.shape[0]
        row = lax.broadcasted_iota(jnp.int32, (tm, 1), 0)
        x = jnp.where(row < tv_ref[i], xs_ref[...], 0.0).astype(BF16)
        gate = _dot(x, wg_bf[...])
        up = _dot(x, wu_bf[...])
        hidden = (_silu(gate) * up).astype(BF16)
        y_ref[...] = _dot(hidden, wd_bf[...])


def _expert_ffn(xs, w_gate, w_up, w_down, layer, tile_expert, tile_valid, num_tiles):
    p, d = xs.shape
    tm = EXPERT_TILE
    e_dim = w_gate.shape[3]
    row_map = lambda i, te, tv, nt: (jnp.minimum(i, nt[0] - 1), 0)
    w_map = lambda i, te, tv, nt: (layer, te[i], 0, 0)
    return pl.pallas_call(
        _expert_kernel,
        grid_spec=pltpu.PrefetchScalarGridSpec(
            num_scalar_prefetch=3,
            grid=(p // tm,),
            in_specs=[pl.BlockSpec((tm, d), row_map),
                      pl.BlockSpec((1, 1, d, e_dim), w_map),
                      pl.BlockSpec((1, 1, d, e_dim), w_map),
                      pl.BlockSpec((1, 1, e_dim, d), w_map)],
            out_specs=pl.BlockSpec((tm, d), row_map),
            scratch_shapes=[pltpu.VMEM((d, e_dim), BF16), pltpu.VMEM((d, e_dim), BF16),
                            pltpu.VMEM((e_dim, d), BF16)]),
        out_shape=jax.ShapeDtypeStruct((p, d), F32),
        compiler_params=_params("arbitrary"),
        name="moe_expert_ffn",
    )(tile_expert, tile_valid, num_tiles, xs, w_gate, w_up, w_down)


def _combine_kernel(dest_ref, x_ref, w8_ref, sg_ref, su_ref, sd_ref, g_ref, b_ref, y_hbm,
                    o_ref, obf_ref, ybuf, sem):
    tc = x_ref.shape[0]

    def copy(t, k):
        return pltpu.make_async_copy(y_hbm.at[pl.ds(dest_ref[0, k, t], 1), :],
                                     ybuf.at[k, pl.ds(t, 1), :], sem)

    def issue(t, carry):
        for k in range(TOP_K):
            copy(t, k).start()
        return carry

    lax.fori_loop(0, tc, issue, 0)
    x = x_ref[...]
    xb = x.astype(BF16)
    hidden = (_silu(_dot(xb, sg_ref[...])) * _dot(xb, su_ref[...])).astype(BF16)
    acc = _dot(hidden, sd_ref[...])
    pltpu.make_async_copy(ybuf, ybuf, sem).wait()
    w8 = w8_ref[...]
    for k in range(TOP_K):
        acc = acc + w8[:, k:k + 1] * ybuf[k]
    y = _layer_norm_rows(DN_ALPHA * x + acc, g_ref[...], b_ref[...])
    o_ref[...] = y
    obf_ref[...] = y.astype(BF16)


def _combine(x, w8_rows, dest_tiles, y_sorted, s_gate, s_up, s_down, g, b):
    n, d = x.shape
    tc = dest_tiles.shape[2]
    sdim = s_gate.shape[1]
    const = lambda shape: pl.BlockSpec(shape, lambda i: (0,) * len(shape))
    row = pl.BlockSpec((tc, d), lambda i: (i, 0))
    return pl.pallas_call(
        _combine_kernel,
        grid=(n // tc,),
        in_specs=[pl.BlockSpec((1, TOP_K, tc), lambda i: (i, 0, 0), memory_space=pltpu.SMEM),
                  row, pl.BlockSpec((tc, TOP_K), lambda i: (i, 0)),
                  const((d, sdim)), const((d, sdim)), const((sdim, d)), const((1, d)), const((1, d)),
                  pl.BlockSpec(memory_space=pl.ANY)],
        out_specs=[row, row],
        out_shape=[jax.ShapeDtypeStruct((n, d), F32), jax.ShapeDtypeStruct((n, d), BF16)],
        scratch_shapes=[pltpu.VMEM((TOP_K, tc, d), F32), pltpu.SemaphoreType.DMA(())],
        compiler_params=_params("arbitrary"),
        name="moe_combine",
    )(dest_tiles, x, w8_rows, s_gate, s_up, s_down, g.reshape(1, d), b.reshape(1, d), y_sorted)


def _tile_major(a, tile):
    k, n = a.shape
    return a.reshape(k, n // tile, tile).transpose(1, 0, 2)


def _moe(x, router_w, router_bias, w_gate, w_up, w_down, layer, s_gate, s_up, s_down, ln_g, ln_b):
    n, d = x.shape
    tm = EXPERT_TILE
    eidx, rank, w8, cnt = _router(x, router_w, router_bias)
    tiles_e = (cnt + tm - 1) // tm
    tile_end = jnp.cumsum(tiles_e)
    tile_start = tile_end - tiles_e
    row_start = tile_start * tm
    e_ids = jnp.arange(N_EXPERTS, dtype=jnp.int32)
    dest = jnp.sum(jnp.where(eidx[None] == e_ids[:, None, None], row_start[:, None, None], 0), axis=0) + rank
    max_tiles = (n * TOP_K) // tm + N_EXPERTS
    num_tiles = tile_end[-1]
    tile_ids = jnp.arange(max_tiles, dtype=jnp.int32)
    t_clamped = jnp.minimum(tile_ids, num_tiles - 1)
    tile_expert = jnp.sum(t_clamped[:, None] >= tile_end[None, :], axis=1).astype(jnp.int32)
    tile_valid = jnp.clip(cnt[tile_expert] - (t_clamped - tile_start[tile_expert]) * tm, 0, tm).astype(jnp.int32)

    xs = _dispatch(x, _tile_major(dest, min(DISPATCH_TILE, n)), max_tiles * tm)
    ys = _expert_ffn(xs, w_gate, w_up, w_down, layer, tile_expert, tile_valid,
                     num_tiles.reshape(1).astype(jnp.int32))
    return _combine(x, w8.T, _tile_major(dest, min(COMBINE_TILE, n)), ys, s_gate, s_up, s_down, ln_g, ln_b)


def _even_layer(x, x_bf, bsz, seq, w_in, gn_g, w_out, rel_bias, lb, ln_g, ln_b):
    heads = w_out.shape[0] // (2 * HEAD_DIM)
    width = heads * HEAD_DIM
    proj_width = w_in.shape[1]
    col_scale = jnp.concatenate([jnp.full((1, width), HEAD_DIM ** -0.5, F32),
                                 jnp.ones((1, proj_width - width), F32)], axis=1)
    proj = _matmul(x_bf, w_in, col_scale, F32)
    o_a = _dilated_mixture(proj, _dilated_bias_tables(rel_bias), bsz, seq, heads, proj_width)
    o_b = _hgrn2(proj, lb, gn_g, bsz, seq, heads, proj_width, first_group=3)
    return _proj_residual_ln([o_a, o_b], [w_out[:width], w_out[width:]], x, ln_g, ln_b)


def _odd_layer(x, x_bf, bsz, seq, w_qkv, w_f, b_f, w_out, ln_g, ln_b):
    heads = w_f.shape[1]
    width = heads * HEAD_DIM
    col_scale = jnp.concatenate([jnp.full((1, width), HEAD_DIM ** -0.5, F32),
                                 jnp.ones((1, 2 * width), F32)], axis=1)
    qkv = _matmul(x_bf, w_qkv, col_scale, BF16)
    qx, kx = _fox_gate_terms(x, w_f, b_f, bsz, seq)
    o = _fox_attention(qkv, qx, kx, bsz, seq, heads)
    return _proj_residual_ln([o], [w_out], x, ln_g, ln_b)


def kernel(x, rel_bias, hgrn_lb_logits, a_w_in, a_gn_g, a_w_out, c_w_in, c_b_f, c_w_out, ln_mix_g, ln_mix_b, ln_ffn_g, ln_ffn_b, router_w, router_bias, exp_w_gate, exp_w_up, exp_w_down, sh_w_gate, sh_w_up, sh_w_down):
    bsz, seq, d = x.shape
    depth = ln_mix_g.shape[0]
    lb_all = jnp.cumsum(jax.nn.softmax(hgrn_lb_logits.astype(F32), axis=0), axis=0)
    h = x.reshape(bsz * seq, d).astype(F32)
    h_bf = h.astype(BF16)
    for layer in range(depth):
        j = layer // 2
        if layer % 2 == 0:
            h, h_bf = _even_layer(h, h_bf, bsz, seq, a_w_in[j].astype(BF16), a_gn_g[j],
                                  a_w_out[j].astype(BF16), rel_bias, lb_all[layer],
                                  ln_mix_g[layer], ln_mix_b[layer])
        else:
            c_width = c_w_out.shape[1]
            w_in = c_w_in[j]
            h, h_bf = _odd_layer(h, h_bf, bsz, seq, w_in[:, :3 * c_width].astype(BF16),
                                 w_in[:, 3 * c_width:], c_b_f[j], c_w_out[j].astype(BF16),
                                 ln_mix_g[layer], ln_mix_b[layer])
        h, h_bf = _moe(h, router_w[layer], router_bias[layer], exp_w_gate, exp_w_up, exp_w_down, layer,
                       sh_w_gate[layer].astype(BF16), sh_w_up[layer].astype(BF16),
                       sh_w_down[layer].astype(BF16), ln_ffn_g[layer], ln_ffn_b[layer])
    return h.reshape(bsz, seq, d)
```

```python
import functools

import numpy as np
import jax
import jax.numpy as jnp
from jax import lax
from jax.experimental import pallas as pl
from jax.experimental.pallas import tpu as pltpu

F32 = jnp.float32
BF16 = jnp.bfloat16

HEAD_DIM = 128
A_PATTERNS = ((128, 1), (512, 4), (2048, 16))
A_BLOCK = 128
REL_BUCKETS = 32
REL_MAX_DIST = 2048
N_EXPERTS = 64
TOP_K = 8
N_GROUPS = 8
GROUP_SIZE = N_EXPERTS // N_GROUPS
TOPK_GROUPS = 4
ROUTED_SCALE = 2.5
LN_EPS = 1e-5
RMS_EPS = 1e-6
DEPTH = 2
DN_ALPHA = (2 * DEPTH) ** 0.25
NEG_INF = -1e30

DILATED_UNROLL = 8
HGRN_CHUNK = 128
HGRN_BCAST_MIN = 8
FOX_Q_TILE = 512
FOX_K_TILE = 256
ROUTER_TILE = 512
DISPATCH_TILE = 256
EXPERT_TILE = 512
COMBINE_TILE = 128
VMEM_LIMIT = 56 * 1024 * 1024

_NT = (((1,), (1,)), ((), ()))
_TN = (((0,), (0,)), ((), ()))


def _dot(a, b):
    return jnp.dot(a, b, preferred_element_type=F32)


def _dot_nt(a, b):
    return lax.dot_general(a, b, _NT, preferred_element_type=F32)


def _dot_tn(a, b):
    return lax.dot_general(a, b, _TN, preferred_element_type=F32)


def _sigmoid(x):
    return 1.0 / (1.0 + jnp.exp(-x))


def _silu(x):
    return x * _sigmoid(x)


def _split3(x, axis):
    hi = x.astype(BF16)
    r1 = x - hi.astype(F32)
    mid = r1.astype(BF16)
    lo = (r1 - mid.astype(F32)).astype(BF16)
    return jnp.concatenate([hi, mid, lo], axis=axis)


def _layer_norm_rows(y, g, b):
    mu = jnp.mean(y, axis=-1, keepdims=True)
    yc = y - mu
    var = jnp.mean(yc * yc, axis=-1, keepdims=True)
    return yc * lax.rsqrt(var + LN_EPS) * g + b


def _params(*sem):
    return pltpu.CompilerParams(dimension_semantics=sem, vmem_limit_bytes=VMEM_LIMIT)


def _mm_kernel(x_ref, w_ref, cs_ref, o_ref):
    o_ref[...] = (_dot(x_ref[...], w_ref[...]) * cs_ref[...]).astype(o_ref.dtype)


def _matmul(x, w, col_scale, out_dtype, tm=1024, tn=1024):
    m, k = x.shape
    nw = w.shape[1]
    tm, tn = min(tm, m), min(tn, nw)
    assert m % tm == 0 and nw % tn == 0
    return pl.pallas_call(
        _mm_kernel,
        grid=(nw // tn, m // tm),
        in_specs=[pl.BlockSpec((tm, k), lambda j, i: (i, 0)),
                  pl.BlockSpec((k, tn), lambda j, i: (0, j)),
                  pl.BlockSpec((1, tn), lambda j, i: (0, j))],
        out_specs=pl.BlockSpec((tm, tn), lambda j, i: (i, j)),
        out_shape=jax.ShapeDtypeStruct((m, nw), out_dtype),
        compiler_params=_params("parallel", "parallel"),
        name="proj_matmul",
    )(x, w, col_scale)


def _proj_ln_kernel(*refs, n_in):
    a_refs, w_refs = refs[:n_in], refs[n_in:2 * n_in]
    r_ref, g_ref, b_ref, o_ref, obf_ref = refs[2 * n_in:]
    acc = _dot(a_refs[0][...], w_refs[0][...])
    for a_ref, w_ref in zip(a_refs[1:], w_refs[1:]):
        acc = acc + _dot(a_ref[...], w_ref[...])
    y = _layer_norm_rows(DN_ALPHA * r_ref[...] + acc, g_ref[...], b_ref[...])
    o_ref[...] = y
    obf_ref[...] = y.astype(BF16)


def _proj_residual_ln(acts, weights, resid, g, b, tm=256):
    m, d = resid.shape
    tm = min(tm, m)
    n_in = len(acts)
    in_specs = [pl.BlockSpec((tm, a.shape[1]), lambda i: (i, 0)) for a in acts]
    in_specs += [pl.BlockSpec(w.shape, lambda i: (0, 0)) for w in weights]
    in_specs += [pl.BlockSpec((tm, d), lambda i: (i, 0)),
                 pl.BlockSpec((1, d), lambda i: (0, 0)),
                 pl.BlockSpec((1, d), lambda i: (0, 0))]
    return pl.pallas_call(
        functools.partial(_proj_ln_kernel, n_in=n_in),
        grid=(m // tm,),
        in_specs=in_specs,
        out_specs=[pl.BlockSpec((tm, d), lambda i: (i, 0)), pl.BlockSpec((tm, d), lambda i: (i, 0))],
        out_shape=[jax.ShapeDtypeStruct((m, d), F32), jax.ShapeDtypeStruct((m, d), BF16)],
        compiler_params=_params("parallel"),
        name="out_proj_ln",
    )(*acts, *weights, resid, g.reshape(1, d), b.reshape(1, d))


def _t5_bucket(dist):
    max_exact = REL_BUCKETS // 2
    d = np.maximum(dist, 1).astype(np.float64)
    large = max_exact + (np.log(d / max_exact) / np.log(REL_MAX_DIST / max_exact)
                         * (REL_BUCKETS - max_exact)).astype(np.int64)
    large = np.minimum(large, REL_BUCKETS - 1)
    return np.where(dist < max_exact, dist, large).astype(np.int32)


def _dilated_kernel(q_ref, k_ref, v_ref, bucket_ref, rel_bias_ref, out_ref, o_scr, lse_scr):
    blk = A_BLOCK
    seq = q_ref.shape[1]
    head = pl.program_id(1)
    qi = lax.broadcasted_iota(jnp.int32, (blk, 2 * blk), 0)
    ki = lax.broadcasted_iota(jnp.int32, (blk, 2 * blk), 1)
    rel = qi + blk - ki

    for bi, (window, dil) in enumerate(A_PATTERNS):
        band = window // dil
        n_blocks = seq // (dil * blk)
        valid = jnp.abs(2 * rel - band) <= band
        bucket = bucket_ref[bi]
        bias = jnp.zeros((blk, 2 * blk), F32)
        for b in range(REL_BUCKETS):
            bias = jnp.where(bucket == b, rel_bias_ref[head, b], bias)

        def attend(q_start, k_start, n_keys, bias_blk, valid_blk, bi=bi, dil=dil):
            q_rows = pl.ds(q_start, blk, stride=dil)
            k_rows = pl.ds(k_start, n_keys, stride=dil)
            q = q_ref[0, q_rows, :].astype(BF16)
            kk = k_ref[0, k_rows, :].astype(BF16)
            vv = v_ref[0, k_rows, :].astype(BF16)
            s = _dot_nt(q, kk) + bias_blk
            s = jnp.where(valid_blk, s, NEG_INF)
            m = jnp.max(s, axis=-1, keepdims=True)
            p = jnp.exp(s - m)
            l = jnp.sum(p, axis=-1, keepdims=True)
            o_scr[bi, q_rows, :] = _dot(p.astype(BF16), vv) / l
            lse_scr[bi, q_rows, :] = jnp.broadcast_to(m + jnp.log(l), (blk, HEAD_DIM))

        def first(r, carry, attend=attend, bias=bias, valid=valid):
            attend(r, r, blk, bias[:, blk:], valid[:, blk:])
            return carry

        lax.fori_loop(0, dil, first, 0, unroll=min(dil, DILATED_UNROLL))

        def rest(t, carry, attend=attend, bias=bias, valid=valid, dil=dil):
            r = t & (dil - 1)
            n = (t >> (dil.bit_length() - 1)) + 1
            attend(r + n * blk * dil, r + (n - 1) * blk * dil, 2 * blk, bias, valid)
            return carry

        lax.fori_loop(0, dil * (n_blocks - 1), rest, 0, unroll=DILATED_UNROLL)

    chunk = 2 * blk
    nb = len(A_PATTERNS)

    def mix(c, carry):
        rows = pl.ds(pl.multiple_of(c * chunk, chunk), chunk)
        lses = [lse_scr[b, rows, :] for b in range(nb)]
        m = functools.reduce(jnp.maximum, lses)
        es = [jnp.exp(x - m) for x in lses]
        den = functools.reduce(lambda a, b: a + b, es)
        num = functools.reduce(lambda a, b: a + b, [e * o_scr[b, rows, :] for b, e in enumerate(es)])
        out_ref[0, rows, :] = (num / den).astype(out_ref.dtype)
        return carry

    lax.fori_loop(0, seq // chunk, mix, 0)


def _dilated_bucket_tables():
    qi = np.arange(A_BLOCK)[:, None]
    ki = np.arange(2 * A_BLOCK)[None, :]
    rel = qi + A_BLOCK - ki
    return np.stack([_t5_bucket(np.clip(rel, 0, None) * dilation) for _, dilation in A_PATTERNS], axis=0)


def _dilated_mixture(proj, rel_bias, bsz, seq, heads, proj_width):
    for window, dilation in A_PATTERNS:
        assert seq % (dilation * A_BLOCK) == 0 and window // dilation <= A_BLOCK
    nb = len(A_PATTERNS)
    view = proj.reshape(bsz, seq, proj_width)
    buckets = jnp.asarray(_dilated_bucket_tables())

    def in_spec(group):
        return pl.BlockSpec((1, seq, HEAD_DIM), lambda b, h, g=group: (b, 0, g * heads + h))

    o = pl.pallas_call(
        _dilated_kernel,
        grid=(bsz, heads),
        in_specs=[in_spec(0), in_spec(1), in_spec(2),
                  pl.BlockSpec(buckets.shape, lambda b, h: (0, 0, 0)),
                  pl.BlockSpec(memory_space=pltpu.SMEM)],
        out_specs=pl.BlockSpec((1, seq, HEAD_DIM), lambda b, h: (b, 0, h)),
        out_shape=jax.ShapeDtypeStruct((bsz, seq, heads * HEAD_DIM), BF16),
        scratch_shapes=[pltpu.VMEM((nb, seq, HEAD_DIM), F32), pltpu.VMEM((nb, seq, HEAD_DIM), F32)],
        compiler_params=_params("parallel", "parallel"),
        name="dilated_attention",
    )(view, view, view, buckets, rel_bias.astype(F32).T)
    return o.reshape(bsz * seq, heads * HEAD_DIM)


def _hgrn_levels(t):
    cs = []
    c = t // 2
    while c >= 1:
        cs.append(c)
        c //= 2
    return cs


def _hgrn_consts(t):
    idx = np.arange(t)
    ltri = (idx[:, None] >= idx[None, :]).astype(np.float32)
    sels = []
    for c in _hgrn_levels(t):
        if c < HGRN_BCAST_MIN:
            ref = (idx // (2 * c)) * (2 * c) + c - 1
            sels.append((idx[None, :] == ref[:, None]).astype(np.float32))
    return jnp.asarray(ltri, BF16), jnp.asarray(np.concatenate(sels, axis=0), BF16)


def _hgrn_kernel(q_ref, f_ref, i_ref, g_ref, lb_ref, gn_ref, ltri_ref, sel_ref, o_ref, state_ref,
                 *, n_chunks):
    t = HGRN_CHUNK
    dk = HEAD_DIM
    levels = _hgrn_levels(t)
    lb = lb_ref[0]
    gn = gn_ref[0]
    ltri = ltri_ref[...]
    sel = sel_ref[...]
    row = lax.broadcasted_iota(jnp.int32, (t, dk), 0)
    ti = lax.broadcasted_iota(jnp.int32, (t, t), 0)
    si = lax.broadcasted_iota(jnp.int32, (t, t), 1)
    state_ref[...] = jnp.zeros_like(state_ref)

    def sum3(y):
        return y[:, :dk] + y[:, dk:2 * dk] + y[:, 2 * dk:]

    def chunk(n, carry):
        st = pl.multiple_of(n * t, t)
        fq = _silu(q_ref[0, pl.ds(st, t), :])
        forget = lb + (1.0 - lb) * _sigmoid(f_ref[0, pl.ds(st, t), :])
        logf = jnp.log(forget)
        key = 1.0 - forget
        val = i_ref[0, pl.ds(st, t), :].astype(BF16)
        a = sum3(_dot(ltri, _split3(logf, 1)))
        aref_fine = sum3(_dot(sel, _split3(a, 1)))
        fq_b = fq.astype(BF16)
        key_b = key.astype(BF16)
        scores = jnp.where(ti == si, _dot_nt(fq_b, key_b), 0.0)
        n_fine = 0
        for c in levels:
            if c >= HGRN_BCAST_MIN:
                aref = jnp.concatenate(
                    [jnp.broadcast_to(a[b0 + c - 1:b0 + c, :], (2 * c, dk)) for b0 in range(0, t, 2 * c)],
                    axis=0)
            else:
                aref = aref_fine[n_fine * t:(n_fine + 1) * t, :]
                n_fine += 1
            upper = (row & (2 * c - 1)) >= c
            decayed = jnp.where(upper, fq, key) * jnp.exp(-jnp.abs(a - aref))
            ql = jnp.where(upper, decayed, 0.0).astype(BF16)
            kl = jnp.where(upper, 0.0, decayed).astype(BF16)
            s_l = _dot_nt(ql, kl)
            if 2 * c < t:
                sh = (2 * c).bit_length() - 1
                s_l = jnp.where((ti >> sh) == (si >> sh), s_l, 0.0)
            scores = scores + s_l
        state_t = state_ref[...]
        inter = _dot_nt((fq * jnp.exp(a)).astype(BF16), state_t.astype(BF16))
        o = inter + _dot(scores.astype(BF16), val)
        a_last = a[t - 1:t, :]
        k_dec = (key * jnp.exp(a_last - a)).astype(BF16)
        state_ref[...] = state_t * jnp.exp(a_last) + _dot_tn(val, k_dec)
        o = o * lax.rsqrt(jnp.mean(o * o, axis=-1, keepdims=True) + RMS_EPS) * gn
        o = o * _silu(g_ref[0, pl.ds(st, t), :])
        o_ref[0, pl.ds(st, t), :] = o.astype(o_ref.dtype)
        return carry

    lax.fori_loop(0, n_chunks, chunk, 0, unroll=2)


def _hgrn2(proj, lb, gn_g, bsz, seq, heads, proj_width, first_group):
    assert seq % HGRN_CHUNK == 0
    view = proj.reshape(bsz, seq, proj_width)
    ltri, sel = _hgrn_consts(HGRN_CHUNK)

    def in_spec(group):
        return pl.BlockSpec((1, seq, HEAD_DIM), lambda b, h, g=group: (b, 0, (first_group + g) * heads + h))

    head_vec = pl.BlockSpec((1, 1, HEAD_DIM), lambda b, h: (h, 0, 0))
    o = pl.pallas_call(
        functools.partial(_hgrn_kernel, n_chunks=seq // HGRN_CHUNK),
        grid=(bsz, heads),
        in_specs=[in_spec(0), in_spec(1), in_spec(2), in_spec(3), head_vec, head_vec,
                  pl.BlockSpec(ltri.shape, lambda b, h: (0, 0)),
                  pl.BlockSpec(sel.shape, lambda b, h: (0, 0))],
        out_specs=pl.BlockSpec((1, seq, HEAD_DIM), lambda b, h: (b, 0, h)),
        out_shape=jax.ShapeDtypeStruct((bsz, seq, heads * HEAD_DIM), BF16),
        scratch_shapes=[pltpu.VMEM((HEAD_DIM, HEAD_DIM), F32)],
        compiler_params=_params("parallel", "parallel"),
        name="hgrn2",
    )(view, view, view, view, lb.reshape(heads, 1, HEAD_DIM).astype(F32),
      gn_g.reshape(heads, 1, HEAD_DIM).astype(F32), ltri, sel)
    return o.reshape(bsz * seq, heads * HEAD_DIM)


def _fox_gate_kernel(x_ref, w_ref, b_ref, ltri_ref, qx_ref, kx_ref, carry):
    heads = w_ref.shape[1]
    ts = x_ref.shape[1]

    @pl.when(pl.program_id(1) == 0)
    def _():
        carry[...] = jnp.zeros_like(carry)

    def log_sigmoid(z):
        return jnp.minimum(z, 0.0) - jnp.log(1.0 + jnp.exp(-jnp.abs(z)))

    x = x_ref[0]
    xh = x.astype(BF16)
    xl = (x - xh.astype(F32)).astype(BF16)
    w = w_ref[...]
    wh = w.astype(BF16)
    wl = (w - wh.astype(F32)).astype(BF16)
    f = _dot(xh, wh) + _dot(xh, wl) + _dot(xl, wh) + b_ref[...]
    c = _dot(ltri_ref[...], _split3(log_sigmoid(f), 1))
    c = c[:, :heads] + c[:, heads:2 * heads] + c[:, 2 * heads:] + carry[...]
    carry[...] = c[ts - 1:ts, :]
    lane = lax.broadcasted_iota(jnp.int32, (ts, HEAD_DIM), 1)
    for h in range(heads):
        ch = c[:, h:h + 1]
        hi = ch.astype(BF16).astype(F32)
        r1 = ch - hi
        mid = r1.astype(BF16).astype(F32)
        lo = r1 - mid
        qx = jnp.where(lane == 0, hi, jnp.where(lane == 1, mid, jnp.where(lane == 2, lo,
                       jnp.where(lane < 6, 1.0, 0.0))))
        kx = jnp.where(lane < 3, 1.0, jnp.where(lane == 3, -hi, jnp.where(lane == 4, -mid,
                       jnp.where(lane == 5, -lo, 0.0))))
        qx_ref[0, :, h * HEAD_DIM:(h + 1) * HEAD_DIM] = qx.astype(BF16)
        kx_ref[0, :, h * HEAD_DIM:(h + 1) * HEAD_DIM] = kx.astype(BF16)


def _fox_gate_terms(x, w_f, b_f, bsz, seq, ts=512):
    d = x.shape[-1]
    heads = w_f.shape[1]
    ts = min(ts, seq)
    idx = np.arange(ts)
    ltri = jnp.asarray(idx[:, None] >= idx[None, :], BF16)
    const = lambda shape: pl.BlockSpec(shape, lambda b, i: (0,) * len(shape))
    out_spec = pl.BlockSpec((1, ts, heads * HEAD_DIM), lambda b, i: (b, i, 0))
    out_sds = jax.ShapeDtypeStruct((bsz, seq, heads * HEAD_DIM), BF16)
    return pl.pallas_call(
        _fox_gate_kernel,
        grid=(bsz, seq // ts),
        in_specs=[pl.BlockSpec((1, ts, d), lambda b, i: (b, i, 0)),
                  const((d, heads)), const((1, heads)), const((ts, ts))],
        out_specs=[out_spec, out_spec],
        out_shape=[out_sds, out_sds],
        scratch_shapes=[pltpu.VMEM((1, heads), F32)],
        compiler_params=_params("parallel", "arbitrary"),
        name="fox_gate_cumsum",
    )(x.reshape(bsz, seq, d), w_f.astype(F32), b_f.astype(F32).reshape(1, heads), ltri)


def _fox_kernel(q_ref, qx_ref, k_ref, kx_ref, vt_ref, o_ref, s_a, s_b, p_a, p_b, acc_ref):
    tq, tk = FOX_Q_TILE, FOX_K_TILE
    lanes = HEAD_DIM
    i = pl.program_id(2)
    q = jnp.concatenate([q_ref[0], qx_ref[0]], axis=1)

    def logits_into(j, s_ref):
        st = pl.multiple_of(j * tk, tk)
        kk = jnp.concatenate([k_ref[0, pl.ds(st, tk), :], kx_ref[0, pl.ds(st, tk), :]], axis=1)
        s_ref[...] = _dot_nt(kk, q)

    def softmax_pv(j, s_ref, p_ref, m, l, diagonal):
        m_out, l_out, alphas = [], [], []
        for c in range(tq // lanes):
            cols = slice(c * lanes, (c + 1) * lanes)
            s = s_ref[:, cols]
            if diagonal:
                key = lax.broadcasted_iota(jnp.int32, (tk, lanes), 0) + j * tk
                qry = lax.broadcasted_iota(jnp.int32, (tk, lanes), 1) + (i * tq + c * lanes)
                s = jnp.where(key <= qry, s, NEG_INF)
            m_new = jnp.maximum(m[:, cols], jnp.max(s, axis=0, keepdims=True))
            alpha = jnp.exp(m[:, cols] - m_new)
            p = jnp.exp(s - m_new)
            l_out.append(alpha * l[:, cols] + jnp.sum(p, axis=0, keepdims=True))
            m_out.append(m_new)
            alphas.append(alpha)
            p_ref[:, cols] = p.astype(BF16)
        vt = vt_ref[0, :, pl.ds(pl.multiple_of(j * tk, tk), tk)]
        acc_ref[...] = jnp.concatenate(alphas, axis=1) * acc_ref[...] + _dot(vt, p_ref[...])
        return jnp.concatenate(m_out, axis=1), jnp.concatenate(l_out, axis=1)

    def pair(jp, carry):
        m, l = carry
        t0 = 2 * jp
        logits_into(t0 + 1, s_b)
        m, l = softmax_pv(t0, s_a, p_a, m, l, False)
        logits_into(t0 + 2, s_a)
        return softmax_pv(t0 + 1, s_b, p_b, m, l, False)

    assert tq == 2 * tk
    acc_ref[...] = jnp.zeros_like(acc_ref)
    logits_into(0, s_a)
    m, l = lax.fori_loop(0, i, pair, (jnp.full((1, tq), NEG_INF, F32), jnp.zeros((1, tq), F32)))
    t0 = 2 * i
    logits_into(t0 + 1, s_b)
    m, l = softmax_pv(t0, s_a, p_a, m, l, True)
    m, l = softmax_pv(t0 + 1, s_b, p_b, m, l, True)
    o_ref[0] = jnp.transpose(acc_ref[...] / l).astype(o_ref.dtype)


def _fox_attention(qkv, qx, kx, bsz, seq, heads):
    tq = FOX_Q_TILE
    assert seq % tq == 0 and tq % FOX_K_TILE == 0
    width = heads * HEAD_DIM
    view = qkv.reshape(bsz, seq, 3 * width)
    v_t = jnp.swapaxes(view[:, :, 2 * width:], 1, 2)
    q_spec = lambda off: pl.BlockSpec((1, tq, HEAD_DIM), lambda b, h, i: (b, i, off + h))
    kv_spec = lambda off: pl.BlockSpec((1, seq, HEAD_DIM), lambda b, h, i: (b, 0, off + h))
    o = pl.pallas_call(
        _fox_kernel,
        grid=(bsz, heads, seq // tq),
        in_specs=[q_spec(0), q_spec(0), kv_spec(heads), kv_spec(0),
                  pl.BlockSpec((1, HEAD_DIM, seq), lambda b, h, i: (b, h, 0))],
        out_specs=pl.BlockSpec((1, tq, HEAD_DIM), lambda b, h, i: (b, i, h)),
        out_shape=jax.ShapeDtypeStruct((bsz, seq, width), BF16),
        scratch_shapes=[pltpu.VMEM((FOX_K_TILE, tq), F32), pltpu.VMEM((FOX_K_TILE, tq), F32),
                        pltpu.VMEM((FOX_K_TILE, tq), BF16), pltpu.VMEM((FOX_K_TILE, tq), BF16),
                        pltpu.VMEM((HEAD_DIM, tq), F32)],
        compiler_params=_params("parallel", "parallel", "parallel"),
        name="fox_attention",
    )(view, qx, view, kx, v_t)
    return o.reshape(bsz * seq, width)


def _router_kernel(x_ref, wt_ref, bias_ref, utri_ref, eidx_ref, rank_ref, w_ref, cnt_ref, carry):
    gsz = GROUP_SIZE
    tt = x_ref.shape[0]

    @pl.when(pl.program_id(0) == 0)
    def _():
        carry[...] = jnp.zeros_like(carry)

    x = x_ref[...]
    xh = x.astype(BF16)
    xl = (x - xh.astype(F32)).astype(BF16)
    wt = wt_ref[...]
    wh = wt.astype(BF16)
    wl = (wt - wh.astype(F32)).astype(BF16)
    logits = _dot_nt(wh, xh) + _dot_nt(wh, xl) + _dot_nt(wl, xh)
    scores = _sigmoid(logits)
    sel = scores + bias_ref[...]
    sub = lax.broadcasted_iota(jnp.int32, (gsz, tt), 0)
    neg = -jnp.inf

    group_score = []
    for g in range(N_GROUPS):
        blk = sel[g * gsz:(g + 1) * gsz, :]
        m1 = jnp.max(blk, axis=0, keepdims=True)
        first = jnp.min(jnp.where(blk == m1, sub, gsz), axis=0, keepdims=True)
        m2 = jnp.max(jnp.where(sub == first, neg, blk), axis=0, keepdims=True)
        group_score.append(m1 + m2)

    chosen = [jnp.zeros((1, tt), F32) for _ in range(N_GROUPS)]
    for _ in range(TOPK_GROUPS):
        m = functools.reduce(jnp.maximum, group_score)
        gi = jnp.full((1, tt), N_GROUPS, jnp.int32)
        for g in reversed(range(N_GROUPS)):
            gi = jnp.where(group_score[g] == m, g, gi)
        for g in range(N_GROUPS):
            hit = gi == g
            chosen[g] = jnp.where(hit, 1.0, chosen[g])
            group_score[g] = jnp.where(hit, neg, group_score[g])

    cand = [jnp.where(jnp.broadcast_to(chosen[g], (gsz, tt)) > 0.5, sel[g * gsz:(g + 1) * gsz, :], NEG_INF)
            for g in range(N_GROUPS)]
    score_g = [scores[g * gsz:(g + 1) * gsz, :] for g in range(N_GROUPS)]
    eiota = [sub + g * gsz for g in range(N_GROUPS)]
    picked = [jnp.zeros((gsz, tt), F32) for _ in range(N_GROUPS)]
    top_idx, top_score = [], []
    for _ in range(TOP_K):
        m = functools.reduce(jnp.maximum, [jnp.max(c, axis=0, keepdims=True) for c in cand])
        ei = functools.reduce(jnp.minimum, [
            jnp.min(jnp.where(cand[g] == m, eiota[g], N_EXPERTS), axis=0, keepdims=True)
            for g in range(N_GROUPS)])
        sc = jnp.zeros((1, tt), F32)
        for g in range(N_GROUPS):
            hit = eiota[g] == ei
            picked[g] = jnp.where(hit, 1.0, picked[g])
            cand[g] = jnp.where(hit, neg, cand[g])
            sc = sc + jnp.sum(jnp.where(hit, score_g[g], 0.0), axis=0, keepdims=True)
        top_idx.append(ei)
        top_score.append(sc)

    total = functools.reduce(lambda a, b: a + b, top_score)
    mask = jnp.concatenate(picked, axis=0)
    incl = _dot(mask.astype(BF16), utri_ref[...])
    rank = carry[...] + incl - mask
    for r in range(TOP_K):
        ei = top_idx[r]
        rk = jnp.zeros((1, tt), F32)
        for g in range(N_GROUPS):
            rk = rk + jnp.sum(jnp.where(eiota[g] == ei, rank[g * gsz:(g + 1) * gsz, :], 0.0),
                              axis=0, keepdims=True)
        eidx_ref[r:r + 1, :] = ei
        rank_ref[r:r + 1, :] = rk.astype(jnp.int32)
        w_ref[r:r + 1, :] = top_score[r] / total * ROUTED_SCALE
    new_carry = carry[...] + incl[:, tt - 1:tt]
    carry[...] = new_carry
    cnt_ref[...] = jnp.broadcast_to(new_carry, cnt_ref.shape)


def _router(x, router_w, router_bias):
    n, d = x.shape
    tt = min(ROUTER_TILE, n)
    idx = np.arange(tt)
    utri = jnp.asarray(idx[:, None] <= idx[None, :], BF16)
    eidx, rank, w8, cnt = pl.pallas_call(
        _router_kernel,
        grid=(n // tt,),
        in_specs=[pl.BlockSpec((tt, d), lambda i: (i, 0)),
                  pl.BlockSpec((N_EXPERTS, d), lambda i: (0, 0)),
                  pl.BlockSpec((N_EXPERTS, 1), lambda i: (0, 0)),
                  pl.BlockSpec((tt, tt), lambda i: (0, 0))],
        out_specs=[pl.BlockSpec((TOP_K, tt), lambda i: (0, i)),
                   pl.BlockSpec((TOP_K, tt), lambda i: (0, i)),
                   pl.BlockSpec((TOP_K, tt), lambda i: (0, i)),
                   pl.BlockSpec((N_EXPERTS, 128), lambda i: (0, 0))],
        out_shape=[jax.ShapeDtypeStruct((TOP_K, n), jnp.int32),
                   jax.ShapeDtypeStruct((TOP_K, n), jnp.int32),
                   jax.ShapeDtypeStruct((TOP_K, n), F32),
                   jax.ShapeDtypeStruct((N_EXPERTS, 128), F32)],
        scratch_shapes=[pltpu.VMEM((N_EXPERTS, 1), F32)],
        compiler_params=_params("arbitrary"),
        name="moe_router",
    )(x, router_w.astype(F32).T, router_bias.astype(F32).reshape(N_EXPERTS, 1), utri)
    return eidx, rank, w8, cnt[:, 0].astype(jnp.int32)


def _dispatch_kernel(dest_ref, x_ref, xs_hbm, sem):
    tt = x_ref.shape[0]

    def copy(t, k):
        return pltpu.make_async_copy(x_ref.at[pl.ds(t, 1), :],
                                     xs_hbm.at[pl.ds(dest_ref[0, k, t], 1), :], sem)

    def issue(t, carry):
        for k in range(TOP_K):
            copy(t, k).start()
        return carry

    lax.fori_loop(0, tt, issue, 0)
    rows = xs_hbm.at[pl.ds(0, TOP_K * tt), :]
    pltpu.make_async_copy(rows, rows, sem).wait()


def _dispatch(x, dest_tiles, n_rows):
    n, d = x.shape
    tt = dest_tiles.shape[2]
    return pl.pallas_call(
        _dispatch_kernel,
        grid=(n // tt,),
        in_specs=[pl.BlockSpec((1, TOP_K, tt), lambda i: (i, 0, 0), memory_space=pltpu.SMEM),
                  pl.BlockSpec((tt, d), lambda i: (i, 0))],
        out_specs=pl.BlockSpec(memory_space=pl.ANY),
        out_shape=jax.ShapeDtypeStruct((n_rows, d), x.dtype),
        scratch_shapes=[pltpu.SemaphoreType.DMA(())],
        compiler_params=_params("arbitrary"),
        name="moe_dispatch",
    )(dest_tiles, x)


def _expert_kernel(te_ref, tv_ref, nt_ref, xs_ref, wg_ref, wu_ref, wd_ref, y_ref,
                   wg_bf, wu_bf, wd_bf):
    i = pl.program_id(0)
    live = i < nt_ref[0]
    new_expert = jnp.logical_or(i == 0, te_ref[i] != te_ref[jnp.maximum(i - 1, 0)])

    @pl.when(jnp.logical_and(live, new_expert))
    def _():
        wg_bf[...] = wg_ref[0, 0].astype(BF16)
        wu_bf[...] = wu_ref[0, 0].astype(BF16)
        wd_bf[...] = wd_ref[0, 0].astype(BF16)

    @pl.when(live)
    def _():
        tm = xs_ref.shape[0]
        row = lax.broadcasted_iota(jnp.int32, (tm, 1), 0)
        x = jnp.where(row < tv_ref[i], xs_ref[...], 0.0).astype(BF16)
        gate = _dot(x, wg_bf[...])
        up = _dot(x, wu_bf[...])
        hidden = (_silu(gate) * up).astype(BF16)
        y_ref[...] = _dot(hidden, wd_bf[...])


def _expert_ffn(xs, w_gate, w_up, w_down, layer, tile_expert, tile_valid, num_tiles):
    p, d = xs.shape
    tm = EXPERT_TILE
    e_dim = w_gate.shape[3]
    row_map = lambda i, te, tv, nt: (jnp.minimum(i, nt[0] - 1), 0)
    w_map = lambda i, te, tv, nt: (layer, te[i], 0, 0)
    return pl.pallas_call(
        _expert_kernel,
        grid_spec=pltpu.PrefetchScalarGridSpec(
            num_scalar_prefetch=3,
            grid=(p // tm,),
            in_specs=[pl.BlockSpec((tm, d), row_map),
                      pl.BlockSpec((1, 1, d, e_dim), w_map),
                      pl.BlockSpec((1, 1, d, e_dim), w_map),
                      pl.BlockSpec((1, 1, e_dim, d), w_map)],
            out_specs=pl.BlockSpec((tm, d), row_map),
            scratch_shapes=[pltpu.VMEM((d, e_dim), BF16), pltpu.VMEM((d, e_dim), BF16),
                            pltpu.VMEM((e_dim, d), BF16)]),
        out_shape=jax.ShapeDtypeStruct((p, d), F32),
        compiler_params=_params("arbitrary"),
        name="moe_expert_ffn",
    )(tile_expert, tile_valid, num_tiles, xs, w_gate, w_up, w_down)


def _combine_kernel(dest_ref, x_ref, w8_ref, sg_ref, su_ref, sd_ref, g_ref, b_ref, y_hbm,
                    o_ref, obf_ref, ybuf, sem):
    tc = x_ref.shape[0]

    def copy(t, k):
        return pltpu.make_async_copy(y_hbm.at[pl.ds(dest_ref[0, k, t], 1), :],
                                     ybuf.at[k, pl.ds(t, 1), :], sem)

    def issue(t, carry):
        for k in range(TOP_K):
            copy(t, k).start()
        return carry

    lax.fori_loop(0, tc, issue, 0)
    x = x_ref[...]
    xb = x.astype(BF16)
    hidden = (_silu(_dot(xb, sg_ref[...])) * _dot(xb, su_ref[...])).astype(BF16)
    acc = _dot(hidden, sd_ref[...])
    pltpu.make_async_copy(ybuf, ybuf, sem).wait()
    w8 = w8_ref[...]
    for k in range(TOP_K):
        acc = acc + w8[:, k:k + 1] * ybuf[k]
    y = _layer_norm_rows(DN_ALPHA * x + acc, g_ref[...], b_ref[...])
    o_ref[...] = y
    obf_ref[...] = y.astype(BF16)


def _combine(x, w8_rows, dest_tiles, y_sorted, s_gate, s_up, s_down, g, b):
    n, d = x.shape
    tc = dest_tiles.shape[2]
    sdim = s_gate.shape[1]
    const = lambda shape: pl.BlockSpec(shape, lambda i: (0,) * len(shape))
    row = pl.BlockSpec((tc, d), lambda i: (i, 0))
    return pl.pallas_call(
        _combine_kernel,
        grid=(n // tc,),
        in_specs=[pl.BlockSpec((1, TOP_K, tc), lambda i: (i, 0, 0), memory_space=pltpu.SMEM),
                  row, pl.BlockSpec((tc, TOP_K), lambda i: (i, 0)),
                  const((d, sdim)), const((d, sdim)), const((sdim, d)), const((1, d)), const((1, d)),
                  pl.BlockSpec(memory_space=pl.ANY)],
        out_specs=[row, row],
        out_shape=[jax.ShapeDtypeStruct((n, d), F32), jax.ShapeDtypeStruct((n, d), BF16)],
        scratch_shapes=[pltpu.VMEM((TOP_K, tc, d), F32), pltpu.SemaphoreType.DMA(())],
        compiler_params=_params("arbitrary"),
        name="moe_combine",
    )(dest_tiles, x, w8_rows, s_gate, s_up, s_down, g.reshape(1, d), b.reshape(1, d), y_sorted)


def _tile_major(a, tile):
    k, n = a.shape
    return a.reshape(k, n // tile, tile).transpose(1, 0, 2)


def _moe(x, router_w, router_bias, w_gate, w_up, w_down, layer, s_gate, s_up, s_down, ln_g, ln_b):
    n, d = x.shape
    tm = EXPERT_TILE
    eidx, rank, w8, cnt = _router(x, router_w, router_bias)
    tiles_e = (cnt + tm - 1) // tm
    tile_end = jnp.cumsum(tiles_e)
    tile_start = tile_end - tiles_e
    row_start = tile_start * tm
    e_ids = jnp.arange(N_EXPERTS, dtype=jnp.int32)
    dest = jnp.sum(jnp.where(eidx[None] == e_ids[:, None, None], row_start[:, None, None], 0), axis=0) + rank
    max_tiles = (n * TOP_K) // tm + N_EXPERTS
    num_tiles = tile_end[-1]
    tile_ids = jnp.arange(max_tiles, dtype=jnp.int32)
    t_clamped = jnp.minimum(tile_ids, num_tiles - 1)
    tile_expert = jnp.sum(t_clamped[:, None] >= tile_end[None, :], axis=1).astype(jnp.int32)
    tile_valid = jnp.clip(cnt[tile_expert] - (t_clamped - tile_start[tile_expert]) * tm, 0, tm).astype(jnp.int32)

    xs = _dispatch(x, _tile_major(dest, min(DISPATCH_TILE, n)), max_tiles * tm)
    ys = _expert_ffn(xs, w_gate, w_up, w_down, layer, tile_expert, tile_valid,
                     num_tiles.reshape(1).astype(jnp.int32))
    return _combine(x, w8.T, _tile_major(dest, min(COMBINE_TILE, n)), ys, s_gate, s_up, s_down, ln_g, ln_b)


def _even_layer(x, x_bf, bsz, seq, w_in, gn_g, w_out, rel_bias, lb, ln_g, ln_b):
    heads = w_out.shape[0] // (2 * HEAD_DIM)
    width = heads * HEAD_DIM
    proj_width = w_in.shape[1]
    col_scale = jnp.concatenate([jnp.full((1, width), HEAD_DIM ** -0.5, F32),
                                 jnp.ones((1, proj_width - width), F32)], axis=1)
    proj = _matmul(x_bf, w_in, col_scale, F32)
    o_a = _dilated_mixture(proj, rel_bias, bsz, seq, heads, proj_width)
    o_b = _hgrn2(proj, lb, gn_g, bsz, seq, heads, proj_width, first_group=3)
    return _proj_residual_ln([o_a, o_b], [w_out[:width], w_out[width:]], x, ln_g, ln_b)


def _odd_layer(x, x_bf, bsz, seq, w_qkv, w_f, b_f, w_out, ln_g, ln_b):
    heads = w_f.shape[1]
    width = heads * HEAD_DIM
    col_scale = jnp.concatenate([jnp.full((1, width), HEAD_DIM ** -0.5, F32),
                                 jnp.ones((1, 2 * width), F32)], axis=1)
    qkv = _matmul(x_bf, w_qkv, col_scale, BF16)
    qx, kx = _fox_gate_terms(x, w_f, b_f, bsz, seq)
    o = _fox_attention(qkv, qx, kx, bsz, seq, heads)
    return _proj_residual_ln([o], [w_out], x, ln_g, ln_b)


def kernel(x, rel_bias, hgrn_lb_logits, a_w_in, a_gn_g, a_w_out, c_w_in, c_b_f, c_w_out, ln_mix_g, ln_mix_b, ln_ffn_g, ln_ffn_b, router_w, router_bias, exp_w_gate, exp_w_up, exp_w_down, sh_w_gate, sh_w_up, sh_w_down):
    bsz, seq, d = x.shape
    depth = ln_mix_g.shape[0]
    lb_all = jnp.cumsum(jax.nn.softmax(hgrn_lb_logits.astype(F32), axis=0), axis=0)
    h = x.reshape(bsz * seq, d).astype(F32)
    h_bf = h.astype(BF16)
    for layer in range(depth):
        j = layer // 2
        if layer % 2 == 0:
            h, h_bf = _even_layer(h, h_bf, bsz, seq, a_w_in[j].astype(BF16), a_gn_g[j],
                                  a_w_out[j].astype(BF16), rel_bias, lb_all[layer],
                                  ln_mix_g[layer], ln_mix_b[layer])
        else:
            c_width = c_w_out.shape[1]
            w_in = c_w_in[j]
            h, h_bf = _odd_layer(h, h_bf, bsz, seq, w_in[:, :3 * c_width].astype(BF16),
                                 w_in[:, 3 * c_width:], c_b_f[j], c_w_out[j].astype(BF16),
                                 ln_mix_g[layer], ln_mix_b[layer])
        h, h_bf = _moe(h, router_w[layer], router_bias[layer], exp_w_gate, exp_w_up, exp_w_down, layer,
                       sh_w_gate[layer].astype(BF16), sh_w_up[layer].astype(BF16),
                       sh_w_down[layer].astype(BF16), ln_ffn_g[layer], ln_ffn_b[layer])
    return h.reshape(bsz, seq, d)
```

```python
import functools

import numpy as np
import jax
import jax.numpy as jnp
from jax import lax
from jax.experimental import pallas as pl
from jax.experimental.pallas import tpu as pltpu

F32 = jnp.float32
BF16 = jnp.bfloat16

HEAD_DIM = 128
A_PATTERNS = ((128, 1), (512, 4), (2048, 16))
A_BLOCK = 128
REL_BUCKETS = 32
REL_MAX_DIST = 2048
N_EXPERTS = 64
TOP_K = 8
N_GROUPS = 8
GROUP_SIZE = N_EXPERTS // N_GROUPS
TOPK_GROUPS = 4
ROUTED_SCALE = 2.5
LN_EPS = 1e-5
RMS_EPS = 1e-6
DEPTH = 2
DN_ALPHA = (2 * DEPTH) ** 0.25
NEG_INF = -1e30

DILATED_UNROLL = 8
HGRN_CHUNK = 128
HGRN_BCAST_MIN = 8
FOX_Q_TILE = 512
FOX_K_TILE = 256
ROUTER_TILE = 512
DISPATCH_TILE = 256
EXPERT_TILE = 512
COMBINE_TILE = 128
VMEM_LIMIT = 56 * 1024 * 1024

_NT = (((1,), (1,)), ((), ()))
_TN = (((0,), (0,)), ((), ()))


def _dot(a, b):
    return jnp.dot(a, b, preferred_element_type=F32)


def _dot_nt(a, b):
    return lax.dot_general(a, b, _NT, preferred_element_type=F32)


def _dot_tn(a, b):
    return lax.dot_general(a, b, _TN, preferred_element_type=F32)


def _sigmoid(x):
    return 1.0 / (1.0 + jnp.exp(-x))


def _silu(x):
    return x * _sigmoid(x)


def _split3(x, axis):
    hi = x.astype(BF16)
    r1 = x - hi.astype(F32)
    mid = r1.astype(BF16)
    lo = (r1 - mid.astype(F32)).astype(BF16)
    return jnp.concatenate([hi, mid, lo], axis=axis)


def _layer_norm_rows(y, g, b):
    mu = jnp.mean(y, axis=-1, keepdims=True)
    yc = y - mu
    var = jnp.mean(yc * yc, axis=-1, keepdims=True)
    return yc * lax.rsqrt(var + LN_EPS) * g + b


def _params(*sem):
    return pltpu.CompilerParams(dimension_semantics=sem, vmem_limit_bytes=VMEM_LIMIT)


def _mm_kernel(x_ref, w_ref, cs_ref, o_ref):
    o_ref[...] = (_dot(x_ref[...], w_ref[...]) * cs_ref[...]).astype(o_ref.dtype)


def _matmul(x, w, col_scale, out_dtype, tm=1024, tn=1024):
    m, k = x.shape
    nw = w.shape[1]
    tm, tn = min(tm, m), min(tn, nw)
    assert m % tm == 0 and nw % tn == 0
    return pl.pallas_call(
        _mm_kernel,
        grid=(nw // tn, m // tm),
        in_specs=[pl.BlockSpec((tm, k), lambda j, i: (i, 0)),
                  pl.BlockSpec((k, tn), lambda j, i: (0, j)),
                  pl.BlockSpec((1, tn), lambda j, i: (0, j))],
        out_specs=pl.BlockSpec((tm, tn), lambda j, i: (i, j)),
        out_shape=jax.ShapeDtypeStruct((m, nw), out_dtype),
        compiler_params=_params("parallel", "parallel"),
        name="proj_matmul",
    )(x, w, col_scale)


def _proj_ln_kernel(*refs, n_in):
    a_refs, w_refs = refs[:n_in], refs[n_in:2 * n_in]
    r_ref, g_ref, b_ref, o_ref, obf_ref = refs[2 * n_in:]
    acc = _dot(a_refs[0][...], w_refs[0][...])
    for a_ref, w_ref in zip(a_refs[1:], w_refs[1:]):
        acc = acc + _dot(a_ref[...], w_ref[...])
    y = _layer_norm_rows(DN_ALPHA * r_ref[...] + acc, g_ref[...], b_ref[...])
    o_ref[...] = y
    half = y.shape[1] // 2
    obf_ref[...] = _pack_bf16_pair(y[:, :half], y[:, half:])


def _pack_bf16_pair(lo, hi):
    lo_bits = lax.shift_right_logical(lax.bitcast_convert_type(lo.astype(BF16).astype(F32), jnp.int32), 16)
    hi_bits = lax.bitcast_convert_type(hi.astype(BF16).astype(F32), jnp.int32) & jnp.int32(-65536)
    return lo_bits | hi_bits


def _unpack_bf16_pair(words):
    lo = lax.bitcast_convert_type(lax.shift_left(words, 16), F32)
    hi = lax.bitcast_convert_type(words & jnp.int32(-65536), F32)
    return lo, hi


def _proj_residual_ln(acts, weights, resid, g, b, tm=256):
    m, d = resid.shape
    tm = min(tm, m)
    n_in = len(acts)
    in_specs = [pl.BlockSpec((tm, a.shape[1]), lambda i: (i, 0)) for a in acts]
    in_specs += [pl.BlockSpec(w.shape, lambda i: (0, 0)) for w in weights]
    in_specs += [pl.BlockSpec((tm, d), lambda i: (i, 0)),
                 pl.BlockSpec((1, d), lambda i: (0, 0)),
                 pl.BlockSpec((1, d), lambda i: (0, 0))]
    return pl.pallas_call(
        functools.partial(_proj_ln_kernel, n_in=n_in),
        grid=(m // tm,),
        in_specs=in_specs,
        out_specs=[pl.BlockSpec((tm, d), lambda i: (i, 0)), pl.BlockSpec((tm, d // 2), lambda i: (i, 0))],
        out_shape=[jax.ShapeDtypeStruct((m, d), F32), jax.ShapeDtypeStruct((m, d // 2), jnp.int32)],
        compiler_params=_params("parallel"),
        name="out_proj_ln",
    )(*acts, *weights, resid, g.reshape(1, d), b.reshape(1, d))


def _t5_bucket(dist):
    max_exact = REL_BUCKETS // 2
    d = np.maximum(dist, 1).astype(np.float64)
    large = max_exact + (np.log(d / max_exact) / np.log(REL_MAX_DIST / max_exact)
                         * (REL_BUCKETS - max_exact)).astype(np.int64)
    large = np.minimum(large, REL_BUCKETS - 1)
    return np.where(dist < max_exact, dist, large).astype(np.int32)


def _dilated_kernel(q_ref, k_ref, v_ref, bucket_ref, rel_bias_ref, out_ref, o_scr, lse_scr):
    blk = A_BLOCK
    seq = q_ref.shape[1]
    head = pl.program_id(1)
    qi = lax.broadcasted_iota(jnp.int32, (blk, 2 * blk), 0)
    ki = lax.broadcasted_iota(jnp.int32, (blk, 2 * blk), 1)
    rel = qi + blk - ki

    for bi, (window, dil) in enumerate(A_PATTERNS):
        band = window // dil
        n_blocks = seq // (dil * blk)
        valid = jnp.abs(2 * rel - band) <= band
        bucket = bucket_ref[bi]
        bias = jnp.zeros((blk, 2 * blk), F32)
        for b in range(REL_BUCKETS):
            bias = jnp.where(bucket == b, rel_bias_ref[head, b], bias)

        def attend(q_start, k_start, n_keys, bias_blk, valid_blk, bi=bi, dil=dil):
            q_rows = pl.ds(q_start, blk, stride=dil)
            k_rows = pl.ds(k_start, n_keys, stride=dil)
            q = q_ref[0, q_rows, :].astype(BF16)
            kk = k_ref[0, k_rows, :].astype(BF16)
            vv = v_ref[0, k_rows, :].astype(BF16)
            s = _dot_nt(q, kk) + bias_blk
            s = jnp.where(valid_blk, s, NEG_INF)
            m = jnp.max(s, axis=-1, keepdims=True)
            p = jnp.exp(s - m)
            l = jnp.sum(p, axis=-1, keepdims=True)
            o_scr[bi, q_rows, :] = _dot(p.astype(BF16), vv) / l
            lse_scr[bi, q_rows, :] = jnp.broadcast_to(m + jnp.log(l), (blk, HEAD_DIM))

        def first(r, carry, attend=attend, bias=bias, valid=valid):
            attend(r, r, blk, bias[:, blk:], valid[:, blk:])
            return carry

        lax.fori_loop(0, dil, first, 0, unroll=min(dil, DILATED_UNROLL))

        def rest(t, carry, attend=attend, bias=bias, valid=valid, dil=dil):
            r = t & (dil - 1)
            n = (t >> (dil.bit_length() - 1)) + 1
            attend(r + n * blk * dil, r + (n - 1) * blk * dil, 2 * blk, bias, valid)
            return carry

        lax.fori_loop(0, dil * (n_blocks - 1), rest, 0, unroll=DILATED_UNROLL)

    chunk = 2 * blk
    nb = len(A_PATTERNS)

    def mix(c, carry):
        rows = pl.ds(pl.multiple_of(c * chunk, chunk), chunk)
        lses = [lse_scr[b, rows, :] for b in range(nb)]
        m = functools.reduce(jnp.maximum, lses)
        es = [jnp.exp(x - m) for x in lses]
        den = functools.reduce(lambda a, b: a + b, es)
        num = functools.reduce(lambda a, b: a + b, [e * o_scr[b, rows, :] for b, e in enumerate(es)])
        out_ref[0, rows, :] = (num / den).astype(out_ref.dtype)
        return carry

    lax.fori_loop(0, seq // chunk, mix, 0)


def _dilated_bucket_tables():
    qi = np.arange(A_BLOCK)[:, None]
    ki = np.arange(2 * A_BLOCK)[None, :]
    rel = qi + A_BLOCK - ki
    return np.stack([_t5_bucket(np.clip(rel, 0, None) * dilation) for _, dilation in A_PATTERNS], axis=0)


def _dilated_mixture(proj, rel_bias, bsz, seq, heads, proj_width):
    for window, dilation in A_PATTERNS:
        assert seq % (dilation * A_BLOCK) == 0 and window // dilation <= A_BLOCK
    nb = len(A_PATTERNS)
    view = proj.reshape(bsz, seq, proj_width)
    buckets = jnp.asarray(_dilated_bucket_tables())

    def in_spec(group):
        return pl.BlockSpec((1, seq, HEAD_DIM), lambda b, h, g=group: (b, 0, g * heads + h))

    o = pl.pallas_call(
        _dilated_kernel,
        grid=(bsz, heads),
        in_specs=[in_spec(0), in_spec(1), in_spec(2),
                  pl.BlockSpec(buckets.shape, lambda b, h: (0, 0, 0)),
                  pl.BlockSpec(memory_space=pltpu.SMEM)],
        out_specs=pl.BlockSpec((1, seq, HEAD_DIM), lambda b, h: (b, 0, h)),
        out_shape=jax.ShapeDtypeStruct((bsz, seq, heads * HEAD_DIM), BF16),
        scratch_shapes=[pltpu.VMEM((nb, seq, HEAD_DIM), F32), pltpu.VMEM((nb, seq, HEAD_DIM), F32)],
        compiler_params=_params("parallel", "parallel"),
        name="dilated_attention",
    )(view, view, view, buckets, rel_bias.astype(F32).T)
    return o.reshape(bsz * seq, heads * HEAD_DIM)


def _hgrn_levels(t):
    cs = []
    c = t // 2
    while c >= 1:
        cs.append(c)
        c //= 2
    return cs


def _hgrn_consts(t):
    idx = np.arange(t)
    ltri = (idx[:, None] >= idx[None, :]).astype(np.float32)
    sels = []
    for c in _hgrn_levels(t):
        if c < HGRN_BCAST_MIN:
            ref = (idx // (2 * c)) * (2 * c) + c - 1
            sels.append((idx[None, :] == ref[:, None]).astype(np.float32))
    return jnp.asarray(ltri, BF16), jnp.asarray(np.concatenate(sels, axis=0), BF16)


def _hgrn_kernel(q_ref, f_ref, i_ref, g_ref, lb_ref, gn_ref, ltri_ref, sel_ref, o_ref, state_ref,
                 *, n_chunks):
    t = HGRN_CHUNK
    dk = HEAD_DIM
    levels = _hgrn_levels(t)
    lb = lb_ref[0]
    gn = gn_ref[0]
    ltri = ltri_ref[...]
    sel = sel_ref[...]
    row = lax.broadcasted_iota(jnp.int32, (t, dk), 0)
    ti = lax.broadcasted_iota(jnp.int32, (t, t), 0)
    si = lax.broadcasted_iota(jnp.int32, (t, t), 1)
    state_ref[...] = jnp.zeros_like(state_ref)

    def sum3(y):
        return y[:, :dk] + y[:, dk:2 * dk] + y[:, 2 * dk:]

    def chunk(n, carry):
        st = pl.multiple_of(n * t, t)
        fq = _silu(q_ref[0, pl.ds(st, t), :])
        forget = lb + (1.0 - lb) * _sigmoid(f_ref[0, pl.ds(st, t), :])
        logf = jnp.log(forget)
        key = 1.0 - forget
        val = i_ref[0, pl.ds(st, t), :].astype(BF16)
        a = sum3(_dot(ltri, _split3(logf, 1)))
        aref_fine = sum3(_dot(sel, _split3(a, 1)))
        fq_b = fq.astype(BF16)
        key_b = key.astype(BF16)
        scores = jnp.where(ti == si, _dot_nt(fq_b, key_b), 0.0)
        n_fine = 0
        for c in levels:
            if c >= HGRN_BCAST_MIN:
                aref = jnp.concatenate(
                    [jnp.broadcast_to(a[b0 + c - 1:b0 + c, :], (2 * c, dk)) for b0 in range(0, t, 2 * c)],
                    axis=0)
            else:
                aref = aref_fine[n_fine * t:(n_fine + 1) * t, :]
                n_fine += 1
            upper = (row & (2 * c - 1)) >= c
            decayed = jnp.where(upper, fq, key) * jnp.exp(-jnp.abs(a - aref))
            ql = jnp.where(upper, decayed, 0.0).astype(BF16)
            kl = jnp.where(upper, 0.0, decayed).astype(BF16)
            s_l = _dot_nt(ql, kl)
            if 2 * c < t:
                sh = (2 * c).bit_length() - 1
                s_l = jnp.where((ti >> sh) == (si >> sh), s_l, 0.0)
            scores = scores + s_l
        state_t = state_ref[...]
        inter = _dot_nt((fq * jnp.exp(a)).astype(BF16), state_t.astype(BF16))
        o = inter + _dot(scores.astype(BF16), val)
        a_last = a[t - 1:t, :]
        k_dec = (key * jnp.exp(a_last - a)).astype(BF16)
        state_ref[...] = state_t * jnp.exp(a_last) + _dot_tn(val, k_dec)
        o = o * lax.rsqrt(jnp.mean(o * o, axis=-1, keepdims=True) + RMS_EPS) * gn
        o = o * _silu(g_ref[0, pl.ds(st, t), :])
        o_ref[0, pl.ds(st, t), :] = o.astype(o_ref.dtype)
        return carry

    lax.fori_loop(0, n_chunks, chunk, 0, unroll=2)


def _hgrn2(proj, lb, gn_g, bsz, seq, heads, proj_width, first_group):
    assert seq % HGRN_CHUNK == 0
    view = proj.reshape(bsz, seq, proj_width)
    ltri, sel = _hgrn_consts(HGRN_CHUNK)

    def in_spec(group):
        return pl.BlockSpec((1, seq, HEAD_DIM), lambda b, h, g=group: (b, 0, (first_group + g) * heads + h))

    head_vec = pl.BlockSpec((1, 1, HEAD_DIM), lambda b, h: (h, 0, 0))
    o = pl.pallas_call(
        functools.partial(_hgrn_kernel, n_chunks=seq // HGRN_CHUNK),
        grid=(bsz, heads),
        in_specs=[in_spec(0), in_spec(1), in_spec(2), in_spec(3), head_vec, head_vec,
                  pl.BlockSpec(ltri.shape, lambda b, h: (0, 0)),
                  pl.BlockSpec(sel.shape, lambda b, h: (0, 0))],
        out_specs=pl.BlockSpec((1, seq, HEAD_DIM), lambda b, h: (b, 0, h)),
        out_shape=jax.ShapeDtypeStruct((bsz, seq, heads * HEAD_DIM), BF16),
        scratch_shapes=[pltpu.VMEM((HEAD_DIM, HEAD_DIM), F32)],
        compiler_params=_params("parallel", "parallel"),
        name="hgrn2",
    )(view, view, view, view, lb.reshape(heads, 1, HEAD_DIM).astype(F32),
      gn_g.reshape(heads, 1, HEAD_DIM).astype(F32), ltri, sel)
    return o.reshape(bsz * seq, heads * HEAD_DIM)


def _fox_gate_kernel(x_ref, w_ref, b_ref, ltri_ref, qx_ref, kx_ref, carry):
    heads = w_ref.shape[1]
    ts = x_ref.shape[1]

    @pl.when(pl.program_id(1) == 0)
    def _():
        carry[...] = jnp.zeros_like(carry)

    def log_sigmoid(z):
        return jnp.minimum(z, 0.0) - jnp.log(1.0 + jnp.exp(-jnp.abs(z)))

    x = x_ref[0]
    xh = x.astype(BF16)
    xl = (x - xh.astype(F32)).astype(BF16)
    w = w_ref[...]
    wh = w.astype(BF16)
    wl = (w - wh.astype(F32)).astype(BF16)
    f = _dot(xh, wh) + _dot(xh, wl) + _dot(xl, wh) + b_ref[...]
    c = _dot(ltri_ref[...], _split3(log_sigmoid(f), 1))
    c = c[:, :heads] + c[:, heads:2 * heads] + c[:, 2 * heads:] + carry[...]
    carry[...] = c[ts - 1:ts, :]
    lane = lax.broadcasted_iota(jnp.int32, (ts, HEAD_DIM), 1)
    for h in range(heads):
        ch = c[:, h:h + 1]
        hi = ch.astype(BF16).astype(F32)
        r1 = ch - hi
        mid = r1.astype(BF16).astype(F32)
        lo = r1 - mid
        qx = jnp.where(lane == 0, hi, jnp.where(lane == 1, mid, jnp.where(lane == 2, lo,
                       jnp.where(lane < 6, 1.0, 0.0))))
        kx = jnp.where(lane < 3, 1.0, jnp.where(lane == 3, -hi, jnp.where(lane == 4, -mid,
                       jnp.where(lane == 5, -lo, 0.0))))
        qx_ref[0, :, h * HEAD_DIM:(h + 1) * HEAD_DIM] = qx.astype(BF16)
        kx_ref[0, :, h * HEAD_DIM:(h + 1) * HEAD_DIM] = kx.astype(BF16)


def _fox_gate_terms(x, w_f, b_f, bsz, seq, ts=512):
    d = x.shape[-1]
    heads = w_f.shape[1]
    ts = min(ts, seq)
    idx = np.arange(ts)
    ltri = jnp.asarray(idx[:, None] >= idx[None, :], BF16)
    const = lambda shape: pl.BlockSpec(shape, lambda b, i: (0,) * len(shape))
    out_spec = pl.BlockSpec((1, ts, heads * HEAD_DIM), lambda b, i: (b, i, 0))
    out_sds = jax.ShapeDtypeStruct((bsz, seq, heads * HEAD_DIM), BF16)
    return pl.pallas_call(
        _fox_gate_kernel,
        grid=(bsz, seq // ts),
        in_specs=[pl.BlockSpec((1, ts, d), lambda b, i: (b, i, 0)),
                  const((d, heads)), const((1, heads)), const((ts, ts))],
        out_specs=[out_spec, out_spec],
        out_shape=[out_sds, out_sds],
        scratch_shapes=[pltpu.VMEM((1, heads), F32)],
        compiler_params=_params("parallel", "arbitrary"),
        name="fox_gate_cumsum",
    )(x.reshape(bsz, seq, d), w_f.astype(F32), b_f.astype(F32).reshape(1, heads), ltri)


def _fox_kernel(q_ref, qx_ref, k_ref, kx_ref, vt_ref, o_ref, s_a, s_b, p_a, p_b, acc_ref):
    tq, tk = FOX_Q_TILE, FOX_K_TILE
    lanes = HEAD_DIM
    i = pl.program_id(2)
    q = jnp.concatenate([q_ref[0], qx_ref[0]], axis=1)

    def logits_into(j, s_ref):
        st = pl.multiple_of(j * tk, tk)
        kk = jnp.concatenate([k_ref[0, pl.ds(st, tk), :], kx_ref[0, pl.ds(st, tk), :]], axis=1)
        s_ref[...] = _dot_nt(kk, q)

    def softmax_pv(j, s_ref, p_ref, m, l, diagonal):
        m_out, l_out, alphas = [], [], []
        for c in range(tq // lanes):
            cols = slice(c * lanes, (c + 1) * lanes)
            s = s_ref[:, cols]
            if diagonal:
                key = lax.broadcasted_iota(jnp.int32, (tk, lanes), 0) + j * tk
                qry = lax.broadcasted_iota(jnp.int32, (tk, lanes), 1) + (i * tq + c * lanes)
                s = jnp.where(key <= qry, s, NEG_INF)
            m_new = jnp.maximum(m[:, cols], jnp.max(s, axis=0, keepdims=True))
            alpha = jnp.exp(m[:, cols] - m_new)
            p = jnp.exp(s - m_new)
            l_out.append(alpha * l[:, cols] + jnp.sum(p, axis=0, keepdims=True))
            m_out.append(m_new)
            alphas.append(alpha)
            p_ref[:, cols] = p.astype(BF16)
        vt = vt_ref[0, :, pl.ds(pl.multiple_of(j * tk, tk), tk)]
        acc_ref[...] = jnp.concatenate(alphas, axis=1) * acc_ref[...] + _dot(vt, p_ref[...])
        return jnp.concatenate(m_out, axis=1), jnp.concatenate(l_out, axis=1)

    def pair(jp, carry):
        m, l = carry
        t0 = 2 * jp
        logits_into(t0 + 1, s_b)
        m, l = softmax_pv(t0, s_a, p_a, m, l, False)
        logits_into(t0 + 2, s_a)
        return softmax_pv(t0 + 1, s_b, p_b, m, l, False)

    assert tq == 2 * tk
    acc_ref[...] = jnp.zeros_like(acc_ref)
    logits_into(0, s_a)
    m, l = lax.fori_loop(0, i, pair, (jnp.full((1, tq), NEG_INF, F32), jnp.zeros((1, tq), F32)))
    t0 = 2 * i
    logits_into(t0 + 1, s_b)
    m, l = softmax_pv(t0, s_a, p_a, m, l, True)
    m, l = softmax_pv(t0 + 1, s_b, p_b, m, l, True)
    o_ref[0] = jnp.transpose(acc_ref[...] / l).astype(o_ref.dtype)


def _fox_attention(qkv, qx, kx, bsz, seq, heads):
    tq = FOX_Q_TILE
    assert seq % tq == 0 and tq % FOX_K_TILE == 0
    width = heads * HEAD_DIM
    view = qkv.reshape(bsz, seq, 3 * width)
    v_t = jnp.swapaxes(view[:, :, 2 * width:], 1, 2)
    q_spec = lambda off: pl.BlockSpec((1, tq, HEAD_DIM), lambda b, h, i: (b, i, off + h))
    kv_spec = lambda off: pl.BlockSpec((1, seq, HEAD_DIM), lambda b, h, i: (b, 0, off + h))
    o = pl.pallas_call(
        _fox_kernel,
        grid=(bsz, heads, seq // tq),
        in_specs=[q_spec(0), q_spec(0), kv_spec(heads), kv_spec(0),
                  pl.BlockSpec((1, HEAD_DIM, seq), lambda b, h, i: (b, h, 0))],
        out_specs=pl.BlockSpec((1, tq, HEAD_DIM), lambda b, h, i: (b, i, h)),
        out_shape=jax.ShapeDtypeStruct((bsz, seq, width), BF16),
        scratch_shapes=[pltpu.VMEM((FOX_K_TILE, tq), F32), pltpu.VMEM((FOX_K_TILE, tq), F32),
                        pltpu.VMEM((FOX_K_TILE, tq), BF16), pltpu.VMEM((FOX_K_TILE, tq), BF16),
                        pltpu.VMEM((HEAD_DIM, tq), F32)],
        compiler_params=_params("parallel", "parallel", "parallel"),
        name="fox_attention",
    )(view, qx, view, kx, v_t)
    return o.reshape(bsz * seq, width)


def _router_kernel(x_ref, wt_ref, bias_ref, utri_ref, eidx_ref, rank_ref, w_ref, cnt_ref, carry):
    gsz = GROUP_SIZE
    tt = x_ref.shape[0]

    @pl.when(pl.program_id(0) == 0)
    def _():
        carry[...] = jnp.zeros_like(carry)

    x = x_ref[...]
    xh = x.astype(BF16)
    xl = (x - xh.astype(F32)).astype(BF16)
    wt = wt_ref[...]
    wh = wt.astype(BF16)
    wl = (wt - wh.astype(F32)).astype(BF16)
    logits = _dot_nt(wh, xh) + _dot_nt(wh, xl) + _dot_nt(wl, xh)
    scores = _sigmoid(logits)
    sel = scores + bias_ref[...]
    sub = lax.broadcasted_iota(jnp.int32, (gsz, tt), 0)
    neg = -jnp.inf

    group_score = []
    for g in range(N_GROUPS):
        blk = sel[g * gsz:(g + 1) * gsz, :]
        m1 = jnp.max(blk, axis=0, keepdims=True)
        first = jnp.min(jnp.where(blk == m1, sub, gsz), axis=0, keepdims=True)
        m2 = jnp.max(jnp.where(sub == first, neg, blk), axis=0, keepdims=True)
        group_score.append(m1 + m2)

    chosen = [jnp.zeros((1, tt), F32) for _ in range(N_GROUPS)]
    for _ in range(TOPK_GROUPS):
        m = functools.reduce(jnp.maximum, group_score)
        gi = jnp.full((1, tt), N_GROUPS, jnp.int32)
        for g in reversed(range(N_GROUPS)):
            gi = jnp.where(group_score[g] == m, g, gi)
        for g in range(N_GROUPS):
            hit = gi == g
            chosen[g] = jnp.where(hit, 1.0, chosen[g])
            group_score[g] = jnp.where(hit, neg, group_score[g])

    cand = [jnp.where(jnp.broadcast_to(chosen[g], (gsz, tt)) > 0.5, sel[g * gsz:(g + 1) * gsz, :], NEG_INF)
            for g in range(N_GROUPS)]
    score_g = [scores[g * gsz:(g + 1) * gsz, :] for g in range(N_GROUPS)]
    eiota = [sub + g * gsz for g in range(N_GROUPS)]
    picked = [jnp.zeros((gsz, tt), F32) for _ in range(N_GROUPS)]
    top_idx, top_score = [], []
    for _ in range(TOP_K):
        m = functools.reduce(jnp.maximum, [jnp.max(c, axis=0, keepdims=True) for c in cand])
        ei = functools.reduce(jnp.minimum, [
            jnp.min(jnp.where(cand[g] == m, eiota[g], N_EXPERTS), axis=0, keepdims=True)
            for g in range(N_GROUPS)])
        sc = jnp.zeros((1, tt), F32)
        for g in range(N_GROUPS):
            hit = eiota[g] == ei
            picked[g] = jnp.where(hit, 1.0, picked[g])
            cand[g] = jnp.where(hit, neg, cand[g])
            sc = sc + jnp.sum(jnp.where(hit, score_g[g], 0.0), axis=0, keepdims=True)
        top_idx.append(ei)
        top_score.append(sc)

    total = functools.reduce(lambda a, b: a + b, top_score)
    mask = jnp.concatenate(picked, axis=0)
    incl = _dot(mask.astype(BF16), utri_ref[...])
    rank = carry[...] + incl - mask
    for r in range(TOP_K):
        ei = top_idx[r]
        rk = jnp.zeros((1, tt), F32)
        for g in range(N_GROUPS):
            rk = rk + jnp.sum(jnp.where(eiota[g] == ei, rank[g * gsz:(g + 1) * gsz, :], 0.0),
                              axis=0, keepdims=True)
        eidx_ref[r:r + 1, :] = ei
        rank_ref[r:r + 1, :] = rk.astype(jnp.int32)
        w_ref[r:r + 1, :] = top_score[r] / total * ROUTED_SCALE
    new_carry = carry[...] + incl[:, tt - 1:tt]
    carry[...] = new_carry
    cnt_ref[...] = jnp.broadcast_to(new_carry, cnt_ref.shape)


def _router(x, router_w, router_bias):
    n, d = x.shape
    tt = min(ROUTER_TILE, n)
    idx = np.arange(tt)
    utri = jnp.asarray(idx[:, None] <= idx[None, :], BF16)
    eidx, rank, w8, cnt = pl.pallas_call(
        _router_kernel,
        grid=(n // tt,),
        in_specs=[pl.BlockSpec((tt, d), lambda i: (i, 0)),
                  pl.BlockSpec((N_EXPERTS, d), lambda i: (0, 0)),
                  pl.BlockSpec((N_EXPERTS, 1), lambda i: (0, 0)),
                  pl.BlockSpec((tt, tt), lambda i: (0, 0))],
        out_specs=[pl.BlockSpec((TOP_K, tt), lambda i: (0, i)),
                   pl.BlockSpec((TOP_K, tt), lambda i: (0, i)),
                   pl.BlockSpec((TOP_K, tt), lambda i: (0, i)),
                   pl.BlockSpec((N_EXPERTS, 128), lambda i: (0, 0))],
        out_shape=[jax.ShapeDtypeStruct((TOP_K, n), jnp.int32),
                   jax.ShapeDtypeStruct((TOP_K, n), jnp.int32),
                   jax.ShapeDtypeStruct((TOP_K, n), F32),
                   jax.ShapeDtypeStruct((N_EXPERTS, 128), F32)],
        scratch_shapes=[pltpu.VMEM((N_EXPERTS, 1), F32)],
        compiler_params=_params("arbitrary"),
        name="moe_router",
    )(x, router_w.astype(F32).T, router_bias.astype(F32).reshape(N_EXPERTS, 1), utri)
    return eidx, rank, w8, cnt[:, 0].astype(jnp.int32)


def _dispatch_kernel(dest_ref, x_ref, xs_hbm, sem):
    tt = x_ref.shape[0]

    def copy(t, k):
        return pltpu.make_async_copy(x_ref.at[pl.ds(t, 1), :],
                                     xs_hbm.at[pl.ds(dest_ref[0, k, t], 1), :], sem)

    def issue(t, carry):
        for k in range(TOP_K):
            copy(t, k).start()
        return carry

    lax.fori_loop(0, tt, issue, 0)
    rows = xs_hbm.at[pl.ds(0, TOP_K * tt), :]
    pltpu.make_async_copy(rows, rows, sem).wait()


def _dispatch(x, dest_tiles, n_rows):
    n, d = x.shape
    tt = dest_tiles.shape[2]
    return pl.pallas_call(
        _dispatch_kernel,
        grid=(n // tt,),
        in_specs=[pl.BlockSpec((1, TOP_K, tt), lambda i: (i, 0, 0), memory_space=pltpu.SMEM),
                  pl.BlockSpec((tt, d), lambda i: (i, 0))],
        out_specs=pl.BlockSpec(memory_space=pl.ANY),
        out_shape=jax.ShapeDtypeStruct((n_rows, d), x.dtype),
        scratch_shapes=[pltpu.SemaphoreType.DMA(())],
        compiler_params=_params("arbitrary"),
        name="moe_dispatch",
    )(dest_tiles, x)


def _expert_kernel(te_ref, tv_ref, nt_ref, xs_ref, wg_ref, wu_ref, wd_ref, y_ref,
                   wg_bf, wu_bf, wd_bf):
    i = pl.program_id(0)
    live = i < nt_ref[0]
    new_expert = jnp.logical_or(i == 0, te_ref[i] != te_ref[jnp.maximum(i - 1, 0)])

    @pl.when(jnp.logical_and(live, new_expert))
    def _():
        wg_bf[...] = wg_ref[0, 0].astype(BF16)
        wu_bf[...] = wu_ref[0, 0].astype(BF16)
        wd_bf[...] = wd_ref[0, 0].astype(BF16)

    @pl.when(live)
    def _():
        tm, half = xs_ref.shape
        row = lax.broadcasted_iota(jnp.int32, (tm, 1), 0)
        x_lo, x_hi = _unpack_bf16_pair(jnp.where(row < tv_ref[i], xs_ref[...], 0))
        x_lo, x_hi = x_lo.astype(BF16), x_hi.astype(BF16)
        gate = _dot(x_lo, wg_bf[:half, :]) + _dot(x_hi, wg_bf[half:, :])
        up = _dot(x_lo, wu_bf[:half, :]) + _dot(x_hi, wu_bf[half:, :])
        hidden = (_silu(gate) * up).astype(BF16)
        y = _dot(hidden, wd_bf[...])
        y_ref[...] = _pack_bf16_pair(y[:, :half], y[:, half:])


def _expert_ffn(xs, w_gate, w_up, w_down, layer, tile_expert, tile_valid, num_tiles):
    p, half = xs.shape
    d = 2 * half
    tm = EXPERT_TILE
    e_dim = w_gate.shape[3]
    row_map = lambda i, te, tv, nt: (jnp.minimum(i, nt[0] - 1), 0)
    w_map = lambda i, te, tv, nt: (layer, te[i], 0, 0)
    return pl.pallas_call(
        _expert_kernel,
        grid_spec=pltpu.PrefetchScalarGridSpec(
            num_scalar_prefetch=3,
            grid=(p // tm,),
            in_specs=[pl.BlockSpec((tm, half), row_map),
                      pl.BlockSpec((1, 1, d, e_dim), w_map),
                      pl.BlockSpec((1, 1, d, e_dim), w_map),
                      pl.BlockSpec((1, 1, e_dim, d), w_map)],
            out_specs=pl.BlockSpec((tm, half), row_map),
            scratch_shapes=[pltpu.VMEM((d, e_dim), BF16), pltpu.VMEM((d, e_dim), BF16),
                            pltpu.VMEM((e_dim, d), BF16)]),
        out_shape=jax.ShapeDtypeStruct((p, half), jnp.int32),
        compiler_params=_params("arbitrary"),
        name="moe_expert_ffn",
    )(tile_expert, tile_valid, num_tiles, xs, w_gate, w_up, w_down)


def _combine_kernel(dest_ref, x_ref, w8_ref, sg_ref, su_ref, sd_ref, g_ref, b_ref, y_hbm,
                    o_ref, obf_ref, ybuf, sem):
    tc = x_ref.shape[0]

    def copy(t, k):
        return pltpu.make_async_copy(y_hbm.at[pl.ds(dest_ref[0, k, t], 1), :],
                                     ybuf.at[k, pl.ds(t, 1), :], sem)

    def issue(t, carry):
        for k in range(TOP_K):
            copy(t, k).start()
        return carry

    lax.fori_loop(0, tc, issue, 0)
    x = x_ref[...]
    xb = x.astype(BF16)
    hidden = (_silu(_dot(xb, sg_ref[...])) * _dot(xb, su_ref[...])).astype(BF16)
    acc = _dot(hidden, sd_ref[...])
    pltpu.make_async_copy(ybuf, ybuf, sem).wait()
    w8 = w8_ref[...]
    half = ybuf.shape[2]
    routed_lo = jnp.zeros((tc, half), F32)
    routed_hi = jnp.zeros((tc, half), F32)
    for k in range(TOP_K):
        y_lo, y_hi = _unpack_bf16_pair(ybuf[k])
        routed_lo = routed_lo + w8[:, k:k + 1] * y_lo
        routed_hi = routed_hi + w8[:, k:k + 1] * y_hi
    acc = acc + jnp.concatenate([routed_lo, routed_hi], axis=1)
    y = _layer_norm_rows(DN_ALPHA * x + acc, g_ref[...], b_ref[...])
    o_ref[...] = y
    obf_ref[...] = y.astype(BF16)


def _combine(x, w8_rows, dest_tiles, y_sorted, s_gate, s_up, s_down, g, b):
    n, d = x.shape
    tc = dest_tiles.shape[2]
    sdim = s_gate.shape[1]
    const = lambda shape: pl.BlockSpec(shape, lambda i: (0,) * len(shape))
    row = pl.BlockSpec((tc, d), lambda i: (i, 0))
    return pl.pallas_call(
        _combine_kernel,
        grid=(n // tc,),
        in_specs=[pl.BlockSpec((1, TOP_K, tc), lambda i: (i, 0, 0), memory_space=pltpu.SMEM),
                  row, pl.BlockSpec((tc, TOP_K), lambda i: (i, 0)),
                  const((d, sdim)), const((d, sdim)), const((sdim, d)), const((1, d)), const((1, d)),
                  pl.BlockSpec(memory_space=pl.ANY)],
        out_specs=[row, row],
        out_shape=[jax.ShapeDtypeStruct((n, d), F32), jax.ShapeDtypeStruct((n, d), BF16)],
        scratch_shapes=[pltpu.VMEM((TOP_K, tc, d // 2), jnp.int32), pltpu.SemaphoreType.DMA(())],
        compiler_params=_params("arbitrary"),
        name="moe_combine",
    )(dest_tiles, x, w8_rows, s_gate, s_up, s_down, g.reshape(1, d), b.reshape(1, d), y_sorted)


def _tile_major(a, tile):
    k, n = a.shape
    return a.reshape(k, n // tile, tile).transpose(1, 0, 2)


def _moe(x, x_packed, router_w, router_bias, w_gate, w_up, w_down, layer, s_gate, s_up, s_down, ln_g, ln_b):
    n, d = x.shape
    tm = EXPERT_TILE
    eidx, rank, w8, cnt = _router(x, router_w, router_bias)
    tiles_e = (cnt + tm - 1) // tm
    tile_end = jnp.cumsum(tiles_e)
    tile_start = tile_end - tiles_e
    row_start = tile_start * tm
    e_ids = jnp.arange(N_EXPERTS, dtype=jnp.int32)
    dest = jnp.sum(jnp.where(eidx[None] == e_ids[:, None, None], row_start[:, None, None], 0), axis=0) + rank
    max_tiles = (n * TOP_K) // tm + N_EXPERTS
    num_tiles = tile_end[-1]
    tile_ids = jnp.arange(max_tiles, dtype=jnp.int32)
    t_clamped = jnp.minimum(tile_ids, num_tiles - 1)
    tile_expert = jnp.sum(t_clamped[:, None] >= tile_end[None, :], axis=1).astype(jnp.int32)
    tile_valid = jnp.clip(cnt[tile_expert] - (t_clamped - tile_start[tile_expert]) * tm, 0, tm).astype(jnp.int32)

    xs = _dispatch(x_packed, _tile_major(dest, min(DISPATCH_TILE, n)), max_tiles * tm)
    ys = _expert_ffn(xs, w_gate, w_up, w_down, layer, tile_expert, tile_valid,
                     num_tiles.reshape(1).astype(jnp.int32))
    return _combine(x, w8.T, _tile_major(dest, min(COMBINE_TILE, n)), ys, s_gate, s_up, s_down, ln_g, ln_b)


def _even_layer(x, x_bf, bsz, seq, w_in, gn_g, w_out, rel_bias, lb, ln_g, ln_b):
    heads = w_out.shape[0] // (2 * HEAD_DIM)
    width = heads * HEAD_DIM
    proj_width = w_in.shape[1]
    col_scale = jnp.concatenate([jnp.full((1, width), HEAD_DIM ** -0.5, F32),
                                 jnp.ones((1, proj_width - width), F32)], axis=1)
    proj = _matmul(x_bf, w_in, col_scale, F32)
    o_a = _dilated_mixture(proj, rel_bias, bsz, seq, heads, proj_width)
    o_b = _hgrn2(proj, lb, gn_g, bsz, seq, heads, proj_width, first_group=3)
    return _proj_residual_ln([o_a, o_b], [w_out[:width], w_out[width:]], x, ln_g, ln_b)


def _odd_layer(x, x_bf, bsz, seq, w_qkv, w_f, b_f, w_out, ln_g, ln_b):
    heads = w_f.shape[1]
    width = heads * HEAD_DIM
    col_scale = jnp.concatenate([jnp.full((1, width), HEAD_DIM ** -0.5, F32),
                                 jnp.ones((1, 2 * width), F32)], axis=1)
    qkv = _matmul(x_bf, w_qkv, col_scale, BF16)
    qx, kx = _fox_gate_terms(x, w_f, b_f, bsz, seq)
    o = _fox_attention(qkv, qx, kx, bsz, seq, heads)
    return _proj_residual_ln([o], [w_out], x, ln_g, ln_b)


def kernel(x, rel_bias, hgrn_lb_logits, a_w_in, a_gn_g, a_w_out, c_w_in, c_b_f, c_w_out, ln_mix_g, ln_mix_b, ln_ffn_g, ln_ffn_b, router_w, router_bias, exp_w_gate, exp_w_up, exp_w_down, sh_w_gate, sh_w_up, sh_w_down):
    bsz, seq, d = x.shape
    depth = ln_mix_g.shape[0]
    lb_all = jnp.cumsum(jax.nn.softmax(hgrn_lb_logits.astype(F32), axis=0), axis=0)
    h = x.reshape(bsz * seq, d).astype(F32)
    h_bf = h.astype(BF16)
    for layer in range(depth):
        j = layer // 2
        if layer % 2 == 0:
            h, h_packed = _even_layer(h, h_bf, bsz, seq, a_w_in[j].astype(BF16), a_gn_g[j],
                                  a_w_out[j].astype(BF16), rel_bias, lb_all[layer],
                                  ln_mix_g[layer], ln_mix_b[layer])
        else:
            c_width = c_w_out.shape[1]
            w_in = c_w_in[j]
            h, h_packed = _odd_layer(h, h_bf, bsz, seq, w_in[:, :3 * c_width].astype(BF16),
                                 w_in[:, 3 * c_width:], c_b_f[j], c_w_out[j].astype(BF16),
                                 ln_mix_g[layer], ln_mix_b[layer])
        h, h_bf = _moe(h, h_packed, router_w[layer], router_bias[layer], exp_w_gate, exp_w_up, exp_w_down, layer,
                       sh_w_gate[layer].astype(BF16), sh_w_up[layer].astype(BF16),
                       sh_w_down[layer].astype(BF16), ln_ffn_g[layer], ln_ffn_b[layer])
    return h.reshape(bsz, seq, d)
```

```python
import functools

import numpy as np
import jax
import jax.numpy as jnp
from jax import lax
from jax.experimental import pallas as pl
from jax.experimental.pallas import tpu as pltpu

F32 = jnp.float32
BF16 = jnp.bfloat16

HEAD_DIM = 128
A_PATTERNS = ((128, 1), (512, 4), (2048, 16))
A_BLOCK = 128
REL_BUCKETS = 32
REL_MAX_DIST = 2048
N_EXPERTS = 64
TOP_K = 8
N_GROUPS = 8
GROUP_SIZE = N_EXPERTS // N_GROUPS
TOPK_GROUPS = 4
ROUTED_SCALE = 2.5
LN_EPS = 1e-5
RMS_EPS = 1e-6
DEPTH = 2
DN_ALPHA = (2 * DEPTH) ** 0.25
NEG_INF = -1e30

DILATED_UNROLL = 8
HGRN_CHUNK = 128
HGRN_BCAST_MIN = 8
FOX_Q_TILE = 512
FOX_K_TILE = 256
ROUTER_TILE = 512
EXPERT_TILE = 512
COMBINE_TILE = 128
VMEM_LIMIT = 56 * 1024 * 1024

_NT = (((1,), (1,)), ((), ()))
_TN = (((0,), (0,)), ((), ()))


def _dot(a, b):
    return jnp.dot(a, b, preferred_element_type=F32)


def _dot_nt(a, b):
    return lax.dot_general(a, b, _NT, preferred_element_type=F32)


def _dot_tn(a, b):
    return lax.dot_general(a, b, _TN, preferred_element_type=F32)


def _sigmoid(x):
    return 1.0 / (1.0 + jnp.exp(-x))


def _silu(x):
    return x * _sigmoid(x)


def _split3(x, axis):
    hi = x.astype(BF16)
    r1 = x - hi.astype(F32)
    mid = r1.astype(BF16)
    lo = (r1 - mid.astype(F32)).astype(BF16)
    return jnp.concatenate([hi, mid, lo], axis=axis)


def _layer_norm_rows(y, g, b):
    mu = jnp.mean(y, axis=-1, keepdims=True)
    yc = y - mu
    var = jnp.mean(yc * yc, axis=-1, keepdims=True)
    return yc * lax.rsqrt(var + LN_EPS) * g + b


def _params(*sem):
    return pltpu.CompilerParams(dimension_semantics=sem, vmem_limit_bytes=VMEM_LIMIT)


def _mm_kernel(x_ref, w_ref, cs_ref, o_ref):
    o_ref[...] = (_dot(x_ref[...], w_ref[...]) * cs_ref[...]).astype(o_ref.dtype)


def _matmul(x, w, col_scale, out_dtype, tm=1024, tn=1024):
    m, k = x.shape
    nw = w.shape[1]
    tm, tn = min(tm, m), min(tn, nw)
    assert m % tm == 0 and nw % tn == 0
    return pl.pallas_call(
        _mm_kernel,
        grid=(nw // tn, m // tm),
        in_specs=[pl.BlockSpec((tm, k), lambda j, i: (i, 0)),
                  pl.BlockSpec((k, tn), lambda j, i: (0, j)),
                  pl.BlockSpec((1, tn), lambda j, i: (0, j))],
        out_specs=pl.BlockSpec((tm, tn), lambda j, i: (i, j)),
        out_shape=jax.ShapeDtypeStruct((m, nw), out_dtype),
        compiler_params=_params("parallel", "parallel"),
        name="proj_matmul",
    )(x, w, col_scale)


def _proj_ln_kernel(*refs, n_in):
    a_refs, w_refs = refs[:n_in], refs[n_in:2 * n_in]
    r_ref, g_ref, b_ref, o_ref, obf_ref = refs[2 * n_in:]
    acc = _dot(a_refs[0][...], w_refs[0][...])
    for a_ref, w_ref in zip(a_refs[1:], w_refs[1:]):
        acc = acc + _dot(a_ref[...], w_ref[...])
    y = _layer_norm_rows(DN_ALPHA * r_ref[...] + acc, g_ref[...], b_ref[...])
    o_ref[...] = y
    half = y.shape[1] // 2
    obf_ref[...] = _pack_bf16_pair(y[:, :half], y[:, half:])


def _pack_bf16_pair(lo, hi):
    lo_bits = lax.shift_right_logical(lax.bitcast_convert_type(lo.astype(BF16).astype(F32), jnp.int32), 16)
    hi_bits = lax.bitcast_convert_type(hi.astype(BF16).astype(F32), jnp.int32) & jnp.int32(-65536)
    return lo_bits | hi_bits


def _unpack_bf16_pair(words):
    lo = lax.bitcast_convert_type(lax.shift_left(words, 16), F32)
    hi = lax.bitcast_convert_type(words & jnp.int32(-65536), F32)
    return lo, hi


def _proj_residual_ln(acts, weights, resid, g, b, tm=256):
    m, d = resid.shape
    tm = min(tm, m)
    n_in = len(acts)
    in_specs = [pl.BlockSpec((tm, a.shape[1]), lambda i: (i, 0)) for a in acts]
    in_specs += [pl.BlockSpec(w.shape, lambda i: (0, 0)) for w in weights]
    in_specs += [pl.BlockSpec((tm, d), lambda i: (i, 0)),
                 pl.BlockSpec((1, d), lambda i: (0, 0)),
                 pl.BlockSpec((1, d), lambda i: (0, 0))]
    return pl.pallas_call(
        functools.partial(_proj_ln_kernel, n_in=n_in),
        grid=(m // tm,),
        in_specs=in_specs,
        out_specs=[pl.BlockSpec((tm, d), lambda i: (i, 0)), pl.BlockSpec((tm, d // 2), lambda i: (i, 0))],
        out_shape=[jax.ShapeDtypeStruct((m, d), F32), jax.ShapeDtypeStruct((m, d // 2), jnp.int32)],
        compiler_params=_params("parallel"),
        name="out_proj_ln",
    )(*acts, *weights, resid, g.reshape(1, d), b.reshape(1, d))


def _t5_bucket(dist):
    max_exact = REL_BUCKETS // 2
    d = np.maximum(dist, 1).astype(np.float64)
    large = max_exact + (np.log(d / max_exact) / np.log(REL_MAX_DIST / max_exact)
                         * (REL_BUCKETS - max_exact)).astype(np.int64)
    large = np.minimum(large, REL_BUCKETS - 1)
    return np.where(dist < max_exact, dist, large).astype(np.int32)


def _dilated_kernel(q_ref, k_ref, v_ref, bucket_ref, rel_bias_ref, out_ref, o_scr, lse_scr):
    blk = A_BLOCK
    seq = q_ref.shape[1]
    head = pl.program_id(1)
    qi = lax.broadcasted_iota(jnp.int32, (blk, 2 * blk), 0)
    ki = lax.broadcasted_iota(jnp.int32, (blk, 2 * blk), 1)
    rel = qi + blk - ki

    for bi, (window, dil) in enumerate(A_PATTERNS):
        band = window // dil
        n_blocks = seq // (dil * blk)
        valid = jnp.abs(2 * rel - band) <= band
        bucket = bucket_ref[bi]
        bias = jnp.zeros((blk, 2 * blk), F32)
        for b in range(REL_BUCKETS):
            bias = jnp.where(bucket == b, rel_bias_ref[head, b], bias)

        def attend(q_start, k_start, n_keys, bias_blk, valid_blk, bi=bi, dil=dil):
            q_rows = pl.ds(q_start, blk, stride=dil)
            k_rows = pl.ds(k_start, n_keys, stride=dil)
            q = q_ref[0, q_rows, :].astype(BF16)
            kk = k_ref[0, k_rows, :].astype(BF16)
            vv = v_ref[0, k_rows, :].astype(BF16)
            s = _dot_nt(q, kk) + bias_blk
            s = jnp.where(valid_blk, s, NEG_INF)
            m = jnp.max(s, axis=-1, keepdims=True)
            p = jnp.exp(s - m)
            l = jnp.sum(p, axis=-1, keepdims=True)
            o_scr[bi, q_rows, :] = _dot(p.astype(BF16), vv) / l
            lse_scr[bi, q_rows, :] = jnp.broadcast_to(m + jnp.log(l), (blk, HEAD_DIM))

        def first(r, carry, attend=attend, bias=bias, valid=valid):
            attend(r, r, blk, bias[:, blk:], valid[:, blk:])
            return carry

        lax.fori_loop(0, dil, first, 0, unroll=min(dil, DILATED_UNROLL))

        def rest(t, carry, attend=attend, bias=bias, valid=valid, dil=dil):
            r = t & (dil - 1)
            n = (t >> (dil.bit_length() - 1)) + 1
            attend(r + n * blk * dil, r + (n - 1) * blk * dil, 2 * blk, bias, valid)
            return carry

        lax.fori_loop(0, dil * (n_blocks - 1), rest, 0, unroll=DILATED_UNROLL)

    chunk = 2 * blk
    nb = len(A_PATTERNS)

    def mix(c, carry):
        rows = pl.ds(pl.multiple_of(c * chunk, chunk), chunk)
        lses = [lse_scr[b, rows, :] for b in range(nb)]
        m = functools.reduce(jnp.maximum, lses)
        es = [jnp.exp(x - m) for x in lses]
        den = functools.reduce(lambda a, b: a + b, es)
        num = functools.reduce(lambda a, b: a + b, [e * o_scr[b, rows, :] for b, e in enumerate(es)])
        out_ref[0, rows, :] = (num / den).astype(out_ref.dtype)
        return carry

    lax.fori_loop(0, seq // chunk, mix, 0)


def _dilated_bucket_tables():
    qi = np.arange(A_BLOCK)[:, None]
    ki = np.arange(2 * A_BLOCK)[None, :]
    rel = qi + A_BLOCK - ki
    return np.stack([_t5_bucket(np.clip(rel, 0, None) * dilation) for _, dilation in A_PATTERNS], axis=0)


def _dilated_mixture(proj, rel_bias, bsz, seq, heads, proj_width):
    for window, dilation in A_PATTERNS:
        assert seq % (dilation * A_BLOCK) == 0 and window // dilation <= A_BLOCK
    nb = len(A_PATTERNS)
    view = proj.reshape(bsz, seq, proj_width)
    buckets = jnp.asarray(_dilated_bucket_tables())

    def in_spec(group):
        return pl.BlockSpec((1, seq, HEAD_DIM), lambda b, h, g=group: (b, 0, g * heads + h))

    o = pl.pallas_call(
        _dilated_kernel,
        grid=(bsz, heads),
        in_specs=[in_spec(0), in_spec(1), in_spec(2),
                  pl.BlockSpec(buckets.shape, lambda b, h: (0, 0, 0)),
                  pl.BlockSpec(memory_space=pltpu.SMEM)],
        out_specs=pl.BlockSpec((1, seq, HEAD_DIM), lambda b, h: (b, 0, h)),
        out_shape=jax.ShapeDtypeStruct((bsz, seq, heads * HEAD_DIM), BF16),
        scratch_shapes=[pltpu.VMEM((nb, seq, HEAD_DIM), F32), pltpu.VMEM((nb, seq, HEAD_DIM), F32)],
        compiler_params=_params("parallel", "parallel"),
        name="dilated_attention",
    )(view, view, view, buckets, rel_bias.astype(F32).T)
    return o.reshape(bsz * seq, heads * HEAD_DIM)


def _hgrn_levels(t):
    cs = []
    c = t // 2
    while c >= 1:
        cs.append(c)
        c //= 2
    return cs


def _hgrn_consts(t):
    idx = np.arange(t)
    ltri = (idx[:, None] >= idx[None, :]).astype(np.float32)
    sels = []
    for c in _hgrn_levels(t):
        if c < HGRN_BCAST_MIN:
            ref = (idx // (2 * c)) * (2 * c) + c - 1
            sels.append((idx[None, :] == ref[:, None]).astype(np.float32))
    return jnp.asarray(ltri, BF16), jnp.asarray(np.concatenate(sels, axis=0), BF16)


def _hgrn_kernel(q_ref, f_ref, i_ref, g_ref, lb_ref, gn_ref, ltri_ref, sel_ref, o_ref, state_ref,
                 *, n_chunks):
    t = HGRN_CHUNK
    dk = HEAD_DIM
    levels = _hgrn_levels(t)
    lb = lb_ref[0]
    gn = gn_ref[0]
    ltri = ltri_ref[...]
    sel = sel_ref[...]
    row = lax.broadcasted_iota(jnp.int32, (t, dk), 0)
    ti = lax.broadcasted_iota(jnp.int32, (t, t), 0)
    si = lax.broadcasted_iota(jnp.int32, (t, t), 1)
    state_ref[...] = jnp.zeros_like(state_ref)

    def sum3(y):
        return y[:, :dk] + y[:, dk:2 * dk] + y[:, 2 * dk:]

    def chunk(n, carry):
        st = pl.multiple_of(n * t, t)
        fq = _silu(q_ref[0, pl.ds(st, t), :])
        forget = lb + (1.0 - lb) * _sigmoid(f_ref[0, pl.ds(st, t), :])
        logf = jnp.log(forget)
        key = 1.0 - forget
        val = i_ref[0, pl.ds(st, t), :].astype(BF16)
        a = sum3(_dot(ltri, _split3(logf, 1)))
        aref_fine = sum3(_dot(sel, _split3(a, 1)))
        fq_b = fq.astype(BF16)
        key_b = key.astype(BF16)
        scores = jnp.where(ti == si, _dot_nt(fq_b, key_b), 0.0)
        n_fine = 0
        for c in levels:
            if c >= HGRN_BCAST_MIN:
                aref = jnp.concatenate(
                    [jnp.broadcast_to(a[b0 + c - 1:b0 + c, :], (2 * c, dk)) for b0 in range(0, t, 2 * c)],
                    axis=0)
            else:
                aref = aref_fine[n_fine * t:(n_fine + 1) * t, :]
                n_fine += 1
            upper = (row & (2 * c - 1)) >= c
            decayed = jnp.where(upper, fq, key) * jnp.exp(-jnp.abs(a - aref))
            ql = jnp.where(upper, decayed, 0.0).astype(BF16)
            kl = jnp.where(upper, 0.0, decayed).astype(BF16)
            s_l = _dot_nt(ql, kl)
            if 2 * c < t:
                sh = (2 * c).bit_length() - 1
                s_l = jnp.where((ti >> sh) == (si >> sh), s_l, 0.0)
            scores = scores + s_l
        state_t = state_ref[...]
        inter = _dot_nt((fq * jnp.exp(a)).astype(BF16), state_t.astype(BF16))
        o = inter + _dot(scores.astype(BF16), val)
        a_last = a[t - 1:t, :]
        k_dec = (key * jnp.exp(a_last - a)).astype(BF16)
        state_ref[...] = state_t * jnp.exp(a_last) + _dot_tn(val, k_dec)
        o = o * lax.rsqrt(jnp.mean(o * o, axis=-1, keepdims=True) + RMS_EPS) * gn
        o = o * _silu(g_ref[0, pl.ds(st, t), :])
        o_ref[0, pl.ds(st, t), :] = o.astype(o_ref.dtype)
        return carry

    lax.fori_loop(0, n_chunks, chunk, 0, unroll=2)


def _hgrn2(proj, lb, gn_g, bsz, seq, heads, proj_width, first_group):
    assert seq % HGRN_CHUNK == 0
    view = proj.reshape(bsz, seq, proj_width)
    ltri, sel = _hgrn_consts(HGRN_CHUNK)

    def in_spec(group):
        return pl.BlockSpec((1, seq, HEAD_DIM), lambda b, h, g=group: (b, 0, (first_group + g) * heads + h))

    head_vec = pl.BlockSpec((1, 1, HEAD_DIM), lambda b, h: (h, 0, 0))
    o = pl.pallas_call(
        functools.partial(_hgrn_kernel, n_chunks=seq // HGRN_CHUNK),
        grid=(bsz, heads),
        in_specs=[in_spec(0), in_spec(1), in_spec(2), in_spec(3), head_vec, head_vec,
                  pl.BlockSpec(ltri.shape, lambda b, h: (0, 0)),
                  pl.BlockSpec(sel.shape, lambda b, h: (0, 0))],
        out_specs=pl.BlockSpec((1, seq, HEAD_DIM), lambda b, h: (b, 0, h)),
        out_shape=jax.ShapeDtypeStruct((bsz, seq, heads * HEAD_DIM), BF16),
        scratch_shapes=[pltpu.VMEM((HEAD_DIM, HEAD_DIM), F32)],
        compiler_params=_params("parallel", "parallel"),
        name="hgrn2",
    )(view, view, view, view, lb.reshape(heads, 1, HEAD_DIM).astype(F32),
      gn_g.reshape(heads, 1, HEAD_DIM).astype(F32), ltri, sel)
    return o.reshape(bsz * seq, heads * HEAD_DIM)


def _fox_gate_kernel(x_ref, w_ref, b_ref, ltri_ref, qx_ref, kx_ref, carry):
    heads = w_ref.shape[1]
    ts = x_ref.shape[1]

    @pl.when(pl.program_id(1) == 0)
    def _():
        carry[...] = jnp.zeros_like(carry)

    def log_sigmoid(z):
        return jnp.minimum(z, 0.0) - jnp.log(1.0 + jnp.exp(-jnp.abs(z)))

    x = x_ref[0]
    xh = x.astype(BF16)
    xl = (x - xh.astype(F32)).astype(BF16)
    w = w_ref[...]
    wh = w.astype(BF16)
    wl = (w - wh.astype(F32)).astype(BF16)
    f = _dot(xh, wh) + _dot(xh, wl) + _dot(xl, wh) + b_ref[...]
    c = _dot(ltri_ref[...], _split3(log_sigmoid(f), 1))
    c = c[:, :heads] + c[:, heads:2 * heads] + c[:, 2 * heads:] + carry[...]
    carry[...] = c[ts - 1:ts, :]
    lane = lax.broadcasted_iota(jnp.int32, (ts, HEAD_DIM), 1)
    for h in range(heads):
        ch = c[:, h:h + 1]
        hi = ch.astype(BF16).astype(F32)
        r1 = ch - hi
        mid = r1.astype(BF16).astype(F32)
        lo = r1 - mid
        qx = jnp.where(lane == 0, hi, jnp.where(lane == 1, mid, jnp.where(lane == 2, lo,
                       jnp.where(lane < 6, 1.0, 0.0))))
        kx = jnp.where(lane < 3, 1.0, jnp.where(lane == 3, -hi, jnp.where(lane == 4, -mid,
                       jnp.where(lane == 5, -lo, 0.0))))
        qx_ref[0, :, h * HEAD_DIM:(h + 1) * HEAD_DIM] = qx.astype(BF16)
        kx_ref[0, :, h * HEAD_DIM:(h + 1) * HEAD_DIM] = kx.astype(BF16)


def _fox_gate_terms(x, w_f, b_f, bsz, seq, ts=512):
    d = x.shape[-1]
    heads = w_f.shape[1]
    ts = min(ts, seq)
    idx = np.arange(ts)
    ltri = jnp.asarray(idx[:, None] >= idx[None, :], BF16)
    const = lambda shape: pl.BlockSpec(shape, lambda b, i: (0,) * len(shape))
    out_spec = pl.BlockSpec((1, ts, heads * HEAD_DIM), lambda b, i: (b, i, 0))
    out_sds = jax.ShapeDtypeStruct((bsz, seq, heads * HEAD_DIM), BF16)
    return pl.pallas_call(
        _fox_gate_kernel,
        grid=(bsz, seq // ts),
        in_specs=[pl.BlockSpec((1, ts, d), lambda b, i: (b, i, 0)),
                  const((d, heads)), const((1, heads)), const((ts, ts))],
        out_specs=[out_spec, out_spec],
        out_shape=[out_sds, out_sds],
        scratch_shapes=[pltpu.VMEM((1, heads), F32)],
        compiler_params=_params("parallel", "arbitrary"),
        name="fox_gate_cumsum",
    )(x.reshape(bsz, seq, d), w_f.astype(F32), b_f.astype(F32).reshape(1, heads), ltri)


def _fox_kernel(q_ref, qx_ref, k_ref, kx_ref, vt_ref, o_ref, s_a, s_b, p_a, p_b, acc_ref):
    tq, tk = FOX_Q_TILE, FOX_K_TILE
    lanes = HEAD_DIM
    i = pl.program_id(2)
    q = jnp.concatenate([q_ref[0], qx_ref[0]], axis=1)

    def logits_into(j, s_ref):
        st = pl.multiple_of(j * tk, tk)
        kk = jnp.concatenate([k_ref[0, pl.ds(st, tk), :], kx_ref[0, pl.ds(st, tk), :]], axis=1)
        s_ref[...] = _dot_nt(kk, q)

    def softmax_pv(j, s_ref, p_ref, m, l, diagonal):
        m_out, l_out, alphas = [], [], []
        for c in range(tq // lanes):
            cols = slice(c * lanes, (c + 1) * lanes)
            s = s_ref[:, cols]
            if diagonal:
                key = lax.broadcasted_iota(jnp.int32, (tk, lanes), 0) + j * tk
                qry = lax.broadcasted_iota(jnp.int32, (tk, lanes), 1) + (i * tq + c * lanes)
                s = jnp.where(key <= qry, s, NEG_INF)
            m_new = jnp.maximum(m[:, cols], jnp.max(s, axis=0, keepdims=True))
            alpha = jnp.exp(m[:, cols] - m_new)
            p = jnp.exp(s - m_new)
            l_out.append(alpha * l[:, cols] + jnp.sum(p, axis=0, keepdims=True))
            m_out.append(m_new)
            alphas.append(alpha)
            p_ref[:, cols] = p.astype(BF16)
        vt = vt_ref[0, :, pl.ds(pl.multiple_of(j * tk, tk), tk)]
        acc_ref[...] = jnp.concatenate(alphas, axis=1) * acc_ref[...] + _dot(vt, p_ref[...])
        return jnp.concatenate(m_out, axis=1), jnp.concatenate(l_out, axis=1)

    def pair(jp, carry):
        m, l = carry
        t0 = 2 * jp
        logits_into(t0 + 1, s_b)
        m, l = softmax_pv(t0, s_a, p_a, m, l, False)
        logits_into(t0 + 2, s_a)
        return softmax_pv(t0 + 1, s_b, p_b, m, l, False)

    assert tq == 2 * tk
    acc_ref[...] = jnp.zeros_like(acc_ref)
    logits_into(0, s_a)
    m, l = lax.fori_loop(0, i, pair, (jnp.full((1, tq), NEG_INF, F32), jnp.zeros((1, tq), F32)))
    t0 = 2 * i
    logits_into(t0 + 1, s_b)
    m, l = softmax_pv(t0, s_a, p_a, m, l, True)
    m, l = softmax_pv(t0 + 1, s_b, p_b, m, l, True)
    o_ref[0] = jnp.transpose(acc_ref[...] / l).astype(o_ref.dtype)


def _fox_attention(qkv, qx, kx, bsz, seq, heads):
    tq = FOX_Q_TILE
    assert seq % tq == 0 and tq % FOX_K_TILE == 0
    width = heads * HEAD_DIM
    view = qkv.reshape(bsz, seq, 3 * width)
    v_t = jnp.swapaxes(view[:, :, 2 * width:], 1, 2)
    q_spec = lambda off: pl.BlockSpec((1, tq, HEAD_DIM), lambda b, h, i: (b, i, off + h))
    kv_spec = lambda off: pl.BlockSpec((1, seq, HEAD_DIM), lambda b, h, i: (b, 0, off + h))
    o = pl.pallas_call(
        _fox_kernel,
        grid=(bsz, heads, seq // tq),
        in_specs=[q_spec(0), q_spec(0), kv_spec(heads), kv_spec(0),
                  pl.BlockSpec((1, HEAD_DIM, seq), lambda b, h, i: (b, h, 0))],
        out_specs=pl.BlockSpec((1, tq, HEAD_DIM), lambda b, h, i: (b, i, h)),
        out_shape=jax.ShapeDtypeStruct((bsz, seq, width), BF16),
        scratch_shapes=[pltpu.VMEM((FOX_K_TILE, tq), F32), pltpu.VMEM((FOX_K_TILE, tq), F32),
                        pltpu.VMEM((FOX_K_TILE, tq), BF16), pltpu.VMEM((FOX_K_TILE, tq), BF16),
                        pltpu.VMEM((HEAD_DIM, tq), F32)],
        compiler_params=_params("parallel", "parallel", "parallel"),
        name="fox_attention",
    )(view, qx, view, kx, v_t)
    return o.reshape(bsz * seq, width)


def _router_kernel(x_ref, wt_ref, bias_ref, utri_ref, eidx_ref, rank_ref, w_ref, cnt_ref, carry):
    gsz = GROUP_SIZE
    tt = x_ref.shape[0]

    @pl.when(pl.program_id(0) == 0)
    def _():
        carry[...] = jnp.zeros_like(carry)

    x = x_ref[...]
    xh = x.astype(BF16)
    xl = (x - xh.astype(F32)).astype(BF16)
    wt = wt_ref[...]
    wh = wt.astype(BF16)
    wl = (wt - wh.astype(F32)).astype(BF16)
    logits = _dot_nt(wh, xh) + _dot_nt(wh, xl) + _dot_nt(wl, xh)
    scores = _sigmoid(logits)
    sel = scores + bias_ref[...]
    sub = lax.broadcasted_iota(jnp.int32, (gsz, tt), 0)
    neg = -jnp.inf

    group_score = []
    for g in range(N_GROUPS):
        blk = sel[g * gsz:(g + 1) * gsz, :]
        m1 = jnp.max(blk, axis=0, keepdims=True)
        first = jnp.min(jnp.where(blk == m1, sub, gsz), axis=0, keepdims=True)
        m2 = jnp.max(jnp.where(sub == first, neg, blk), axis=0, keepdims=True)
        group_score.append(m1 + m2)

    chosen = [jnp.zeros((1, tt), F32) for _ in range(N_GROUPS)]
    for _ in range(TOPK_GROUPS):
        m = functools.reduce(jnp.maximum, group_score)
        gi = jnp.full((1, tt), N_GROUPS, jnp.int32)
        for g in reversed(range(N_GROUPS)):
            gi = jnp.where(group_score[g] == m, g, gi)
        for g in range(N_GROUPS):
            hit = gi == g
            chosen[g] = jnp.where(hit, 1.0, chosen[g])
            group_score[g] = jnp.where(hit, neg, group_score[g])

    cand = [jnp.where(jnp.broadcast_to(chosen[g], (gsz, tt)) > 0.5, sel[g * gsz:(g + 1) * gsz, :], NEG_INF)
            for g in range(N_GROUPS)]
    score_g = [scores[g * gsz:(g + 1) * gsz, :] for g in range(N_GROUPS)]
    eiota = [sub + g * gsz for g in range(N_GROUPS)]
    picked = [jnp.zeros((gsz, tt), F32) for _ in range(N_GROUPS)]
    top_idx, top_score = [], []
    for _ in range(TOP_K):
        m = functools.reduce(jnp.maximum, [jnp.max(c, axis=0, keepdims=True) for c in cand])
        ei = functools.reduce(jnp.minimum, [
            jnp.min(jnp.where(cand[g] == m, eiota[g], N_EXPERTS), axis=0, keepdims=True)
            for g in range(N_GROUPS)])
        sc = jnp.zeros((1, tt), F32)
        for g in range(N_GROUPS):
            hit = eiota[g] == ei
            picked[g] = jnp.where(hit, 1.0, picked[g])
            cand[g] = jnp.where(hit, neg, cand[g])
            sc = sc + jnp.sum(jnp.where(hit, score_g[g], 0.0), axis=0, keepdims=True)
        top_idx.append(ei)
        top_score.append(sc)

    total = functools.reduce(lambda a, b: a + b, top_score)
    mask = jnp.concatenate(picked, axis=0)
    incl = _dot(mask.astype(BF16), utri_ref[...])
    rank = carry[...] + incl - mask
    for r in range(TOP_K):
        ei = top_idx[r]
        rk = jnp.zeros((1, tt), F32)
        for g in range(N_GROUPS):
            rk = rk + jnp.sum(jnp.where(eiota[g] == ei, rank[g * gsz:(g + 1) * gsz, :], 0.0),
                              axis=0, keepdims=True)
        eidx_ref[r:r + 1, :] = ei
        rank_ref[r:r + 1, :] = rk.astype(jnp.int32)
        w_ref[r:r + 1, :] = top_score[r] / total * ROUTED_SCALE
    new_carry = carry[...] + incl[:, tt - 1:tt]
    carry[...] = new_carry
    cnt_ref[...] = jnp.broadcast_to(new_carry, cnt_ref.shape)


def _router(x, router_w, router_bias):
    n, d = x.shape
    tt = min(ROUTER_TILE, n)
    idx = np.arange(tt)
    utri = jnp.asarray(idx[:, None] <= idx[None, :], BF16)
    eidx, rank, w8, cnt = pl.pallas_call(
        _router_kernel,
        grid=(n // tt,),
        in_specs=[pl.BlockSpec((tt, d), lambda i: (i, 0)),
                  pl.BlockSpec((N_EXPERTS, d), lambda i: (0, 0)),
                  pl.BlockSpec((N_EXPERTS, 1), lambda i: (0, 0)),
                  pl.BlockSpec((tt, tt), lambda i: (0, 0))],
        out_specs=[pl.BlockSpec((TOP_K, tt), lambda i: (0, i)),
                   pl.BlockSpec((TOP_K, tt), lambda i: (0, i)),
                   pl.BlockSpec((TOP_K, tt), lambda i: (0, i)),
                   pl.BlockSpec((N_EXPERTS, 128), lambda i: (0, 0))],
        out_shape=[jax.ShapeDtypeStruct((TOP_K, n), jnp.int32),
                   jax.ShapeDtypeStruct((TOP_K, n), jnp.int32),
                   jax.ShapeDtypeStruct((TOP_K, n), F32),
                   jax.ShapeDtypeStruct((N_EXPERTS, 128), F32)],
        scratch_shapes=[pltpu.VMEM((N_EXPERTS, 1), F32)],
        compiler_params=_params("arbitrary"),
        name="moe_router",
    )(x, router_w.astype(F32).T, router_bias.astype(F32).reshape(N_EXPERTS, 1), utri)
    return eidx, rank, w8, cnt[:, 0].astype(jnp.int32)


def _expert_kernel(te_ref, tv_ref, nt_ref, src_ref, x_hbm, wg_ref, wu_ref, wd_ref, y_ref,
                   xbuf_a, xbuf_b, sem_a, sem_b, wg_bf, wu_bf, wd_bf):
    i = pl.program_id(0)
    n_live = nt_ref[0]
    tm, half = xbuf_a.shape
    bufs = ((xbuf_a, sem_a), (xbuf_b, sem_b))
    prev = jnp.maximum(i - 1, 0)

    def copy(buf, sem, r):
        return pltpu.make_async_copy(x_hbm.at[pl.ds(src_ref[0, 0, r], 1), :], buf.at[pl.ds(r, 1), :], sem)

    def ffn(buf, sem, gather=None):
        pltpu.make_async_copy(buf, buf, sem).wait()
        if gather is not None:
            for r in range(tm):
                copy(*gather, r).start()
        row = lax.broadcasted_iota(jnp.int32, (tm, 1), 0)
        x_lo, x_hi = _unpack_bf16_pair(jnp.where(row < tv_ref[prev], buf[...], 0))
        x_lo, x_hi = x_lo.astype(BF16), x_hi.astype(BF16)
        gate = _dot(x_lo, wg_bf[:half, :]) + _dot(x_hi, wg_bf[half:, :])
        up = _dot(x_lo, wu_bf[:half, :]) + _dot(x_hi, wu_bf[half:, :])
        hidden = (_silu(gate) * up).astype(BF16)
        y = _dot(hidden, wd_bf[...])
        y_ref[...] = _pack_bf16_pair(y[:, :half], y[:, half:])

    @pl.when(i == 0)
    def _():
        def issue(r, carry):
            copy(*bufs[0], r).start()
            return carry
        lax.fori_loop(0, tm, issue, 0)

    computing = jnp.logical_and(i >= 1, i <= n_live)
    new_expert = jnp.logical_or(i == 1, te_ref[prev] != te_ref[jnp.maximum(i - 2, 0)])

    @pl.when(jnp.logical_and(computing, new_expert))
    def _():
        wg_bf[...] = wg_ref[0, 0].astype(BF16)
        wu_bf[...] = wu_ref[0, 0].astype(BF16)
        wd_bf[...] = wd_ref[0, 0].astype(BF16)

    for parity in range(2):
        @pl.when(jnp.logical_and(jnp.logical_and(i >= 1, i < n_live), i % 2 == parity))
        def _(parity=parity):
            ffn(*bufs[1 - parity], gather=bufs[parity])

        @pl.when(jnp.logical_and(i == n_live, (i - 1) % 2 == parity))
        def _(parity=parity):
            ffn(*bufs[parity])


def _expert_ffn(x_packed, src_table, w_gate, w_up, w_down, layer, tile_expert, tile_valid, num_tiles):
    n, half = x_packed.shape
    d = 2 * half
    max_tiles, _, tm = src_table.shape
    e_dim = w_gate.shape[3]
    prev = lambda i, nt: jnp.minimum(jnp.maximum(i - 1, 0), nt[0] - 1)
    w_map = lambda i, te, tv, nt: (layer, te[prev(i, nt)], 0, 0)
    xbuf = pltpu.VMEM((tm, half), jnp.int32)
    return pl.pallas_call(
        _expert_kernel,
        grid_spec=pltpu.PrefetchScalarGridSpec(
            num_scalar_prefetch=3,
            grid=(max_tiles + 1,),
            in_specs=[pl.BlockSpec((1, 1, tm), lambda i, te, tv, nt: (jnp.minimum(i, nt[0] - 1), 0, 0),
                                   memory_space=pltpu.SMEM),
                      pl.BlockSpec(memory_space=pl.ANY),
                      pl.BlockSpec((1, 1, d, e_dim), w_map),
                      pl.BlockSpec((1, 1, d, e_dim), w_map),
                      pl.BlockSpec((1, 1, e_dim, d), w_map)],
            out_specs=pl.BlockSpec((tm, half), lambda i, te, tv, nt: (prev(i, nt), 0)),
            scratch_shapes=[xbuf, xbuf, pltpu.SemaphoreType.DMA(()), pltpu.SemaphoreType.DMA(()),
                            pltpu.VMEM((d, e_dim), BF16), pltpu.VMEM((d, e_dim), BF16),
                            pltpu.VMEM((e_dim, d), BF16)]),
        out_shape=jax.ShapeDtypeStruct((max_tiles * tm, half), jnp.int32),
        compiler_params=_params("arbitrary"),
        name="moe_expert_ffn",
    )(tile_expert, tile_valid, num_tiles, src_table, x_packed, w_gate, w_up, w_down)


def _combine_kernel(dest_ref, x_ref, w8_ref, sg_ref, su_ref, sd_ref, g_ref, b_ref, y_hbm,
                    o_ref, obf_ref, ybuf_a, ybuf_b, sem_a, sem_b):
    tc = x_ref.shape[0]
    i = pl.program_id(0)
    n_tiles = pl.num_programs(0) - 1
    bufs = ((ybuf_a, sem_a), (ybuf_b, sem_b))

    def copy(buf, sem, t, k):
        return pltpu.make_async_copy(y_hbm.at[pl.ds(dest_ref[0, k, t], 1), :],
                                     buf.at[k, pl.ds(t, 1), :], sem)

    def finish(buf, sem, gather=None):
        x = x_ref[...]
        xb = x.astype(BF16)
        hidden = (_silu(_dot(xb, sg_ref[...])) * _dot(xb, su_ref[...])).astype(BF16)
        acc = _dot(hidden, sd_ref[...])
        pltpu.make_async_copy(buf, buf, sem).wait()
        if gather is not None:
            for t in range(tc):
                for k in range(TOP_K):
                    copy(*gather, t, k).start()
        w8 = w8_ref[...]
        half = buf.shape[2]
        routed_lo = jnp.zeros((tc, half), F32)
        routed_hi = jnp.zeros((tc, half), F32)
        for k in range(TOP_K):
            y_lo, y_hi = _unpack_bf16_pair(buf[k])
            routed_lo = routed_lo + w8[:, k:k + 1] * y_lo
            routed_hi = routed_hi + w8[:, k:k + 1] * y_hi
        acc = acc + jnp.concatenate([routed_lo, routed_hi], axis=1)
        y = _layer_norm_rows(DN_ALPHA * x + acc, g_ref[...], b_ref[...])
        o_ref[...] = y
        obf_ref[...] = y.astype(BF16)

    @pl.when(i == 0)
    def _():
        def issue(t, carry):
            for k in range(TOP_K):
                copy(*bufs[0], t, k).start()
            return carry
        lax.fori_loop(0, tc, issue, 0)

    for parity in range(2):
        @pl.when(jnp.logical_and(jnp.logical_and(i >= 1, i < n_tiles), i % 2 == parity))
        def _(parity=parity):
            finish(*bufs[1 - parity], gather=bufs[parity])

        @pl.when(jnp.logical_and(i == n_tiles, (i - 1) % 2 == parity))
        def _(parity=parity):
            finish(*bufs[parity])


def _combine(x, w8_rows, dest_tiles, y_sorted, s_gate, s_up, s_down, g, b):
    n, d = x.shape
    tc = dest_tiles.shape[2]
    sdim = s_gate.shape[1]
    const = lambda shape: pl.BlockSpec(shape, lambda i: (0,) * len(shape))
    n_tiles = n // tc
    prev = lambda i: (jnp.maximum(i - 1, 0), 0)
    row = pl.BlockSpec((tc, d), prev)
    ybuf = pltpu.VMEM((TOP_K, tc, d // 2), jnp.int32)
    return pl.pallas_call(
        _combine_kernel,
        grid=(n_tiles + 1,),
        in_specs=[pl.BlockSpec((1, TOP_K, tc), lambda i: (jnp.minimum(i, n_tiles - 1), 0, 0),
                               memory_space=pltpu.SMEM),
                  row, pl.BlockSpec((tc, TOP_K), prev),
                  const((d, sdim)), const((d, sdim)), const((sdim, d)), const((1, d)), const((1, d)),
                  pl.BlockSpec(memory_space=pl.ANY)],
        out_specs=[row, row],
        out_shape=[jax.ShapeDtypeStruct((n, d), F32), jax.ShapeDtypeStruct((n, d), BF16)],
        scratch_shapes=[ybuf, ybuf, pltpu.SemaphoreType.DMA(()), pltpu.SemaphoreType.DMA(())],
        compiler_params=_params("arbitrary"),
        name="moe_combine",
    )(dest_tiles, x, w8_rows, s_gate, s_up, s_down, g.reshape(1, d), b.reshape(1, d), y_sorted)


def _tile_major(a, tile):
    k, n = a.shape
    return a.reshape(k, n // tile, tile).transpose(1, 0, 2)


def _moe(x, x_packed, router_w, router_bias, w_gate, w_up, w_down, layer, s_gate, s_up, s_down, ln_g, ln_b):
    n, d = x.shape
    tm = EXPERT_TILE
    eidx, rank, w8, cnt = _router(x, router_w, router_bias)
    tiles_e = (cnt + tm - 1) // tm
    tile_end = jnp.cumsum(tiles_e)
    tile_start = tile_end - tiles_e
    row_start = tile_start * tm
    e_ids = jnp.arange(N_EXPERTS, dtype=jnp.int32)
    dest = jnp.sum(jnp.where(eidx[None] == e_ids[:, None, None], row_start[:, None, None], 0), axis=0) + rank
    max_tiles = (n * TOP_K) // tm + N_EXPERTS
    num_tiles = tile_end[-1]
    tile_ids = jnp.arange(max_tiles, dtype=jnp.int32)
    t_clamped = jnp.minimum(tile_ids, num_tiles - 1)
    tile_expert = jnp.sum(t_clamped[:, None] >= tile_end[None, :], axis=1).astype(jnp.int32)
    tile_valid = jnp.clip(cnt[tile_expert] - (t_clamped - tile_start[tile_expert]) * tm, 0, tm).astype(jnp.int32)

    tokens = jnp.tile(jnp.arange(n, dtype=jnp.int32), TOP_K)
    _, tok_sorted = lax.sort_key_val(dest.reshape(-1), tokens)
    tok_sorted = jnp.concatenate([tok_sorted, jnp.zeros((tm,), jnp.int32)])
    cnt_before = jnp.cumsum(cnt) - cnt
    first_slot = cnt_before[tile_expert] + (t_clamped - tile_start[tile_expert]) * tm
    src_table = jax.vmap(lambda s: lax.dynamic_slice(tok_sorted, (s,), (tm,)))(first_slot)
    ys = _expert_ffn(x_packed, src_table.reshape(max_tiles, 1, tm), w_gate, w_up, w_down, layer,
                     tile_expert, tile_valid, num_tiles.reshape(1).astype(jnp.int32))
    return _combine(x, w8.T, _tile_major(dest, min(COMBINE_TILE, n)), ys, s_gate, s_up, s_down, ln_g, ln_b)


def _even_layer(x, x_bf, bsz, seq, w_in, gn_g, w_out, rel_bias, lb, ln_g, ln_b):
    heads = w_out.shape[0] // (2 * HEAD_DIM)
    width = heads * HEAD_DIM
    proj_width = w_in.shape[1]
    col_scale = jnp.concatenate([jnp.full((1, width), HEAD_DIM ** -0.5, F32),
                                 jnp.ones((1, proj_width - width), F32)], axis=1)
    proj = _matmul(x_bf, w_in, col_scale, F32)
    o_a = _dilated_mixture(proj, rel_bias, bsz, seq, heads, proj_width)
    o_b = _hgrn2(proj, lb, gn_g, bsz, seq, heads, proj_width, first_group=3)
    return _proj_residual_ln([o_a, o_b], [w_out[:width], w_out[width:]], x, ln_g, ln_b)


def _odd_layer(x, x_bf, bsz, seq, w_qkv, w_f, b_f, w_out, ln_g, ln_b):
    heads = w_f.shape[1]
    width = heads * HEAD_DIM
    col_scale = jnp.concatenate([jnp.full((1, width), HEAD_DIM ** -0.5, F32),
                                 jnp.ones((1, 2 * width), F32)], axis=1)
    qkv = _matmul(x_bf, w_qkv, col_scale, BF16)
    qx, kx = _fox_gate_terms(x, w_f, b_f, bsz, seq)
    o = _fox_attention(qkv, qx, kx, bsz, seq, heads)
    return _proj_residual_ln([o], [w_out], x, ln_g, ln_b)


def kernel(x, rel_bias, hgrn_lb_logits, a_w_in, a_gn_g, a_w_out, c_w_in, c_b_f, c_w_out, ln_mix_g, ln_mix_b, ln_ffn_g, ln_ffn_b, router_w, router_bias, exp_w_gate, exp_w_up, exp_w_down, sh_w_gate, sh_w_up, sh_w_down):
    bsz, seq, d = x.shape
    depth = ln_mix_g.shape[0]
    lb_all = jnp.cumsum(jax.nn.softmax(hgrn_lb_logits.astype(F32), axis=0), axis=0)
    h = x.reshape(bsz * seq, d).astype(F32)
    h_bf = h.astype(BF16)
    for layer in range(depth):
        j = layer // 2
        if layer % 2 == 0:
            h, h_packed = _even_layer(h, h_bf, bsz, seq, a_w_in[j].astype(BF16), a_gn_g[j],
                                  a_w_out[j].astype(BF16), rel_bias, lb_all[layer],
                                  ln_mix_g[layer], ln_mix_b[layer])
        else:
            c_width = c_w_out.shape[1]
            w_in = c_w_in[j]
            h, h_packed = _odd_layer(h, h_bf, bsz, seq, w_in[:, :3 * c_width].astype(BF16),
                                 w_in[:, 3 * c_width:], c_b_f[j], c_w_out[j].astype(BF16),
                                 ln_mix_g[layer], ln_mix_b[layer])
        h, h_bf = _moe(h, h_packed, router_w[layer], router_bias[layer], exp_w_gate, exp_w_up, exp_w_down, layer,
                       sh_w_gate[layer].astype(BF16), sh_w_up[layer].astype(BF16),
                       sh_w_down[layer].astype(BF16), ln_ffn_g[layer], ln_ffn_b[layer])
    return h.reshape(bsz, seq, d)
```

```python
import functools

import numpy as np
import jax
import jax.numpy as jnp
from jax import lax
from jax.experimental import pallas as pl
from jax.experimental.pallas import tpu as pltpu

F32 = jnp.float32
BF16 = jnp.bfloat16

HEAD_DIM = 128
A_PATTERNS = ((128, 1), (512, 4), (2048, 16))
A_BLOCK = 128
REL_BUCKETS = 32
REL_MAX_DIST = 2048
N_EXPERTS = 64
TOP_K = 8
N_GROUPS = 8
GROUP_SIZE = N_EXPERTS // N_GROUPS
TOPK_GROUPS = 4
ROUTED_SCALE = 2.5
LN_EPS = 1e-5
RMS_EPS = 1e-6
DEPTH = 2
DN_ALPHA = (2 * DEPTH) ** 0.25
NEG_INF = -1e30
LOG2_E = 1.4426950408889634

DILATED_UNROLL = 8
HGRN_CHUNK = 128
HGRN_BCAST_MIN = 8
FOX_Q_TILE = 512
FOX_K_TILE = 256
ROUTER_TILE = 512
EXPERT_TILE = 512
COMBINE_TILE = 128
VMEM_LIMIT = 56 * 1024 * 1024

_NT = (((1,), (1,)), ((), ()))
_TN = (((0,), (0,)), ((), ()))


def _dot(a, b):
    return jnp.dot(a, b, preferred_element_type=F32)


def _dot_nt(a, b):
    return lax.dot_general(a, b, _NT, preferred_element_type=F32)


def _dot_tn(a, b):
    return lax.dot_general(a, b, _TN, preferred_element_type=F32)


def _sigmoid(x):
    return 1.0 / (1.0 + jnp.exp(-x))


def _silu(x):
    return x * _sigmoid(x)


def _split3(x, axis):
    hi = x.astype(BF16)
    r1 = x - hi.astype(F32)
    mid = r1.astype(BF16)
    lo = (r1 - mid.astype(F32)).astype(BF16)
    return jnp.concatenate([hi, mid, lo], axis=axis)


def _layer_norm_rows(y, g, b):
    mu = jnp.mean(y, axis=-1, keepdims=True)
    yc = y - mu
    var = jnp.mean(yc * yc, axis=-1, keepdims=True)
    return yc * lax.rsqrt(var + LN_EPS) * g + b


def _params(*sem):
    return pltpu.CompilerParams(dimension_semantics=sem, vmem_limit_bytes=VMEM_LIMIT)


def _mm_kernel(x_ref, w_ref, cs_ref, o_ref):
    o_ref[...] = (_dot(x_ref[...], w_ref[...]) * cs_ref[...]).astype(o_ref.dtype)


def _matmul(x, w, col_scale, out_dtype, tm=1024, tn=1024):
    m, k = x.shape
    nw = w.shape[1]
    tm, tn = min(tm, m), min(tn, nw)
    assert m % tm == 0 and nw % tn == 0
    return pl.pallas_call(
        _mm_kernel,
        grid=(nw // tn, m // tm),
        in_specs=[pl.BlockSpec((tm, k), lambda j, i: (i, 0)),
                  pl.BlockSpec((k, tn), lambda j, i: (0, j)),
                  pl.BlockSpec((1, tn), lambda j, i: (0, j))],
        out_specs=pl.BlockSpec((tm, tn), lambda j, i: (i, j)),
        out_shape=jax.ShapeDtypeStruct((m, nw), out_dtype),
        compiler_params=_params("parallel", "parallel"),
        name="proj_matmul",
    )(x, w, col_scale)


def _proj_ln_kernel(*refs, n_in):
    a_refs, w_refs = refs[:n_in], refs[n_in:2 * n_in]
    r_ref, g_ref, b_ref, o_ref, obf_ref = refs[2 * n_in:]
    acc = _dot(a_refs[0][...], w_refs[0][...])
    for a_ref, w_ref in zip(a_refs[1:], w_refs[1:]):
        acc = acc + _dot(a_ref[...], w_ref[...])
    y = _layer_norm_rows(DN_ALPHA * r_ref[...] + acc, g_ref[...], b_ref[...])
    o_ref[...] = y
    half = y.shape[1] // 2
    obf_ref[...] = _pack_bf16_pair(y[:, :half], y[:, half:])


def _pack_bf16_pair(lo, hi):
    lo_bits = lax.shift_right_logical(lax.bitcast_convert_type(lo.astype(BF16).astype(F32), jnp.int32), 16)
    hi_bits = lax.bitcast_convert_type(hi.astype(BF16).astype(F32), jnp.int32) & jnp.int32(-65536)
    return lo_bits | hi_bits


def _unpack_bf16_pair(words):
    lo = lax.bitcast_convert_type(lax.shift_left(words, 16), F32)
    hi = lax.bitcast_convert_type(words & jnp.int32(-65536), F32)
    return lo, hi


def _proj_residual_ln(acts, weights, resid, g, b, tm=256):
    m, d = resid.shape
    tm = min(tm, m)
    n_in = len(acts)
    in_specs = [pl.BlockSpec((tm, a.shape[1]), lambda i: (i, 0)) for a in acts]
    in_specs += [pl.BlockSpec(w.shape, lambda i: (0, 0)) for w in weights]
    in_specs += [pl.BlockSpec((tm, d), lambda i: (i, 0)),
                 pl.BlockSpec((1, d), lambda i: (0, 0)),
                 pl.BlockSpec((1, d), lambda i: (0, 0))]
    return pl.pallas_call(
        functools.partial(_proj_ln_kernel, n_in=n_in),
        grid=(m // tm,),
        in_specs=in_specs,
        out_specs=[pl.BlockSpec((tm, d), lambda i: (i, 0)), pl.BlockSpec((tm, d // 2), lambda i: (i, 0))],
        out_shape=[jax.ShapeDtypeStruct((m, d), F32), jax.ShapeDtypeStruct((m, d // 2), jnp.int32)],
        compiler_params=_params("parallel"),
        name="out_proj_ln",
    )(*acts, *weights, resid, g.reshape(1, d), b.reshape(1, d))


def _t5_bucket(dist):
    max_exact = REL_BUCKETS // 2
    d = np.maximum(dist, 1).astype(np.float64)
    large = max_exact + (np.log(d / max_exact) / np.log(REL_MAX_DIST / max_exact)
                         * (REL_BUCKETS - max_exact)).astype(np.int64)
    large = np.minimum(large, REL_BUCKETS - 1)
    return np.where(dist < max_exact, dist, large).astype(np.int32)


def _dilated_kernel(q_ref, k_ref, v_ref, bucket_ref, rel_bias_ref, out_ref, o_scr, lse_scr):
    blk = A_BLOCK
    seq = q_ref.shape[1]
    head = pl.program_id(1)
    qi = lax.broadcasted_iota(jnp.int32, (blk, 2 * blk), 0)
    ki = lax.broadcasted_iota(jnp.int32, (blk, 2 * blk), 1)
    rel = qi + blk - ki

    for bi, (window, dil) in enumerate(A_PATTERNS):
        band = window // dil
        n_blocks = seq // (dil * blk)
        valid = jnp.abs(2 * rel - band) <= band
        bucket = bucket_ref[bi]
        bias = jnp.zeros((blk, 2 * blk), F32)
        for b in range(REL_BUCKETS):
            bias = jnp.where(bucket == b, rel_bias_ref[head, b], bias)

        def attend(q_start, k_start, n_keys, bias_blk, valid_blk, bi=bi, dil=dil):
            q_rows = pl.ds(q_start, blk, stride=dil)
            k_rows = pl.ds(k_start, n_keys, stride=dil)
            q = q_ref[0, q_rows, :].astype(BF16)
            kk = k_ref[0, k_rows, :].astype(BF16)
            vv = v_ref[0, k_rows, :].astype(BF16)
            s = _dot_nt(q, kk) + bias_blk
            s = jnp.where(valid_blk, s, NEG_INF)
            m = jnp.max(s, axis=-1, keepdims=True)
            p = jnp.exp(s - m)
            l = jnp.sum(p, axis=-1, keepdims=True)
            o_scr[bi, q_rows, :] = _dot(p.astype(BF16), vv) / l
            lse_scr[bi, q_rows, :] = jnp.broadcast_to(m + jnp.log(l), (blk, HEAD_DIM))

        def first(r, carry, attend=attend, bias=bias, valid=valid):
            attend(r, r, blk, bias[:, blk:], valid[:, blk:])
            return carry

        lax.fori_loop(0, dil, first, 0, unroll=min(dil, DILATED_UNROLL))

        def rest(t, carry, attend=attend, bias=bias, valid=valid, dil=dil):
            r = t & (dil - 1)
            n = (t >> (dil.bit_length() - 1)) + 1
            attend(r + n * blk * dil, r + (n - 1) * blk * dil, 2 * blk, bias, valid)
            return carry

        lax.fori_loop(0, dil * (n_blocks - 1), rest, 0, unroll=DILATED_UNROLL)

    chunk = 2 * blk
    nb = len(A_PATTERNS)

    def mix(c, carry):
        rows = pl.ds(pl.multiple_of(c * chunk, chunk), chunk)
        lses = [lse_scr[b, rows, :] for b in range(nb)]
        m = functools.reduce(jnp.maximum, lses)
        es = [jnp.exp(x - m) for x in lses]
        den = functools.reduce(lambda a, b: a + b, es)
        num = functools.reduce(lambda a, b: a + b, [e * o_scr[b, rows, :] for b, e in enumerate(es)])
        out_ref[0, rows, :] = (num / den).astype(out_ref.dtype)
        return carry

    lax.fori_loop(0, seq // chunk, mix, 0)


def _dilated_bucket_tables():
    qi = np.arange(A_BLOCK)[:, None]
    ki = np.arange(2 * A_BLOCK)[None, :]
    rel = qi + A_BLOCK - ki
    return np.stack([_t5_bucket(np.clip(rel, 0, None) * dilation) for _, dilation in A_PATTERNS], axis=0)


def _dilated_mixture(proj, rel_bias, bsz, seq, heads, proj_width):
    for window, dilation in A_PATTERNS:
        assert seq % (dilation * A_BLOCK) == 0 and window // dilation <= A_BLOCK
    nb = len(A_PATTERNS)
    view = proj.reshape(bsz, seq, proj_width)
    buckets = jnp.asarray(_dilated_bucket_tables())

    def in_spec(group):
        return pl.BlockSpec((1, seq, HEAD_DIM), lambda b, h, g=group: (b, 0, g * heads + h))

    o = pl.pallas_call(
        _dilated_kernel,
        grid=(bsz, heads),
        in_specs=[in_spec(0), in_spec(1), in_spec(2),
                  pl.BlockSpec(buckets.shape, lambda b, h: (0, 0, 0)),
                  pl.BlockSpec(memory_space=pltpu.SMEM)],
        out_specs=pl.BlockSpec((1, seq, HEAD_DIM), lambda b, h: (b, 0, h)),
        out_shape=jax.ShapeDtypeStruct((bsz, seq, heads * HEAD_DIM), BF16),
        scratch_shapes=[pltpu.VMEM((nb, seq, HEAD_DIM), F32), pltpu.VMEM((nb, seq, HEAD_DIM), F32)],
        compiler_params=_params("parallel", "parallel"),
        name="dilated_attention",
    )(view, view, view, buckets, rel_bias.astype(F32).T)
    return o.reshape(bsz * seq, heads * HEAD_DIM)


def _hgrn_levels(t):
    cs = []
    c = t // 2
    while c >= 1:
        cs.append(c)
        c //= 2
    return cs


def _hgrn_consts(t):
    idx = np.arange(t)
    ltri = (idx[:, None] >= idx[None, :]).astype(np.float32)
    sels = []
    for c in _hgrn_levels(t):
        if c < HGRN_BCAST_MIN:
            ref = (idx // (2 * c)) * (2 * c) + c - 1
            sels.append((idx[None, :] == ref[:, None]).astype(np.float32))
    return jnp.asarray(ltri, BF16), jnp.asarray(np.concatenate(sels, axis=0), BF16)


def _hgrn_kernel(q_ref, f_ref, i_ref, g_ref, lb_ref, gn_ref, ltri_ref, sel_ref, o_ref, state_ref,
                 *, n_chunks):
    t = HGRN_CHUNK
    dk = HEAD_DIM
    levels = _hgrn_levels(t)
    lb = lb_ref[0]
    gn = gn_ref[0]
    ltri = ltri_ref[...]
    sel = sel_ref[...]
    row = lax.broadcasted_iota(jnp.int32, (t, dk), 0)
    ti = lax.broadcasted_iota(jnp.int32, (t, t), 0)
    si = lax.broadcasted_iota(jnp.int32, (t, t), 1)
    state_ref[...] = jnp.zeros_like(state_ref)

    def sum3(y):
        return y[:, :dk] + y[:, dk:2 * dk] + y[:, 2 * dk:]

    def chunk(n, carry):
        st = pl.multiple_of(n * t, t)
        fq = _silu(q_ref[0, pl.ds(st, t), :])
        forget = lb + (1.0 - lb) * _sigmoid(f_ref[0, pl.ds(st, t), :])
        logf = jnp.log(forget)
        key = 1.0 - forget
        val = i_ref[0, pl.ds(st, t), :].astype(BF16)
        a = sum3(_dot(ltri, _split3(logf, 1)))
        aref_fine = sum3(_dot(sel, _split3(a, 1)))
        fq_b = fq.astype(BF16)
        key_b = key.astype(BF16)
        scores = jnp.where(ti == si, _dot_nt(fq_b, key_b), 0.0)
        n_fine = 0
        for c in levels:
            if c >= HGRN_BCAST_MIN:
                aref = jnp.concatenate(
                    [jnp.broadcast_to(a[b0 + c - 1:b0 + c, :], (2 * c, dk)) for b0 in range(0, t, 2 * c)],
                    axis=0)
            else:
                aref = aref_fine[n_fine * t:(n_fine + 1) * t, :]
                n_fine += 1
            upper = (row & (2 * c - 1)) >= c
            decayed = jnp.where(upper, fq, key) * jnp.exp(-jnp.abs(a - aref))
            ql = jnp.where(upper, decayed, 0.0).astype(BF16)
            kl = jnp.where(upper, 0.0, decayed).astype(BF16)
            s_l = _dot_nt(ql, kl)
            if 2 * c < t:
                sh = (2 * c).bit_length() - 1
                s_l = jnp.where((ti >> sh) == (si >> sh), s_l, 0.0)
            scores = scores + s_l
        state_t = state_ref[...]
        inter = _dot_nt((fq * jnp.exp(a)).astype(BF16), state_t.astype(BF16))
        o = inter + _dot(scores.astype(BF16), val)
        a_last = a[t - 1:t, :]
        k_dec = (key * jnp.exp(a_last - a)).astype(BF16)
        state_ref[...] = state_t * jnp.exp(a_last) + _dot_tn(val, k_dec)
        o = o * lax.rsqrt(jnp.mean(o * o, axis=-1, keepdims=True) + RMS_EPS) * gn
        o = o * _silu(g_ref[0, pl.ds(st, t), :])
        o_ref[0, pl.ds(st, t), :] = o.astype(o_ref.dtype)
        return carry

    lax.fori_loop(0, n_chunks, chunk, 0, unroll=2)


def _hgrn2(proj, lb, gn_g, bsz, seq, heads, proj_width, first_group):
    assert seq % HGRN_CHUNK == 0
    view = proj.reshape(bsz, seq, proj_width)
    ltri, sel = _hgrn_consts(HGRN_CHUNK)

    def in_spec(group):
        return pl.BlockSpec((1, seq, HEAD_DIM), lambda b, h, g=group: (b, 0, (first_group + g) * heads + h))

    head_vec = pl.BlockSpec((1, 1, HEAD_DIM), lambda b, h: (h, 0, 0))
    o = pl.pallas_call(
        functools.partial(_hgrn_kernel, n_chunks=seq // HGRN_CHUNK),
        grid=(bsz, heads),
        in_specs=[in_spec(0), in_spec(1), in_spec(2), in_spec(3), head_vec, head_vec,
                  pl.BlockSpec(ltri.shape, lambda b, h: (0, 0)),
                  pl.BlockSpec(sel.shape, lambda b, h: (0, 0))],
        out_specs=pl.BlockSpec((1, seq, HEAD_DIM), lambda b, h: (b, 0, h)),
        out_shape=jax.ShapeDtypeStruct((bsz, seq, heads * HEAD_DIM), BF16),
        scratch_shapes=[pltpu.VMEM((HEAD_DIM, HEAD_DIM), F32)],
        compiler_params=_params("parallel", "parallel"),
        name="hgrn2",
    )(view, view, view, view, lb.reshape(heads, 1, HEAD_DIM).astype(F32),
      gn_g.reshape(heads, 1, HEAD_DIM).astype(F32), ltri, sel)
    return o.reshape(bsz * seq, heads * HEAD_DIM)


def _fox_gate_kernel(x_ref, w_ref, b_ref, ltri_ref, qx_ref, kx_ref, carry):
    heads = w_ref.shape[1]
    ts = x_ref.shape[1]

    @pl.when(pl.program_id(1) == 0)
    def _():
        carry[...] = jnp.zeros_like(carry)

    def log_sigmoid(z):
        return jnp.minimum(z, 0.0) - jnp.log(1.0 + jnp.exp(-jnp.abs(z)))

    x = x_ref[0]
    xh = x.astype(BF16)
    xl = (x - xh.astype(F32)).astype(BF16)
    w = w_ref[...]
    wh = w.astype(BF16)
    wl = (w - wh.astype(F32)).astype(BF16)
    f = _dot(xh, wh) + _dot(xh, wl) + _dot(xl, wh) + b_ref[...]
    c = _dot(ltri_ref[...], _split3(log_sigmoid(f), 1))
    c = c[:, :heads] + c[:, heads:2 * heads] + c[:, 2 * heads:] + carry[...]
    carry[...] = c[ts - 1:ts, :]
    lane = lax.broadcasted_iota(jnp.int32, (ts, HEAD_DIM), 1)
    for h in range(heads):
        ch = c[:, h:h + 1] * LOG2_E
        hi = ch.astype(BF16).astype(F32)
        r1 = ch - hi
        mid = r1.astype(BF16).astype(F32)
        lo = r1 - mid
        qx = jnp.where(lane == 0, hi, jnp.where(lane == 1, mid, jnp.where(lane == 2, lo,
                       jnp.where(lane < 6, 1.0, 0.0))))
        kx = jnp.where(lane < 3, 1.0, jnp.where(lane == 3, -hi, jnp.where(lane == 4, -mid,
                       jnp.where(lane == 5, -lo, 0.0))))
        qx_ref[0, :, h * HEAD_DIM:(h + 1) * HEAD_DIM] = qx.astype(BF16)
        kx_ref[0, :, h * HEAD_DIM:(h + 1) * HEAD_DIM] = kx.astype(BF16)


def _fox_gate_terms(x, w_f, b_f, bsz, seq, ts=512):
    d = x.shape[-1]
    heads = w_f.shape[1]
    ts = min(ts, seq)
    idx = np.arange(ts)
    ltri = jnp.asarray(idx[:, None] >= idx[None, :], BF16)
    const = lambda shape: pl.BlockSpec(shape, lambda b, i: (0,) * len(shape))
    out_spec = pl.BlockSpec((1, ts, heads * HEAD_DIM), lambda b, i: (b, i, 0))
    out_sds = jax.ShapeDtypeStruct((bsz, seq, heads * HEAD_DIM), BF16)
    return pl.pallas_call(
        _fox_gate_kernel,
        grid=(bsz, seq // ts),
        in_specs=[pl.BlockSpec((1, ts, d), lambda b, i: (b, i, 0)),
                  const((d, heads)), const((1, heads)), const((ts, ts))],
        out_specs=[out_spec, out_spec],
        out_shape=[out_sds, out_sds],
        scratch_shapes=[pltpu.VMEM((1, heads), F32)],
        compiler_params=_params("parallel", "arbitrary"),
        name="fox_gate_cumsum",
    )(x.reshape(bsz, seq, d), w_f.astype(F32), b_f.astype(F32).reshape(1, heads), ltri)


def _fox_kernel(q_ref, qx_ref, k_ref, kx_ref, vt_ref, o_ref, s_a, s_b, p_a, p_b, acc_ref):
    tq, tk = FOX_Q_TILE, FOX_K_TILE
    lanes = HEAD_DIM
    i = pl.program_id(2)
    q = jnp.concatenate([q_ref[0], qx_ref[0]], axis=1)

    def logits_into(j, s_ref):
        st = pl.multiple_of(j * tk, tk)
        kk = jnp.concatenate([k_ref[0, pl.ds(st, tk), :], kx_ref[0, pl.ds(st, tk), :]], axis=1)
        s_ref[...] = _dot_nt(kk, q)

    def softmax_pv(j, s_ref, p_ref, m, l, diagonal):
        m_out, l_out, alphas = [], [], []
        for c in range(tq // lanes):
            cols = slice(c * lanes, (c + 1) * lanes)
            s = s_ref[:, cols]
            if diagonal:
                key = lax.broadcasted_iota(jnp.int32, (tk, lanes), 0) + j * tk
                qry = lax.broadcasted_iota(jnp.int32, (tk, lanes), 1) + (i * tq + c * lanes)
                s = jnp.where(key <= qry, s, NEG_INF)
            m_new = jnp.maximum(m[:, cols], jnp.max(s, axis=0, keepdims=True))
            alpha = jnp.exp2(m[:, cols] - m_new)
            p = jnp.exp2(s - m_new)
            l_out.append(alpha * l[:, cols] + jnp.sum(p, axis=0, keepdims=True))
            m_out.append(m_new)
            alphas.append(alpha)
            p_ref[:, cols] = p.astype(BF16)
        vt = vt_ref[0, :, pl.ds(pl.multiple_of(j * tk, tk), tk)]
        acc_ref[...] = jnp.concatenate(alphas, axis=1) * acc_ref[...] + _dot(vt, p_ref[...])
        return jnp.concatenate(m_out, axis=1), jnp.concatenate(l_out, axis=1)

    def pair(jp, carry):
        m, l = carry
        t0 = 2 * jp
        logits_into(t0 + 1, s_b)
        m, l = softmax_pv(t0, s_a, p_a, m, l, False)
        logits_into(t0 + 2, s_a)
        return softmax_pv(t0 + 1, s_b, p_b, m, l, False)

    assert tq == 2 * tk
    acc_ref[...] = jnp.zeros_like(acc_ref)
    logits_into(0, s_a)
    m, l = lax.fori_loop(0, i, pair, (jnp.full((1, tq), NEG_INF, F32), jnp.zeros((1, tq), F32)))
    t0 = 2 * i
    logits_into(t0 + 1, s_b)
    m, l = softmax_pv(t0, s_a, p_a, m, l, True)
    m, l = softmax_pv(t0 + 1, s_b, p_b, m, l, True)
    o_ref[0] = jnp.transpose(acc_ref[...] / l).astype(o_ref.dtype)


def _fox_attention(qkv, qx, kx, bsz, seq, heads):
    tq = FOX_Q_TILE
    assert seq % tq == 0 and tq % FOX_K_TILE == 0
    width = heads * HEAD_DIM
    view = qkv.reshape(bsz, seq, 3 * width)
    v_t = jnp.swapaxes(view[:, :, 2 * width:], 1, 2)
    q_spec = lambda off: pl.BlockSpec((1, tq, HEAD_DIM), lambda b, h, i: (b, i, off + h))
    kv_spec = lambda off: pl.BlockSpec((1, seq, HEAD_DIM), lambda b, h, i: (b, 0, off + h))
    o = pl.pallas_call(
        _fox_kernel,
        grid=(bsz, heads, seq // tq),
        in_specs=[q_spec(0), q_spec(0), kv_spec(heads), kv_spec(0),
                  pl.BlockSpec((1, HEAD_DIM, seq), lambda b, h, i: (b, h, 0))],
        out_specs=pl.BlockSpec((1, tq, HEAD_DIM), lambda b, h, i: (b, i, h)),
        out_shape=jax.ShapeDtypeStruct((bsz, seq, width), BF16),
        scratch_shapes=[pltpu.VMEM((FOX_K_TILE, tq), F32), pltpu.VMEM((FOX_K_TILE, tq), F32),
                        pltpu.VMEM((FOX_K_TILE, tq), BF16), pltpu.VMEM((FOX_K_TILE, tq), BF16),
                        pltpu.VMEM((HEAD_DIM, tq), F32)],
        compiler_params=_params("parallel", "parallel", "parallel"),
        name="fox_attention",
    )(view, qx, view, kx, v_t)
    return o.reshape(bsz * seq, width)


def _router_kernel(x_ref, wt_ref, bias_ref, utri_ref, eidx_ref, rank_ref, w_ref, cnt_ref, carry):
    gsz = GROUP_SIZE
    tt = x_ref.shape[0]

    @pl.when(pl.program_id(0) == 0)
    def _():
        carry[...] = jnp.zeros_like(carry)

    x = x_ref[...]
    xh = x.astype(BF16)
    xl = (x - xh.astype(F32)).astype(BF16)
    wt = wt_ref[...]
    wh = wt.astype(BF16)
    wl = (wt - wh.astype(F32)).astype(BF16)
    logits = _dot_nt(wh, xh) + _dot_nt(wh, xl) + _dot_nt(wl, xh)
    scores = _sigmoid(logits)
    sel = scores + bias_ref[...]
    sub = lax.broadcasted_iota(jnp.int32, (gsz, tt), 0)
    neg = -jnp.inf

    group_score = []
    for g in range(N_GROUPS):
        blk = sel[g * gsz:(g + 1) * gsz, :]
        m1 = jnp.max(blk, axis=0, keepdims=True)
        first = jnp.min(jnp.where(blk == m1, sub, gsz), axis=0, keepdims=True)
        m2 = jnp.max(jnp.where(sub == first, neg, blk), axis=0, keepdims=True)
        group_score.append(m1 + m2)

    chosen = [jnp.zeros((1, tt), F32) for _ in range(N_GROUPS)]
    for _ in range(TOPK_GROUPS):
        m = functools.reduce(jnp.maximum, group_score)
        gi = jnp.full((1, tt), N_GROUPS, jnp.int32)
        for g in reversed(range(N_GROUPS)):
            gi = jnp.where(group_score[g] == m, g, gi)
        for g in range(N_GROUPS):
            hit = gi == g
            chosen[g] = jnp.where(hit, 1.0, chosen[g])
            group_score[g] = jnp.where(hit, neg, group_score[g])

    cand = [jnp.where(jnp.broadcast_to(chosen[g], (gsz, tt)) > 0.5, sel[g * gsz:(g + 1) * gsz, :], NEG_INF)
            for g in range(N_GROUPS)]
    score_g = [scores[g * gsz:(g + 1) * gsz, :] for g in range(N_GROUPS)]
    eiota = [sub + g * gsz for g in range(N_GROUPS)]
    picked = [jnp.zeros((gsz, tt), F32) for _ in range(N_GROUPS)]
    top_idx, top_score = [], []
    for _ in range(TOP_K):
        m = functools.reduce(jnp.maximum, [jnp.max(c, axis=0, keepdims=True) for c in cand])
        ei = functools.reduce(jnp.minimum, [
            jnp.min(jnp.where(cand[g] == m, eiota[g], N_EXPERTS), axis=0, keepdims=True)
            for g in range(N_GROUPS)])
        sc = jnp.zeros((1, tt), F32)
        for g in range(N_GROUPS):
            hit = eiota[g] == ei
            picked[g] = jnp.where(hit, 1.0, picked[g])
            cand[g] = jnp.where(hit, neg, cand[g])
            sc = sc + jnp.sum(jnp.where(hit, score_g[g], 0.0), axis=0, keepdims=True)
        top_idx.append(ei)
        top_score.append(sc)

    total = functools.reduce(lambda a, b: a + b, top_score)
    mask = jnp.concatenate(picked, axis=0)
    incl = _dot(mask.astype(BF16), utri_ref[...])
    rank = carry[...] + incl - mask
    for r in range(TOP_K):
        ei = top_idx[r]
        rk = jnp.zeros((1, tt), F32)
        for g in range(N_GROUPS):
            rk = rk + jnp.sum(jnp.where(eiota[g] == ei, rank[g * gsz:(g + 1) * gsz, :], 0.0),
                              axis=0, keepdims=True)
        eidx_ref[r:r + 1, :] = ei
        rank_ref[r:r + 1, :] = rk.astype(jnp.int32)
        w_ref[r:r + 1, :] = top_score[r] / total * ROUTED_SCALE
    new_carry = carry[...] + incl[:, tt - 1:tt]
    carry[...] = new_carry
    cnt_ref[...] = jnp.broadcast_to(new_carry, cnt_ref.shape)


def _router(x, router_w, router_bias):
    n, d = x.shape
    tt = min(ROUTER_TILE, n)
    idx = np.arange(tt)
    utri = jnp.asarray(idx[:, None] <= idx[None, :], BF16)
    eidx, rank, w8, cnt = pl.pallas_call(
        _router_kernel,
        grid=(n // tt,),
        in_specs=[pl.BlockSpec((tt, d), lambda i: (i, 0)),
                  pl.BlockSpec((N_EXPERTS, d), lambda i: (0, 0)),
                  pl.BlockSpec((N_EXPERTS, 1), lambda i: (0, 0)),
                  pl.BlockSpec((tt, tt), lambda i: (0, 0))],
        out_specs=[pl.BlockSpec((TOP_K, tt), lambda i: (0, i)),
                   pl.BlockSpec((TOP_K, tt), lambda i: (0, i)),
                   pl.BlockSpec((TOP_K, tt), lambda i: (0, i)),
                   pl.BlockSpec((N_EXPERTS, 128), lambda i: (0, 0))],
        out_shape=[jax.ShapeDtypeStruct((TOP_K, n), jnp.int32),
                   jax.ShapeDtypeStruct((TOP_K, n), jnp.int32),
                   jax.ShapeDtypeStruct((TOP_K, n), F32),
                   jax.ShapeDtypeStruct((N_EXPERTS, 128), F32)],
        scratch_shapes=[pltpu.VMEM((N_EXPERTS, 1), F32)],
        compiler_params=_params("arbitrary"),
        name="moe_router",
    )(x, router_w.astype(F32).T, router_bias.astype(F32).reshape(N_EXPERTS, 1), utri)
    return eidx, rank, w8, cnt[:, 0].astype(jnp.int32)


def _expert_kernel(te_ref, tv_ref, nt_ref, src_ref, x_hbm, wg_ref, wu_ref, wd_ref, y_ref,
                   xbuf_a, xbuf_b, sem_a, sem_b, wg_bf, wu_bf, wd_bf):
    i = pl.program_id(0)
    n_live = nt_ref[0]
    tm, half = xbuf_a.shape
    bufs = ((xbuf_a, sem_a), (xbuf_b, sem_b))
    prev = jnp.maximum(i - 1, 0)

    def copy(buf, sem, r):
        return pltpu.make_async_copy(x_hbm.at[pl.ds(src_ref[0, 0, r], 1), :], buf.at[pl.ds(r, 1), :], sem)

    def ffn(buf, sem, gather=None):
        pltpu.make_async_copy(buf, buf, sem).wait()
        if gather is not None:
            for r in range(tm):
                copy(*gather, r).start()
        row = lax.broadcasted_iota(jnp.int32, (tm, 1), 0)
        x_lo, x_hi = _unpack_bf16_pair(jnp.where(row < tv_ref[prev], buf[...], 0))
        x_lo, x_hi = x_lo.astype(BF16), x_hi.astype(BF16)
        gate = _dot(x_lo, wg_bf[:half, :]) + _dot(x_hi, wg_bf[half:, :])
        up = _dot(x_lo, wu_bf[:half, :]) + _dot(x_hi, wu_bf[half:, :])
        hidden = (_silu(gate) * up).astype(BF16)
        y = _dot(hidden, wd_bf[...])
        y_ref[...] = _pack_bf16_pair(y[:, :half], y[:, half:])

    @pl.when(i == 0)
    def _():
        def issue(r, carry):
            copy(*bufs[0], r).start()
            return carry
        lax.fori_loop(0, tm, issue, 0)

    computing = jnp.logical_and(i >= 1, i <= n_live)
    new_expert = jnp.logical_or(i == 1, te_ref[prev] != te_ref[jnp.maximum(i - 2, 0)])

    @pl.when(jnp.logical_and(computing, new_expert))
    def _():
        wg_bf[...] = wg_ref[0, 0].astype(BF16)
        wu_bf[...] = wu_ref[0, 0].astype(BF16)
        wd_bf[...] = wd_ref[0, 0].astype(BF16)

    for parity in range(2):
        @pl.when(jnp.logical_and(jnp.logical_and(i >= 1, i < n_live), i % 2 == parity))
        def _(parity=parity):
            ffn(*bufs[1 - parity], gather=bufs[parity])

        @pl.when(jnp.logical_and(i == n_live, (i - 1) % 2 == parity))
        def _(parity=parity):
            ffn(*bufs[parity])


def _expert_ffn(x_packed, src_table, w_gate, w_up, w_down, layer, tile_expert, tile_valid, num_tiles):
    n, half = x_packed.shape
    d = 2 * half
    max_tiles, _, tm = src_table.shape
    e_dim = w_gate.shape[3]
    prev = lambda i, nt: jnp.minimum(jnp.maximum(i - 1, 0), nt[0] - 1)
    w_map = lambda i, te, tv, nt: (layer, te[prev(i, nt)], 0, 0)
    xbuf = pltpu.VMEM((tm, half), jnp.int32)
    return pl.pallas_call(
        _expert_kernel,
        grid_spec=pltpu.PrefetchScalarGridSpec(
            num_scalar_prefetch=3,
            grid=(max_tiles + 1,),
            in_specs=[pl.BlockSpec((1, 1, tm), lambda i, te, tv, nt: (jnp.minimum(i, nt[0] - 1), 0, 0),
                                   memory_space=pltpu.SMEM),
                      pl.BlockSpec(memory_space=pl.ANY),
                      pl.BlockSpec((1, 1, d, e_dim), w_map),
                      pl.BlockSpec((1, 1, d, e_dim), w_map),
                      pl.BlockSpec((1, 1, e_dim, d), w_map)],
            out_specs=pl.BlockSpec((tm, half), lambda i, te, tv, nt: (prev(i, nt), 0)),
            scratch_shapes=[xbuf, xbuf, pltpu.SemaphoreType.DMA(()), pltpu.SemaphoreType.DMA(()),
                            pltpu.VMEM((d, e_dim), BF16), pltpu.VMEM((d, e_dim), BF16),
                            pltpu.VMEM((e_dim, d), BF16)]),
        out_shape=jax.ShapeDtypeStruct((max_tiles * tm, half), jnp.int32),
        compiler_params=_params("arbitrary"),
        name="moe_expert_ffn",
    )(tile_expert, tile_valid, num_tiles, src_table, x_packed, w_gate, w_up, w_down)


def _combine_kernel(dest_ref, x_ref, w8_ref, sg_ref, su_ref, sd_ref, g_ref, b_ref, y_hbm,
                    o_ref, obf_ref, ybuf_a, ybuf_b, sem_a, sem_b):
    tc = x_ref.shape[0]
    i = pl.program_id(0)
    n_tiles = pl.num_programs(0) - 1
    bufs = ((ybuf_a, sem_a), (ybuf_b, sem_b))

    def copy(buf, sem, t, k):
        return pltpu.make_async_copy(y_hbm.at[pl.ds(dest_ref[0, k, t], 1), :],
                                     buf.at[k, pl.ds(t, 1), :], sem)

    def finish(buf, sem, gather=None):
        x = x_ref[...]
        xb = x.astype(BF16)
        hidden = (_silu(_dot(xb, sg_ref[...])) * _dot(xb, su_ref[...])).astype(BF16)
        acc = _dot(hidden, sd_ref[...])
        pltpu.make_async_copy(buf, buf, sem).wait()
        if gather is not None:
            for t in range(tc):
                for k in range(TOP_K):
                    copy(*gather, t, k).start()
        w8 = w8_ref[...]
        half = buf.shape[2]
        routed_lo = jnp.zeros((tc, half), F32)
        routed_hi = jnp.zeros((tc, half), F32)
        for k in range(TOP_K):
            y_lo, y_hi = _unpack_bf16_pair(buf[k])
            routed_lo = routed_lo + w8[:, k:k + 1] * y_lo
            routed_hi = routed_hi + w8[:, k:k + 1] * y_hi
        acc = acc + jnp.concatenate([routed_lo, routed_hi], axis=1)
        y = _layer_norm_rows(DN_ALPHA * x + acc, g_ref[...], b_ref[...])
        o_ref[...] = y
        obf_ref[...] = y.astype(BF16)

    @pl.when(i == 0)
    def _():
        def issue(t, carry):
            for k in range(TOP_K):
                copy(*bufs[0], t, k).start()
            return carry
        lax.fori_loop(0, tc, issue, 0)

    for parity in range(2):
        @pl.when(jnp.logical_and(jnp.logical_and(i >= 1, i < n_tiles), i % 2 == parity))
        def _(parity=parity):
            finish(*bufs[1 - parity], gather=bufs[parity])

        @pl.when(jnp.logical_and(i == n_tiles, (i - 1) % 2 == parity))
        def _(parity=parity):
            finish(*bufs[parity])


def _combine(x, w8_rows, dest_tiles, y_sorted, s_gate, s_up, s_down, g, b):
    n, d = x.shape
    tc = dest_tiles.shape[2]
    sdim = s_gate.shape[1]
    const = lambda shape: pl.BlockSpec(shape, lambda i: (0,) * len(shape))
    n_tiles = n // tc
    prev = lambda i: (jnp.maximum(i - 1, 0), 0)
    row = pl.BlockSpec((tc, d), prev)
    ybuf = pltpu.VMEM((TOP_K, tc, d // 2), jnp.int32)
    return pl.pallas_call(
        _combine_kernel,
        grid=(n_tiles + 1,),
        in_specs=[pl.BlockSpec((1, TOP_K, tc), lambda i: (jnp.minimum(i, n_tiles - 1), 0, 0),
                               memory_space=pltpu.SMEM),
                  row, pl.BlockSpec((tc, TOP_K), prev),
                  const((d, sdim)), const((d, sdim)), const((sdim, d)), const((1, d)), const((1, d)),
                  pl.BlockSpec(memory_space=pl.ANY)],
        out_specs=[row, row],
        out_shape=[jax.ShapeDtypeStruct((n, d), F32), jax.ShapeDtypeStruct((n, d), BF16)],
        scratch_shapes=[ybuf, ybuf, pltpu.SemaphoreType.DMA(()), pltpu.SemaphoreType.DMA(())],
        compiler_params=_params("arbitrary"),
        name="moe_combine",
    )(dest_tiles, x, w8_rows, s_gate, s_up, s_down, g.reshape(1, d), b.reshape(1, d), y_sorted)


def _tile_major(a, tile):
    k, n = a.shape
    return a.reshape(k, n // tile, tile).transpose(1, 0, 2)


def _moe(x, x_packed, router_w, router_bias, w_gate, w_up, w_down, layer, s_gate, s_up, s_down, ln_g, ln_b):
    n, d = x.shape
    tm = EXPERT_TILE
    eidx, rank, w8, cnt = _router(x, router_w, router_bias)
    tiles_e = (cnt + tm - 1) // tm
    tile_end = jnp.cumsum(tiles_e)
    tile_start = tile_end - tiles_e
    row_start = tile_start * tm
    e_ids = jnp.arange(N_EXPERTS, dtype=jnp.int32)
    dest = jnp.sum(jnp.where(eidx[None] == e_ids[:, None, None], row_start[:, None, None], 0), axis=0) + rank
    max_tiles = (n * TOP_K) // tm + N_EXPERTS
    num_tiles = tile_end[-1]
    tile_ids = jnp.arange(max_tiles, dtype=jnp.int32)
    t_clamped = jnp.minimum(tile_ids, num_tiles - 1)
    tile_expert = jnp.sum(t_clamped[:, None] >= tile_end[None, :], axis=1).astype(jnp.int32)
    tile_valid = jnp.clip(cnt[tile_expert] - (t_clamped - tile_start[tile_expert]) * tm, 0, tm).astype(jnp.int32)

    tokens = jnp.tile(jnp.arange(n, dtype=jnp.int32), TOP_K)
    _, tok_sorted = lax.sort_key_val(dest.reshape(-1), tokens)
    tok_sorted = jnp.concatenate([tok_sorted, jnp.zeros((tm,), jnp.int32)])
    cnt_before = jnp.cumsum(cnt) - cnt
    first_slot = cnt_before[tile_expert] + (t_clamped - tile_start[tile_expert]) * tm
    window = first_slot[:, None] + jnp.arange(tm, dtype=jnp.int32)[None, :]
    src_table = tok_sorted.at[window].get(mode="promise_in_bounds")
    ys = _expert_ffn(x_packed, src_table.reshape(max_tiles, 1, tm), w_gate, w_up, w_down, layer,
                     tile_expert, tile_valid, num_tiles.reshape(1).astype(jnp.int32))
    return _combine(x, w8.T, _tile_major(dest, min(COMBINE_TILE, n)), ys, s_gate, s_up, s_down, ln_g, ln_b)


def _even_layer(x, x_bf, bsz, seq, w_in, gn_g, w_out, rel_bias, lb, ln_g, ln_b):
    heads = w_out.shape[0] // (2 * HEAD_DIM)
    width = heads * HEAD_DIM
    proj_width = w_in.shape[1]
    col_scale = jnp.concatenate([jnp.full((1, width), HEAD_DIM ** -0.5, F32),
                                 jnp.ones((1, proj_width - width), F32)], axis=1)
    proj = _matmul(x_bf, w_in, col_scale, F32)
    o_a = _dilated_mixture(proj, rel_bias, bsz, seq, heads, proj_width)
    o_b = _hgrn2(proj, lb, gn_g, bsz, seq, heads, proj_width, first_group=3)
    return _proj_residual_ln([o_a, o_b], [w_out[:width], w_out[width:]], x, ln_g, ln_b)


def _odd_layer(x, x_bf, bsz, seq, w_qkv, w_f, b_f, w_out, ln_g, ln_b):
    heads = w_f.shape[1]
    width = heads * HEAD_DIM
    col_scale = jnp.concatenate([jnp.full((1, width), LOG2_E * HEAD_DIM ** -0.5, F32),
                                 jnp.ones((1, 2 * width), F32)], axis=1)
    qkv = _matmul(x_bf, w_qkv, col_scale, BF16)
    qx, kx = _fox_gate_terms(x, w_f, b_f, bsz, seq)
    o = _fox_attention(qkv, qx, kx, bsz, seq, heads)
    return _proj_residual_ln([o], [w_out], x, ln_g, ln_b)


def kernel(x, rel_bias, hgrn_lb_logits, a_w_in, a_gn_g, a_w_out, c_w_in, c_b_f, c_w_out, ln_mix_g, ln_mix_b, ln_ffn_g, ln_ffn_b, router_w, router_bias, exp_w_gate, exp_w_up, exp_w_down, sh_w_gate, sh_w_up, sh_w_down):
    bsz, seq, d = x.shape
    depth = ln_mix_g.shape[0]
    lb_all = jnp.cumsum(jax.nn.softmax(hgrn_lb_logits.astype(F32), axis=0), axis=0)
    h = x.reshape(bsz * seq, d).astype(F32)
    h_bf = h.astype(BF16)
    for layer in range(depth):
        j = layer // 2
        if layer % 2 == 0:
            h, h_packed = _even_layer(h, h_bf, bsz, seq, a_w_in[j].astype(BF16), a_gn_g[j],
                                  a_w_out[j].astype(BF16), rel_bias, lb_all[layer],
                                  ln_mix_g[layer], ln_mix_b[layer])
        else:
            c_width = c_w_out.shape[1]
            w_in = c_w_in[j]
            h, h_packed = _odd_layer(h, h_bf, bsz, seq, w_in[:, :3 * c_width].astype(BF16),
                                 w_in[:, 3 * c_width:], c_b_f[j], c_w_out[j].astype(BF16),
                                 ln_mix_g[layer], ln_mix_b[layer])
        h, h_bf = _moe(h, h_packed, router_w[layer], router_bias[layer], exp_w_gate, exp_w_up, exp_w_down, layer,
                       sh_w_gate[layer].astype(BF16), sh_w_up[layer].astype(BF16),
                       sh_w_down[layer].astype(BF16), ln_ffn_g[layer], ln_ffn_b[layer])
    return h.reshape(bsz, seq, d)
```

```python
import functools

import numpy as np
import jax
import jax.numpy as jnp
from jax import lax
from jax.experimental import pallas as pl
from jax.experimental.pallas import tpu as pltpu

F32 = jnp.float32
BF16 = jnp.bfloat16

HEAD_DIM = 128
A_PATTERNS = ((128, 1), (512, 4), (2048, 16))
A_BLOCK = 128
REL_BUCKETS = 32
REL_MAX_DIST = 2048
N_EXPERTS = 64
TOP_K = 8
N_GROUPS = 8
GROUP_SIZE = N_EXPERTS // N_GROUPS
TOPK_GROUPS = 4
ROUTED_SCALE = 2.5
LN_EPS = 1e-5
RMS_EPS = 1e-6
DEPTH = 2
DN_ALPHA = (2 * DEPTH) ** 0.25
NEG_INF = -1e30
LOG2_E = 1.4426950408889634

DILATED_UNROLL = 8
HGRN_CHUNK = 128
HGRN_BCAST_MIN = 8
FOX_Q_TILE = 512
FOX_K_TILE = 256
ROUTER_TILE = 512
EXPERT_TILE = 512
COMBINE_TILE = 128
VMEM_LIMIT = 56 * 1024 * 1024

_NT = (((1,), (1,)), ((), ()))
_TN = (((0,), (0,)), ((), ()))


def _dot(a, b):
    return jnp.dot(a, b, preferred_element_type=F32)


def _dot_nt(a, b):
    return lax.dot_general(a, b, _NT, preferred_element_type=F32)


def _dot_tn(a, b):
    return lax.dot_general(a, b, _TN, preferred_element_type=F32)


def _sigmoid(x):
    return 1.0 / (1.0 + jnp.exp(-x))


def _silu(x):
    return x * _sigmoid(x)


def _split3(x, axis):
    hi = x.astype(BF16)
    r1 = x - hi.astype(F32)
    mid = r1.astype(BF16)
    lo = (r1 - mid.astype(F32)).astype(BF16)
    return jnp.concatenate([hi, mid, lo], axis=axis)


def _layer_norm_rows(y, g, b):
    mu = jnp.mean(y, axis=-1, keepdims=True)
    yc = y - mu
    var = jnp.mean(yc * yc, axis=-1, keepdims=True)
    return yc * lax.rsqrt(var + LN_EPS) * g + b


def _params(*sem):
    return pltpu.CompilerParams(dimension_semantics=sem, vmem_limit_bytes=VMEM_LIMIT)


def _mm_kernel(x_ref, w_ref, cs_ref, o_ref):
    o_ref[...] = (_dot(x_ref[...], w_ref[...]) * cs_ref[...]).astype(o_ref.dtype)


def _matmul(x, w, col_scale, out_dtype, tm=1024, tn=1024):
    m, k = x.shape
    nw = w.shape[1]
    tm, tn = min(tm, m), min(tn, nw)
    assert m % tm == 0 and nw % tn == 0
    return pl.pallas_call(
        _mm_kernel,
        grid=(nw // tn, m // tm),
        in_specs=[pl.BlockSpec((tm, k), lambda j, i: (i, 0)),
                  pl.BlockSpec((k, tn), lambda j, i: (0, j)),
                  pl.BlockSpec((1, tn), lambda j, i: (0, j))],
        out_specs=pl.BlockSpec((tm, tn), lambda j, i: (i, j)),
        out_shape=jax.ShapeDtypeStruct((m, nw), out_dtype),
        compiler_params=_params("parallel", "parallel"),
        name="proj_matmul",
    )(x, w, col_scale)


def _proj_ln_kernel(*refs, n_in):
    a_refs, w_refs = refs[:n_in], refs[n_in:2 * n_in]
    r_ref, g_ref, b_ref, o_ref, obf_ref = refs[2 * n_in:]
    acc = _dot(a_refs[0][...], w_refs[0][...])
    for a_ref, w_ref in zip(a_refs[1:], w_refs[1:]):
        acc = acc + _dot(a_ref[...], w_ref[...])
    y = _layer_norm_rows(DN_ALPHA * r_ref[...] + acc, g_ref[...], b_ref[...])
    o_ref[...] = y
    half = y.shape[1] // 2
    obf_ref[...] = _pack_bf16_pair(y[:, :half], y[:, half:])


def _pack_bf16_pair(lo, hi):
    lo_bits = lax.shift_right_logical(lax.bitcast_convert_type(lo.astype(BF16).astype(F32), jnp.int32), 16)
    hi_bits = lax.bitcast_convert_type(hi.astype(BF16).astype(F32), jnp.int32) & jnp.int32(-65536)
    return lo_bits | hi_bits


def _unpack_bf16_pair(words):
    lo = lax.bitcast_convert_type(lax.shift_left(words, 16), F32)
    hi = lax.bitcast_convert_type(words & jnp.int32(-65536), F32)
    return lo, hi


def _proj_residual_ln(acts, weights, resid, g, b, tm=256):
    m, d = resid.shape
    tm = min(tm, m)
    n_in = len(acts)
    in_specs = [pl.BlockSpec((tm, a.shape[1]), lambda i: (i, 0)) for a in acts]
    in_specs += [pl.BlockSpec(w.shape, lambda i: (0, 0)) for w in weights]
    in_specs += [pl.BlockSpec((tm, d), lambda i: (i, 0)),
                 pl.BlockSpec((1, d), lambda i: (0, 0)),
                 pl.BlockSpec((1, d), lambda i: (0, 0))]
    return pl.pallas_call(
        functools.partial(_proj_ln_kernel, n_in=n_in),
        grid=(m // tm,),
        in_specs=in_specs,
        out_specs=[pl.BlockSpec((tm, d), lambda i: (i, 0)), pl.BlockSpec((tm, d // 2), lambda i: (i, 0))],
        out_shape=[jax.ShapeDtypeStruct((m, d), F32), jax.ShapeDtypeStruct((m, d // 2), jnp.int32)],
        compiler_params=_params("parallel"),
        name="out_proj_ln",
    )(*acts, *weights, resid, g.reshape(1, d), b.reshape(1, d))


def _t5_bucket(dist):
    max_exact = REL_BUCKETS // 2
    d = np.maximum(dist, 1).astype(np.float64)
    large = max_exact + (np.log(d / max_exact) / np.log(REL_MAX_DIST / max_exact)
                         * (REL_BUCKETS - max_exact)).astype(np.int64)
    large = np.minimum(large, REL_BUCKETS - 1)
    return np.where(dist < max_exact, dist, large).astype(np.int32)


def _dilated_kernel(q_ref, k_ref, v_ref, bucket_ref, rel_bias_ref, out_ref, o_scr, lse_scr):
    blk = A_BLOCK
    seq = q_ref.shape[1]
    head = pl.program_id(1)
    qi = lax.broadcasted_iota(jnp.int32, (blk, 2 * blk), 0)
    ki = lax.broadcasted_iota(jnp.int32, (blk, 2 * blk), 1)
    rel = qi + blk - ki

    for bi, (window, dil) in enumerate(A_PATTERNS):
        band = window // dil
        n_blocks = seq // (dil * blk)
        valid = jnp.abs(2 * rel - band) <= band
        bucket = bucket_ref[bi]
        bias = jnp.zeros((blk, 2 * blk), F32)
        for b in range(REL_BUCKETS):
            bias = jnp.where(bucket == b, rel_bias_ref[head, b], bias)

        def attend(q_start, k_start, n_keys, bias_blk, valid_blk, bi=bi, dil=dil):
            q_rows = pl.ds(q_start, blk, stride=dil)
            k_rows = pl.ds(k_start, n_keys, stride=dil)
            q = q_ref[0, q_rows, :].astype(BF16)
            kk = k_ref[0, k_rows, :].astype(BF16)
            vv = v_ref[0, k_rows, :].astype(BF16)
            s = _dot_nt(q, kk) + bias_blk
            s = jnp.where(valid_blk, s, NEG_INF)
            m = jnp.max(s, axis=-1, keepdims=True)
            p = jnp.exp(s - m)
            l = jnp.sum(p, axis=-1, keepdims=True)
            o_scr[bi, q_rows, :] = _dot(p.astype(BF16), vv) / l
            lse_scr[bi, q_rows, :] = jnp.broadcast_to(m + jnp.log(l), (blk, HEAD_DIM))

        def first(r, carry, attend=attend, bias=bias, valid=valid):
            attend(r, r, blk, bias[:, blk:], valid[:, blk:])
            return carry

        lax.fori_loop(0, dil, first, 0, unroll=min(dil, DILATED_UNROLL))

        def rest(t, carry, attend=attend, bias=bias, valid=valid, dil=dil):
            r = t & (dil - 1)
            n = (t >> (dil.bit_length() - 1)) + 1
            attend(r + n * blk * dil, r + (n - 1) * blk * dil, 2 * blk, bias, valid)
            return carry

        lax.fori_loop(0, dil * (n_blocks - 1), rest, 0, unroll=DILATED_UNROLL)

    chunk = 2 * blk
    nb = len(A_PATTERNS)

    def mix(c, carry):
        rows = pl.ds(pl.multiple_of(c * chunk, chunk), chunk)
        lses = [lse_scr[b, rows, :] for b in range(nb)]
        m = functools.reduce(jnp.maximum, lses)
        es = [jnp.exp(x - m) for x in lses]
        den = functools.reduce(lambda a, b: a + b, es)
        num = functools.reduce(lambda a, b: a + b, [e * o_scr[b, rows, :] for b, e in enumerate(es)])
        out_ref[0, rows, :] = (num / den).astype(out_ref.dtype)
        return carry

    lax.fori_loop(0, seq // chunk, mix, 0)


def _dilated_bucket_tables():
    qi = np.arange(A_BLOCK)[:, None]
    ki = np.arange(2 * A_BLOCK)[None, :]
    rel = qi + A_BLOCK - ki
    return np.stack([_t5_bucket(np.clip(rel, 0, None) * dilation) for _, dilation in A_PATTERNS], axis=0)


def _dilated_mixture(proj, rel_bias, bsz, seq, heads, proj_width):
    for window, dilation in A_PATTERNS:
        assert seq % (dilation * A_BLOCK) == 0 and window // dilation <= A_BLOCK
    nb = len(A_PATTERNS)
    view = proj.reshape(bsz, seq, proj_width)
    buckets = jnp.asarray(_dilated_bucket_tables())

    def in_spec(group):
        return pl.BlockSpec((1, seq, HEAD_DIM), lambda b, h, g=group: (b, 0, g * heads + h))

    o = pl.pallas_call(
        _dilated_kernel,
        grid=(bsz, heads),
        in_specs=[in_spec(0), in_spec(1), in_spec(2),
                  pl.BlockSpec(buckets.shape, lambda b, h: (0, 0, 0)),
                  pl.BlockSpec(memory_space=pltpu.SMEM)],
        out_specs=pl.BlockSpec((1, seq, HEAD_DIM), lambda b, h: (b, 0, h)),
        out_shape=jax.ShapeDtypeStruct((bsz, seq, heads * HEAD_DIM), BF16),
        scratch_shapes=[pltpu.VMEM((nb, seq, HEAD_DIM), F32), pltpu.VMEM((nb, seq, HEAD_DIM), F32)],
        compiler_params=_params("parallel", "parallel"),
        name="dilated_attention",
    )(view, view, view, buckets, rel_bias.astype(F32).T)
    return o.reshape(bsz * seq, heads * HEAD_DIM)


def _hgrn_levels(t):
    cs = []
    c = t // 2
    while c >= 1:
        cs.append(c)
        c //= 2
    return cs


def _hgrn_consts(t):
    idx = np.arange(t)
    ltri = (idx[:, None] >= idx[None, :]).astype(np.float32)
    sels = []
    for c in _hgrn_levels(t):
        if c < HGRN_BCAST_MIN:
            ref = (idx // (2 * c)) * (2 * c) + c - 1
            sels.append((idx[None, :] == ref[:, None]).astype(np.float32))
    return jnp.asarray(ltri, BF16), jnp.asarray(np.concatenate(sels, axis=0), BF16)


def _hgrn_kernel(q_ref, f_ref, i_ref, g_ref, lb_ref, gn_ref, ltri_ref, sel_ref, o_ref, state_ref,
                 *, n_chunks):
    t = HGRN_CHUNK
    dk = HEAD_DIM
    levels = _hgrn_levels(t)
    lb = lb_ref[0]
    gn = gn_ref[0]
    ltri = ltri_ref[...]
    sel = sel_ref[...]
    row = lax.broadcasted_iota(jnp.int32, (t, dk), 0)
    ti = lax.broadcasted_iota(jnp.int32, (t, t), 0)
    si = lax.broadcasted_iota(jnp.int32, (t, t), 1)
    state_ref[...] = jnp.zeros_like(state_ref)

    def sum3(y):
        return y[:, :dk] + y[:, dk:2 * dk] + y[:, 2 * dk:]

    def chunk(n, carry):
        st = pl.multiple_of(n * t, t)
        fq = _silu(q_ref[0, pl.ds(st, t), :])
        forget = lb + (1.0 - lb) * _sigmoid(f_ref[0, pl.ds(st, t), :])
        logf = jnp.log(forget)
        key = 1.0 - forget
        val = i_ref[0, pl.ds(st, t), :].astype(BF16)
        a = sum3(_dot(ltri, _split3(logf, 1)))
        aref_fine = sum3(_dot(sel, _split3(a, 1)))
        fq_b = fq.astype(BF16)
        key_b = key.astype(BF16)
        scores = jnp.where(ti == si, _dot_nt(fq_b, key_b), 0.0)
        n_fine = 0
        for c in levels:
            if c >= HGRN_BCAST_MIN:
                aref = jnp.concatenate(
                    [jnp.broadcast_to(a[b0 + c - 1:b0 + c, :], (2 * c, dk)) for b0 in range(0, t, 2 * c)],
                    axis=0)
            else:
                aref = aref_fine[n_fine * t:(n_fine + 1) * t, :]
                n_fine += 1
            upper = (row & (2 * c - 1)) >= c
            decayed = jnp.where(upper, fq, key) * jnp.exp(-jnp.abs(a - aref))
            ql = jnp.where(upper, decayed, 0.0).astype(BF16)
            kl = jnp.where(upper, 0.0, decayed).astype(BF16)
            s_l = _dot_nt(ql, kl)
            if 2 * c < t:
                sh = (2 * c).bit_length() - 1
                s_l = jnp.where((ti >> sh) == (si >> sh), s_l, 0.0)
            scores = scores + s_l
        state_t = state_ref[...]
        inter = _dot_nt((fq * jnp.exp(a)).astype(BF16), state_t.astype(BF16))
        o = inter + _dot(scores.astype(BF16), val)
        a_last = a[t - 1:t, :]
        k_dec = (key * jnp.exp(a_last - a)).astype(BF16)
        state_ref[...] = state_t * jnp.exp(a_last) + _dot_tn(val, k_dec)
        o = o * lax.rsqrt(jnp.mean(o * o, axis=-1, keepdims=True) + RMS_EPS) * gn
        o = o * _silu(g_ref[0, pl.ds(st, t), :])
        o_ref[0, pl.ds(st, t), :] = o.astype(o_ref.dtype)
        return carry

    lax.fori_loop(0, n_chunks, chunk, 0, unroll=4)


def _hgrn2(proj, lb, gn_g, bsz, seq, heads, proj_width, first_group):
    assert seq % HGRN_CHUNK == 0
    view = proj.reshape(bsz, seq, proj_width)
    ltri, sel = _hgrn_consts(HGRN_CHUNK)

    def in_spec(group):
        return pl.BlockSpec((1, seq, HEAD_DIM), lambda b, h, g=group: (b, 0, (first_group + g) * heads + h))

    head_vec = pl.BlockSpec((1, 1, HEAD_DIM), lambda b, h: (h, 0, 0))
    o = pl.pallas_call(
        functools.partial(_hgrn_kernel, n_chunks=seq // HGRN_CHUNK),
        grid=(bsz, heads),
        in_specs=[in_spec(0), in_spec(1), in_spec(2), in_spec(3), head_vec, head_vec,
                  pl.BlockSpec(ltri.shape, lambda b, h: (0, 0)),
                  pl.BlockSpec(sel.shape, lambda b, h: (0, 0))],
        out_specs=pl.BlockSpec((1, seq, HEAD_DIM), lambda b, h: (b, 0, h)),
        out_shape=jax.ShapeDtypeStruct((bsz, seq, heads * HEAD_DIM), BF16),
        scratch_shapes=[pltpu.VMEM((HEAD_DIM, HEAD_DIM), F32)],
        compiler_params=_params("parallel", "parallel"),
        name="hgrn2",
    )(view, view, view, view, lb.reshape(heads, 1, HEAD_DIM).astype(F32),
      gn_g.reshape(heads, 1, HEAD_DIM).astype(F32), ltri, sel)
    return o.reshape(bsz * seq, heads * HEAD_DIM)


def _fox_gate_kernel(x_ref, w_ref, b_ref, ltri_ref, place_q_ref, place_k_ref, ones_q_ref, ones_k_ref,
                     qx_ref, kx_ref, carry):
    heads = w_ref.shape[1]
    ts = x_ref.shape[1]

    @pl.when(pl.program_id(1) == 0)
    def _():
        carry[...] = jnp.zeros_like(carry)

    def log_sigmoid(z):
        return jnp.minimum(z, 0.0) - jnp.log(1.0 + jnp.exp(-jnp.abs(z)))

    x = x_ref[0]
    xh = x.astype(BF16)
    xl = (x - xh.astype(F32)).astype(BF16)
    w = w_ref[...]
    wh = w.astype(BF16)
    wl = (w - wh.astype(F32)).astype(BF16)
    f = _dot(xh, wh) + _dot(xh, wl) + _dot(xl, wh) + b_ref[...]
    c = _dot(ltri_ref[...], _split3(log_sigmoid(f), 1))
    c = c[:, :heads] + c[:, heads:2 * heads] + c[:, 2 * heads:] + carry[...]
    carry[...] = c[ts - 1:ts, :]
    terms = _split3(c * LOG2_E, 1)
    qx_ref[0] = (_dot(terms, place_q_ref[...]) + ones_q_ref[...]).astype(BF16)
    kx_ref[0] = (_dot(terms, place_k_ref[...]) + ones_k_ref[...]).astype(BF16)


def _fox_gate_placement(heads):
    place_q = np.zeros((3 * heads, heads * HEAD_DIM), np.float32)
    place_k = np.zeros((3 * heads, heads * HEAD_DIM), np.float32)
    ones_q = np.zeros((1, heads * HEAD_DIM), np.float32)
    ones_k = np.zeros((1, heads * HEAD_DIM), np.float32)
    for h in range(heads):
        for j in range(3):
            place_q[j * heads + h, h * HEAD_DIM + j] = 1.0
            place_k[j * heads + h, h * HEAD_DIM + 3 + j] = -1.0
            ones_q[0, h * HEAD_DIM + 3 + j] = 1.0
            ones_k[0, h * HEAD_DIM + j] = 1.0
    return jnp.asarray(place_q, BF16), jnp.asarray(place_k, BF16), jnp.asarray(ones_q), jnp.asarray(ones_k)


def _fox_gate_terms(x, w_f, b_f, bsz, seq, ts=512):
    d = x.shape[-1]
    heads = w_f.shape[1]
    ts = min(ts, seq)
    idx = np.arange(ts)
    ltri = jnp.asarray(idx[:, None] >= idx[None, :], BF16)
    consts = _fox_gate_placement(heads)
    const = lambda shape: pl.BlockSpec(shape, lambda b, i: (0,) * len(shape))
    out_spec = pl.BlockSpec((1, ts, heads * HEAD_DIM), lambda b, i: (b, i, 0))
    out_sds = jax.ShapeDtypeStruct((bsz, seq, heads * HEAD_DIM), BF16)
    return pl.pallas_call(
        _fox_gate_kernel,
        grid=(bsz, seq // ts),
        in_specs=[pl.BlockSpec((1, ts, d), lambda b, i: (b, i, 0)),
                  const((d, heads)), const((1, heads)), const((ts, ts))] + [const(a.shape) for a in consts],
        out_specs=[out_spec, out_spec],
        out_shape=[out_sds, out_sds],
        scratch_shapes=[pltpu.VMEM((1, heads), F32)],
        compiler_params=_params("parallel", "arbitrary"),
        name="fox_gate_cumsum",
    )(x.reshape(bsz, seq, d), w_f.astype(F32), b_f.astype(F32).reshape(1, heads), ltri, *consts)


def _fox_kernel(q_ref, qx_ref, k_ref, kx_ref, vt_ref, o_ref, s_a, s_b, p_a, p_b, acc_ref):
    tq, tk = FOX_Q_TILE, FOX_K_TILE
    lanes = HEAD_DIM
    i = pl.program_id(2)
    q = jnp.concatenate([q_ref[0], qx_ref[0]], axis=1)

    def logits_into(j, s_ref):
        st = pl.multiple_of(j * tk, tk)
        kk = jnp.concatenate([k_ref[0, pl.ds(st, tk), :], kx_ref[0, pl.ds(st, tk), :]], axis=1)
        s_ref[...] = _dot_nt(kk, q)

    def softmax_pv(j, s_ref, p_ref, m, l, diag):
        m_out, l_out, alphas = [], [], []
        for c in range(tq // lanes):
            cols = slice(c * lanes, (c + 1) * lanes)
            if diag is not None and (c + 1) * lanes - 1 < diag * tk:
                m_out.append(m[:, cols])
                l_out.append(l[:, cols])
                alphas.append(jnp.ones((1, lanes), F32))
                p_ref[:, cols] = jnp.zeros((tk, lanes), BF16)
                continue
            s = s_ref[:, cols]
            if diag is not None and c * lanes < diag * tk + tk - 1:
                key = lax.broadcasted_iota(jnp.int32, (tk, lanes), 0) + diag * tk
                qry = lax.broadcasted_iota(jnp.int32, (tk, lanes), 1) + c * lanes
                s = jnp.where(key <= qry, s, NEG_INF)
            m_new = jnp.maximum(m[:, cols], jnp.max(s, axis=0, keepdims=True))
            alpha = jnp.exp2(m[:, cols] - m_new)
            p = jnp.exp2(s - m_new)
            l_out.append(alpha * l[:, cols] + jnp.sum(p, axis=0, keepdims=True))
            m_out.append(m_new)
            alphas.append(alpha)
            p_ref[:, cols] = p.astype(BF16)
        vt = vt_ref[0, :, pl.ds(pl.multiple_of(j * tk, tk), tk)]
        acc_ref[...] = jnp.concatenate(alphas, axis=1) * acc_ref[...] + _dot(vt, p_ref[...])
        return jnp.concatenate(m_out, axis=1), jnp.concatenate(l_out, axis=1)

    def pair(jp, carry):
        m, l = carry
        t0 = 2 * jp
        logits_into(t0 + 1, s_b)
        m, l = softmax_pv(t0, s_a, p_a, m, l, None)
        logits_into(t0 + 2, s_a)
        return softmax_pv(t0 + 1, s_b, p_b, m, l, None)

    assert tq == 2 * tk
    acc_ref[...] = jnp.zeros_like(acc_ref)
    logits_into(0, s_a)
    m, l = lax.fori_loop(0, i, pair, (jnp.full((1, tq), NEG_INF, F32), jnp.zeros((1, tq), F32)))
    t0 = 2 * i
    logits_into(t0 + 1, s_b)
    m, l = softmax_pv(t0, s_a, p_a, m, l, 0)
    m, l = softmax_pv(t0 + 1, s_b, p_b, m, l, 1)
    o_ref[0] = jnp.transpose(acc_ref[...] / l).astype(o_ref.dtype)


def _fox_attention(qkv, qx, kx, bsz, seq, heads):
    tq = FOX_Q_TILE
    assert seq % tq == 0 and tq % FOX_K_TILE == 0
    width = heads * HEAD_DIM
    view = qkv.reshape(bsz, seq, 3 * width)
    v_t = jnp.swapaxes(view[:, :, 2 * width:], 1, 2)
    q_spec = lambda off: pl.BlockSpec((1, tq, HEAD_DIM), lambda b, h, i: (b, i, off + h))
    kv_spec = lambda off: pl.BlockSpec((1, seq, HEAD_DIM), lambda b, h, i: (b, 0, off + h))
    o = pl.pallas_call(
        _fox_kernel,
        grid=(bsz, heads, seq // tq),
        in_specs=[q_spec(0), q_spec(0), kv_spec(heads), kv_spec(0),
                  pl.BlockSpec((1, HEAD_DIM, seq), lambda b, h, i: (b, h, 0))],
        out_specs=pl.BlockSpec((1, tq, HEAD_DIM), lambda b, h, i: (b, i, h)),
        out_shape=jax.ShapeDtypeStruct((bsz, seq, width), BF16),
        scratch_shapes=[pltpu.VMEM((FOX_K_TILE, tq), F32), pltpu.VMEM((FOX_K_TILE, tq), F32),
                        pltpu.VMEM((FOX_K_TILE, tq), BF16), pltpu.VMEM((FOX_K_TILE, tq), BF16),
                        pltpu.VMEM((HEAD_DIM, tq), F32)],
        compiler_params=_params("parallel", "parallel", "parallel"),
        name="fox_attention",
    )(view, qx, view, kx, v_t)
    return o.reshape(bsz * seq, width)


def _router_kernel(x_ref, wt_ref, bias_ref, utri_ref, eidx_ref, rank_ref, w_ref, cnt_ref, carry):
    gsz = GROUP_SIZE
    tt = x_ref.shape[0]

    @pl.when(pl.program_id(0) == 0)
    def _():
        carry[...] = jnp.zeros_like(carry)

    x = x_ref[...]
    xh = x.astype(BF16)
    xl = (x - xh.astype(F32)).astype(BF16)
    wt = wt_ref[...]
    wh = wt.astype(BF16)
    wl = (wt - wh.astype(F32)).astype(BF16)
    logits = _dot_nt(wh, xh) + _dot_nt(wh, xl) + _dot_nt(wl, xh)
    scores = _sigmoid(logits)
    sel = scores + bias_ref[...]
    sub = lax.broadcasted_iota(jnp.int32, (gsz, tt), 0)
    neg = -jnp.inf

    group_score = []
    for g in range(N_GROUPS):
        blk = sel[g * gsz:(g + 1) * gsz, :]
        m1 = jnp.max(blk, axis=0, keepdims=True)
        first = jnp.min(jnp.where(blk == m1, sub, gsz), axis=0, keepdims=True)
        m2 = jnp.max(jnp.where(sub == first, neg, blk), axis=0, keepdims=True)
        group_score.append(m1 + m2)

    chosen = [jnp.zeros((1, tt), F32) for _ in range(N_GROUPS)]
    for _ in range(TOPK_GROUPS):
        m = functools.reduce(jnp.maximum, group_score)
        gi = jnp.full((1, tt), N_GROUPS, jnp.int32)
        for g in reversed(range(N_GROUPS)):
            gi = jnp.where(group_score[g] == m, g, gi)
        for g in range(N_GROUPS):
            hit = gi == g
            chosen[g] = jnp.where(hit, 1.0, chosen[g])
            group_score[g] = jnp.where(hit, neg, group_score[g])

    cand = [jnp.where(jnp.broadcast_to(chosen[g], (gsz, tt)) > 0.5, sel[g * gsz:(g + 1) * gsz, :], NEG_INF)
            for g in range(N_GROUPS)]
    score_g = [scores[g * gsz:(g + 1) * gsz, :] for g in range(N_GROUPS)]
    eiota = [sub + g * gsz for g in range(N_GROUPS)]
    picked = [jnp.zeros((gsz, tt), F32) for _ in range(N_GROUPS)]
    top_idx, top_score = [], []
    for _ in range(TOP_K):
        m = functools.reduce(jnp.maximum, [jnp.max(c, axis=0, keepdims=True) for c in cand])
        ei = functools.reduce(jnp.minimum, [
            jnp.min(jnp.where(cand[g] == m, eiota[g], N_EXPERTS), axis=0, keepdims=True)
            for g in range(N_GROUPS)])
        sc = jnp.zeros((1, tt), F32)
        for g in range(N_GROUPS):
            hit = eiota[g] == ei
            picked[g] = jnp.where(hit, 1.0, picked[g])
            cand[g] = jnp.where(hit, neg, cand[g])
            sc = sc + jnp.sum(jnp.where(hit, score_g[g], 0.0), axis=0, keepdims=True)
        top_idx.append(ei)
        top_score.append(sc)

    total = functools.reduce(lambda a, b: a + b, top_score)
    mask = jnp.concatenate(picked, axis=0)
    incl = _dot(mask.astype(BF16), utri_ref[...])
    rank = carry[...] + incl - mask
    for r in range(TOP_K):
        ei = top_idx[r]
        rk = jnp.zeros((1, tt), F32)
        for g in range(N_GROUPS):
            rk = rk + jnp.sum(jnp.where(eiota[g] == ei, rank[g * gsz:(g + 1) * gsz, :], 0.0),
                              axis=0, keepdims=True)
        eidx_ref[r:r + 1, :] = ei
        rank_ref[r:r + 1, :] = rk.astype(jnp.int32)
        w_ref[r:r + 1, :] = top_score[r] / total * ROUTED_SCALE
    new_carry = carry[...] + incl[:, tt - 1:tt]
    carry[...] = new_carry
    cnt_ref[...] = jnp.broadcast_to(new_carry, cnt_ref.shape)


def _router(x, router_w, router_bias):
    n, d = x.shape
    tt = min(ROUTER_TILE, n)
    idx = np.arange(tt)
    utri = jnp.asarray(idx[:, None] <= idx[None, :], BF16)
    eidx, rank, w8, cnt = pl.pallas_call(
        _router_kernel,
        grid=(n // tt,),
        in_specs=[pl.BlockSpec((tt, d), lambda i: (i, 0)),
                  pl.BlockSpec((N_EXPERTS, d), lambda i: (0, 0)),
                  pl.BlockSpec((N_EXPERTS, 1), lambda i: (0, 0)),
                  pl.BlockSpec((tt, tt), lambda i: (0, 0))],
        out_specs=[pl.BlockSpec((TOP_K, tt), lambda i: (0, i)),
                   pl.BlockSpec((TOP_K, tt), lambda i: (0, i)),
                   pl.BlockSpec((TOP_K, tt), lambda i: (0, i)),
                   pl.BlockSpec((N_EXPERTS, 128), lambda i: (0, 0))],
        out_shape=[jax.ShapeDtypeStruct((TOP_K, n), jnp.int32),
                   jax.ShapeDtypeStruct((TOP_K, n), jnp.int32),
                   jax.ShapeDtypeStruct((TOP_K, n), F32),
                   jax.ShapeDtypeStruct((N_EXPERTS, 128), F32)],
        scratch_shapes=[pltpu.VMEM((N_EXPERTS, 1), F32)],
        compiler_params=_params("arbitrary"),
        name="moe_router",
    )(x, router_w.astype(F32).T, router_bias.astype(F32).reshape(N_EXPERTS, 1), utri)
    return eidx, rank, w8, cnt[:, 0].astype(jnp.int32)


def _expert_kernel(te_ref, tv_ref, nt_ref, src_ref, x_hbm, wg_ref, wu_ref, wd_ref, y_ref,
                   xbuf_a, xbuf_b, sem_a, sem_b, wg_bf, wu_bf, wd_bf):
    i = pl.program_id(0)
    n_live = nt_ref[0]
    tm, half = xbuf_a.shape
    bufs = ((xbuf_a, sem_a), (xbuf_b, sem_b))
    prev = jnp.maximum(i - 1, 0)

    def copy(buf, sem, r):
        return pltpu.make_async_copy(x_hbm.at[pl.ds(src_ref[0, 0, r], 1), :], buf.at[pl.ds(r, 1), :], sem)

    def ffn(buf, sem, gather=None):
        pltpu.make_async_copy(buf, buf, sem).wait()
        if gather is not None:
            for r in range(tm):
                copy(*gather, r).start()
        row = lax.broadcasted_iota(jnp.int32, (tm, 1), 0)
        x_lo, x_hi = _unpack_bf16_pair(jnp.where(row < tv_ref[prev], buf[...], 0))
        x_lo, x_hi = x_lo.astype(BF16), x_hi.astype(BF16)
        gate = _dot(x_lo, wg_bf[:half, :]) + _dot(x_hi, wg_bf[half:, :])
        up = _dot(x_lo, wu_bf[:half, :]) + _dot(x_hi, wu_bf[half:, :])
        hidden = (_silu(gate) * up).astype(BF16)
        y = _dot(hidden, wd_bf[...])
        y_ref[...] = _pack_bf16_pair(y[:, :half], y[:, half:])

    @pl.when(i == 0)
    def _():
        def issue(r, carry):
            copy(*bufs[0], r).start()
            return carry
        lax.fori_loop(0, tm, issue, 0)

    computing = jnp.logical_and(i >= 1, i <= n_live)
    new_expert = jnp.logical_or(i == 1, te_ref[prev] != te_ref[jnp.maximum(i - 2, 0)])

    @pl.when(jnp.logical_and(computing, new_expert))
    def _():
        wg_bf[...] = wg_ref[0, 0].astype(BF16)
        wu_bf[...] = wu_ref[0, 0].astype(BF16)
        wd_bf[...] = wd_ref[0, 0].astype(BF16)

    for parity in range(2):
        @pl.when(jnp.logical_and(jnp.logical_and(i >= 1, i < n_live), i % 2 == parity))
        def _(parity=parity):
            ffn(*bufs[1 - parity], gather=bufs[parity])

        @pl.when(jnp.logical_and(i == n_live, (i - 1) % 2 == parity))
        def _(parity=parity):
            ffn(*bufs[parity])


def _expert_ffn(x_packed, src_table, w_gate, w_up, w_down, layer, tile_expert, tile_valid, num_tiles):
    n, half = x_packed.shape
    d = 2 * half
    max_tiles, _, tm = src_table.shape
    e_dim = w_gate.shape[3]
    prev = lambda i, nt: jnp.minimum(jnp.maximum(i - 1, 0), nt[0] - 1)
    w_map = lambda i, te, tv, nt: (layer, te[prev(i, nt)], 0, 0)
    xbuf = pltpu.VMEM((tm, half), jnp.int32)
    return pl.pallas_call(
        _expert_kernel,
        grid_spec=pltpu.PrefetchScalarGridSpec(
            num_scalar_prefetch=3,
            grid=(max_tiles + 1,),
            in_specs=[pl.BlockSpec((1, 1, tm), lambda i, te, tv, nt: (jnp.minimum(i, nt[0] - 1), 0, 0),
                                   memory_space=pltpu.SMEM),
                      pl.BlockSpec(memory_space=pl.ANY),
                      pl.BlockSpec((1, 1, d, e_dim), w_map),
                      pl.BlockSpec((1, 1, d, e_dim), w_map),
                      pl.BlockSpec((1, 1, e_dim, d), w_map)],
            out_specs=pl.BlockSpec((tm, half), lambda i, te, tv, nt: (prev(i, nt), 0)),
            scratch_shapes=[xbuf, xbuf, pltpu.SemaphoreType.DMA(()), pltpu.SemaphoreType.DMA(()),
                            pltpu.VMEM((d, e_dim), BF16), pltpu.VMEM((d, e_dim), BF16),
                            pltpu.VMEM((e_dim, d), BF16)]),
        out_shape=jax.ShapeDtypeStruct((max_tiles * tm, half), jnp.int32),
        compiler_params=_params("arbitrary"),
        name="moe_expert_ffn",
    )(tile_expert, tile_valid, num_tiles, src_table, x_packed, w_gate, w_up, w_down)


def _combine_kernel(dest_ref, x_ref, w8_ref, sg_ref, su_ref, sd_ref, g_ref, b_ref, y_hbm,
                    o_ref, obf_ref, ybuf_a, ybuf_b, sem_a, sem_b):
    tc = x_ref.shape[0]
    i = pl.program_id(0)
    n_tiles = pl.num_programs(0) - 1
    bufs = ((ybuf_a, sem_a), (ybuf_b, sem_b))

    def copy(buf, sem, t, k):
        return pltpu.make_async_copy(y_hbm.at[pl.ds(dest_ref[0, k, t], 1), :],
                                     buf.at[k, pl.ds(t, 1), :], sem)

    def finish(buf, sem, gather=None):
        x = x_ref[...]
        xb = x.astype(BF16)
        hidden = (_silu(_dot(xb, sg_ref[...])) * _dot(xb, su_ref[...])).astype(BF16)
        acc = _dot(hidden, sd_ref[...])
        pltpu.make_async_copy(buf, buf, sem).wait()
        if gather is not None:
            for t in range(tc):
                for k in range(TOP_K):
                    copy(*gather, t, k).start()
        w8 = w8_ref[...]
        half = buf.shape[2]
        routed_lo = jnp.zeros((tc, half), F32)
        routed_hi = jnp.zeros((tc, half), F32)
        for k in range(TOP_K):
            y_lo, y_hi = _unpack_bf16_pair(buf[k])
            routed_lo = routed_lo + w8[:, k:k + 1] * y_lo
            routed_hi = routed_hi + w8[:, k:k + 1] * y_hi
        acc = acc + jnp.concatenate([routed_lo, routed_hi], axis=1)
        y = _layer_norm_rows(DN_ALPHA * x + acc, g_ref[...], b_ref[...])
        o_ref[...] = y
        obf_ref[...] = y.astype(BF16)

    @pl.when(i == 0)
    def _():
        def issue(t, carry):
            for k in range(TOP_K):
                copy(*bufs[0], t, k).start()
            return carry
        lax.fori_loop(0, tc, issue, 0)

    for parity in range(2):
        @pl.when(jnp.logical_and(jnp.logical_and(i >= 1, i < n_tiles), i % 2 == parity))
        def _(parity=parity):
            finish(*bufs[1 - parity], gather=bufs[parity])

        @pl.when(jnp.logical_and(i == n_tiles, (i - 1) % 2 == parity))
        def _(parity=parity):
            finish(*bufs[parity])


def _combine(x, w8_rows, dest_tiles, y_sorted, s_gate, s_up, s_down, g, b):
    n, d = x.shape
    tc = dest_tiles.shape[2]
    sdim = s_gate.shape[1]
    const = lambda shape: pl.BlockSpec(shape, lambda i: (0,) * len(shape))
    n_tiles = n // tc
    prev = lambda i: (jnp.maximum(i - 1, 0), 0)
    row = pl.BlockSpec((tc, d), prev)
    ybuf = pltpu.VMEM((TOP_K, tc, d // 2), jnp.int32)
    return pl.pallas_call(
        _combine_kernel,
        grid=(n_tiles + 1,),
        in_specs=[pl.BlockSpec((1, TOP_K, tc), lambda i: (jnp.minimum(i, n_tiles - 1), 0, 0),
                               memory_space=pltpu.SMEM),
                  row, pl.BlockSpec((tc, TOP_K), prev),
                  const((d, sdim)), const((d, sdim)), const((sdim, d)), const((1, d)), const((1, d)),
                  pl.BlockSpec(memory_space=pl.ANY)],
        out_specs=[row, row],
        out_shape=[jax.ShapeDtypeStruct((n, d), F32), jax.ShapeDtypeStruct((n, d), BF16)],
        scratch_shapes=[ybuf, ybuf, pltpu.SemaphoreType.DMA(()), pltpu.SemaphoreType.DMA(())],
        compiler_params=_params("arbitrary"),
        name="moe_combine",
    )(dest_tiles, x, w8_rows, s_gate, s_up, s_down, g.reshape(1, d), b.reshape(1, d), y_sorted)


def _tile_major(a, tile):
    k, n = a.shape
    return a.reshape(k, n // tile, tile).transpose(1, 0, 2)


def _moe(x, x_packed, router_w, router_bias, w_gate, w_up, w_down, layer, s_gate, s_up, s_down, ln_g, ln_b):
    n, d = x.shape
    tm = EXPERT_TILE
    eidx, rank, w8, cnt = _router(x, router_w, router_bias)
    tiles_e = (cnt + tm - 1) // tm
    tile_end = jnp.cumsum(tiles_e)
    tile_start = tile_end - tiles_e
    row_start = tile_start * tm
    e_ids = jnp.arange(N_EXPERTS, dtype=jnp.int32)
    dest = jnp.sum(jnp.where(eidx[None] == e_ids[:, None, None], row_start[:, None, None], 0), axis=0) + rank
    max_tiles = (n * TOP_K) // tm + N_EXPERTS
    num_tiles = tile_end[-1]
    tile_ids = jnp.arange(max_tiles, dtype=jnp.int32)
    t_clamped = jnp.minimum(tile_ids, num_tiles - 1)
    tile_expert = jnp.sum(t_clamped[:, None] >= tile_end[None, :], axis=1).astype(jnp.int32)
    tile_valid = jnp.clip(cnt[tile_expert] - (t_clamped - tile_start[tile_expert]) * tm, 0, tm).astype(jnp.int32)

    tokens = jnp.tile(jnp.arange(n, dtype=jnp.int32), TOP_K)
    _, tok_sorted = lax.sort_key_val(dest.reshape(-1), tokens)
    tok_sorted = jnp.concatenate([tok_sorted, jnp.zeros((tm,), jnp.int32)])
    cnt_before = jnp.cumsum(cnt) - cnt
    first_slot = cnt_before[tile_expert] + (t_clamped - tile_start[tile_expert]) * tm
    window = first_slot[:, None] + jnp.arange(tm, dtype=jnp.int32)[None, :]
    src_table = tok_sorted.at[window].get(mode="promise_in_bounds")
    ys = _expert_ffn(x_packed, src_table.reshape(max_tiles, 1, tm), w_gate, w_up, w_down, layer,
                     tile_expert, tile_valid, num_tiles.reshape(1).astype(jnp.int32))
    return _combine(x, w8.T, _tile_major(dest, min(COMBINE_TILE, n)), ys, s_gate, s_up, s_down, ln_g, ln_b)


def _even_layer(x, x_bf, bsz, seq, w_in, gn_g, w_out, rel_bias, lb, ln_g, ln_b):
    heads = w_out.shape[0] // (2 * HEAD_DIM)
    width = heads * HEAD_DIM
    proj_width = w_in.shape[1]
    col_scale = jnp.concatenate([jnp.full((1, width), HEAD_DIM ** -0.5, F32),
                                 jnp.ones((1, proj_width - width), F32)], axis=1)
    proj = _matmul(x_bf, w_in, col_scale, F32)
    o_a = _dilated_mixture(proj, rel_bias, bsz, seq, heads, proj_width)
    o_b = _hgrn2(proj, lb, gn_g, bsz, seq, heads, proj_width, first_group=3)
    return _proj_residual_ln([o_a, o_b], [w_out[:width], w_out[width:]], x, ln_g, ln_b)


def _odd_layer(x, x_bf, bsz, seq, w_qkv, w_f, b_f, w_out, ln_g, ln_b):
    heads = w_f.shape[1]
    width = heads * HEAD_DIM
    col_scale = jnp.concatenate([jnp.full((1, width), LOG2_E * HEAD_DIM ** -0.5, F32),
                                 jnp.ones((1, 2 * width), F32)], axis=1)
    qkv = _matmul(x_bf, w_qkv, col_scale, BF16)
    qx, kx = _fox_gate_terms(x, w_f, b_f, bsz, seq)
    o = _fox_attention(qkv, qx, kx, bsz, seq, heads)
    return _proj_residual_ln([o], [w_out], x, ln_g, ln_b)


def kernel(x, rel_bias, hgrn_lb_logits, a_w_in, a_gn_g, a_w_out, c_w_in, c_b_f, c_w_out, ln_mix_g, ln_mix_b, ln_ffn_g, ln_ffn_b, router_w, router_bias, exp_w_gate, exp_w_up, exp_w_down, sh_w_gate, sh_w_up, sh_w_down):
    bsz, seq, d = x.shape
    depth = ln_mix_g.shape[0]
    lb_all = jnp.cumsum(jax.nn.softmax(hgrn_lb_logits.astype(F32), axis=0), axis=0)
    h = x.reshape(bsz * seq, d).astype(F32)
    h_bf = h.astype(BF16)
    for layer in range(depth):
        j = layer // 2
        if layer % 2 == 0:
            h, h_packed = _even_layer(h, h_bf, bsz, seq, a_w_in[j].astype(BF16), a_gn_g[j],
                                  a_w_out[j].astype(BF16), rel_bias, lb_all[layer],
                                  ln_mix_g[layer], ln_mix_b[layer])
        else:
            c_width = c_w_out.shape[1]
            w_in = c_w_in[j]
            h, h_packed = _odd_layer(h, h_bf, bsz, seq, w_in[:, :3 * c_width].astype(BF16),
                                 w_in[:, 3 * c_width:], c_b_f[j], c_w_out[j].astype(BF16),
                                 ln_mix_g[layer], ln_mix_b[layer])
        h, h_bf = _moe(h, h_packed, router_w[layer], router_bias[layer], exp_w_gate, exp_w_up, exp_w_down, layer,
                       sh_w_gate[layer].astype(BF16), sh_w_up[layer].astype(BF16),
                       sh_w_down[layer].astype(BF16), ln_ffn_g[layer], ln_ffn_b[layer])
    return h.reshape(bsz, seq, d)
```

```python
import functools

import numpy as np
import jax
import jax.numpy as jnp
from jax import lax
from jax.experimental import pallas as pl
from jax.experimental.pallas import tpu as pltpu

F32 = jnp.float32
BF16 = jnp.bfloat16

HEAD_DIM = 128
A_PATTERNS = ((128, 1), (512, 4), (2048, 16))
A_BLOCK = 128
REL_BUCKETS = 32
REL_MAX_DIST = 2048
N_EXPERTS = 64
TOP_K = 8
N_GROUPS = 8
GROUP_SIZE = N_EXPERTS // N_GROUPS
TOPK_GROUPS = 4
ROUTED_SCALE = 2.5
LN_EPS = 1e-5
RMS_EPS = 1e-6
DEPTH = 2
DN_ALPHA = (2 * DEPTH) ** 0.25
NEG_INF = -1e30
LOG2_E = 1.4426950408889634

DILATED_UNROLL = 16
HGRN_CHUNK = 128
HGRN_BCAST_MIN = 8
FOX_Q_TILE = 512
FOX_K_TILE = 256
ROUTER_TILE = 512
EXPERT_TILE = 512
COMBINE_TILE = 128
VMEM_LIMIT = 56 * 1024 * 1024

_NT = (((1,), (1,)), ((), ()))
_TN = (((0,), (0,)), ((), ()))


def _dot(a, b):
    return jnp.dot(a, b, preferred_element_type=F32)


def _dot_nt(a, b):
    return lax.dot_general(a, b, _NT, preferred_element_type=F32)


def _dot_tn(a, b):
    return lax.dot_general(a, b, _TN, preferred_element_type=F32)


def _sigmoid(x):
    return 1.0 / (1.0 + jnp.exp(-x))


def _silu(x):
    return x * _sigmoid(x)


def _split3(x, axis):
    hi = x.astype(BF16)
    r1 = x - hi.astype(F32)
    mid = r1.astype(BF16)
    lo = (r1 - mid.astype(F32)).astype(BF16)
    return jnp.concatenate([hi, mid, lo], axis=axis)


def _layer_norm_rows(y, g, b):
    mu = jnp.mean(y, axis=-1, keepdims=True)
    yc = y - mu
    var = jnp.mean(yc * yc, axis=-1, keepdims=True)
    return yc * lax.rsqrt(var + LN_EPS) * g + b


def _params(*sem):
    return pltpu.CompilerParams(dimension_semantics=sem, vmem_limit_bytes=VMEM_LIMIT)


def _mm_kernel(x_ref, w_ref, cs_ref, o_ref):
    o_ref[...] = (_dot(x_ref[...], w_ref[...]) * cs_ref[...]).astype(o_ref.dtype)


def _matmul(x, w, col_scale, out_dtype, tm=1024, tn=1024):
    m, k = x.shape
    nw = w.shape[1]
    tm, tn = min(tm, m), min(tn, nw)
    assert m % tm == 0 and nw % tn == 0
    return pl.pallas_call(
        _mm_kernel,
        grid=(nw // tn, m // tm),
        in_specs=[pl.BlockSpec((tm, k), lambda j, i: (i, 0)),
                  pl.BlockSpec((k, tn), lambda j, i: (0, j)),
                  pl.BlockSpec((1, tn), lambda j, i: (0, j))],
        out_specs=pl.BlockSpec((tm, tn), lambda j, i: (i, j)),
        out_shape=jax.ShapeDtypeStruct((m, nw), out_dtype),
        compiler_params=_params("parallel", "parallel"),
        name="proj_matmul",
    )(x, w, col_scale)


def _proj_ln_kernel(*refs, n_in):
    a_refs, w_refs = refs[:n_in], refs[n_in:2 * n_in]
    r_ref, g_ref, b_ref, o_ref, obf_ref = refs[2 * n_in:]
    acc = _dot(a_refs[0][...], w_refs[0][...])
    for a_ref, w_ref in zip(a_refs[1:], w_refs[1:]):
        acc = acc + _dot(a_ref[...], w_ref[...])
    y = _layer_norm_rows(DN_ALPHA * r_ref[...] + acc, g_ref[...], b_ref[...])
    o_ref[...] = y
    half = y.shape[1] // 2
    obf_ref[...] = _pack_bf16_pair(y[:, :half], y[:, half:])


def _pack_bf16_pair(lo, hi):
    lo_bits = lax.shift_right_logical(lax.bitcast_convert_type(lo.astype(BF16).astype(F32), jnp.int32), 16)
    hi_bits = lax.bitcast_convert_type(hi.astype(BF16).astype(F32), jnp.int32) & jnp.int32(-65536)
    return lo_bits | hi_bits


def _unpack_bf16_pair(words):
    lo = lax.bitcast_convert_type(lax.shift_left(words, 16), F32)
    hi = lax.bitcast_convert_type(words & jnp.int32(-65536), F32)
    return lo, hi


def _proj_residual_ln(acts, weights, resid, g, b, tm=256):
    m, d = resid.shape
    tm = min(tm, m)
    n_in = len(acts)
    in_specs = [pl.BlockSpec((tm, a.shape[1]), lambda i: (i, 0)) for a in acts]
    in_specs += [pl.BlockSpec(w.shape, lambda i: (0, 0)) for w in weights]
    in_specs += [pl.BlockSpec((tm, d), lambda i: (i, 0)),
                 pl.BlockSpec((1, d), lambda i: (0, 0)),
                 pl.BlockSpec((1, d), lambda i: (0, 0))]
    return pl.pallas_call(
        functools.partial(_proj_ln_kernel, n_in=n_in),
        grid=(m // tm,),
        in_specs=in_specs,
        out_specs=[pl.BlockSpec((tm, d), lambda i: (i, 0)), pl.BlockSpec((tm, d // 2), lambda i: (i, 0))],
        out_shape=[jax.ShapeDtypeStruct((m, d), F32), jax.ShapeDtypeStruct((m, d // 2), jnp.int32)],
        compiler_params=_params("parallel"),
        name="out_proj_ln",
    )(*acts, *weights, resid, g.reshape(1, d), b.reshape(1, d))


def _t5_bucket(dist):
    max_exact = REL_BUCKETS // 2
    d = np.maximum(dist, 1).astype(np.float64)
    large = max_exact + (np.log(d / max_exact) / np.log(REL_MAX_DIST / max_exact)
                         * (REL_BUCKETS - max_exact)).astype(np.int64)
    large = np.minimum(large, REL_BUCKETS - 1)
    return np.where(dist < max_exact, dist, large).astype(np.int32)


def _dilated_kernel(q_ref, k_ref, v_ref, bucket_ref, rel_bias_ref, out_ref, o_scr, lse_scr):
    blk = A_BLOCK
    seq = q_ref.shape[1]
    head = pl.program_id(1)
    qi = lax.broadcasted_iota(jnp.int32, (blk, 2 * blk), 0)
    ki = lax.broadcasted_iota(jnp.int32, (blk, 2 * blk), 1)
    rel = qi + blk - ki

    for bi, (window, dil) in enumerate(A_PATTERNS):
        band = window // dil
        n_blocks = seq // (dil * blk)
        valid = jnp.abs(2 * rel - band) <= band
        bucket = bucket_ref[bi]
        bias = jnp.zeros((blk, 2 * blk), F32)
        for b in range(REL_BUCKETS):
            bias = jnp.where(bucket == b, rel_bias_ref[head, b], bias)

        def attend(q_start, k_start, n_keys, bias_blk, valid_blk, bi=bi, dil=dil):
            q_rows = pl.ds(q_start, blk, stride=dil)
            k_rows = pl.ds(k_start, n_keys, stride=dil)
            q = q_ref[0, q_rows, :].astype(BF16)
            kk = k_ref[0, k_rows, :].astype(BF16)
            vv = v_ref[0, k_rows, :].astype(BF16)
            s = _dot_nt(q, kk) + bias_blk
            s = jnp.where(valid_blk, s, NEG_INF)
            m = jnp.max(s, axis=-1, keepdims=True)
            p = jnp.exp(s - m)
            l = jnp.sum(p, axis=-1, keepdims=True)
            o_scr[bi, q_rows, :] = _dot(p.astype(BF16), vv) / l
            lse_scr[bi, q_rows, :] = jnp.broadcast_to(m + jnp.log(l), (blk, HEAD_DIM))

        def first(r, carry, attend=attend, bias=bias, valid=valid):
            attend(r, r, blk, bias[:, blk:], valid[:, blk:])
            return carry

        lax.fori_loop(0, dil, first, 0, unroll=min(dil, DILATED_UNROLL))

        def rest(t, carry, attend=attend, bias=bias, valid=valid, dil=dil):
            r = t & (dil - 1)
            n = (t >> (dil.bit_length() - 1)) + 1
            attend(r + n * blk * dil, r + (n - 1) * blk * dil, 2 * blk, bias, valid)
            return carry

        lax.fori_loop(0, dil * (n_blocks - 1), rest, 0, unroll=DILATED_UNROLL)

    chunk = 2 * blk
    nb = len(A_PATTERNS)

    def mix(c, carry):
        rows = pl.ds(pl.multiple_of(c * chunk, chunk), chunk)
        lses = [lse_scr[b, rows, :] for b in range(nb)]
        m = functools.reduce(jnp.maximum, lses)
        es = [jnp.exp(x - m) for x in lses]
        den = functools.reduce(lambda a, b: a + b, es)
        num = functools.reduce(lambda a, b: a + b, [e * o_scr[b, rows, :] for b, e in enumerate(es)])
        out_ref[0, rows, :] = (num / den).astype(out_ref.dtype)
        return carry

    lax.fori_loop(0, seq // chunk, mix, 0)


def _dilated_bucket_tables():
    qi = np.arange(A_BLOCK)[:, None]
    ki = np.arange(2 * A_BLOCK)[None, :]
    rel = qi + A_BLOCK - ki
    return np.stack([_t5_bucket(np.clip(rel, 0, None) * dilation) for _, dilation in A_PATTERNS], axis=0)


def _dilated_mixture(proj, rel_bias, bsz, seq, heads, proj_width):
    for window, dilation in A_PATTERNS:
        assert seq % (dilation * A_BLOCK) == 0 and window // dilation <= A_BLOCK
    nb = len(A_PATTERNS)
    view = proj.reshape(bsz, seq, proj_width)
    buckets = jnp.asarray(_dilated_bucket_tables())

    def in_spec(group):
        return pl.BlockSpec((1, seq, HEAD_DIM), lambda b, h, g=group: (b, 0, g * heads + h))

    o = pl.pallas_call(
        _dilated_kernel,
        grid=(bsz, heads),
        in_specs=[in_spec(0), in_spec(1), in_spec(2),
                  pl.BlockSpec(buckets.shape, lambda b, h: (0, 0, 0)),
                  pl.BlockSpec(memory_space=pltpu.SMEM)],
        out_specs=pl.BlockSpec((1, seq, HEAD_DIM), lambda b, h: (b, 0, h)),
        out_shape=jax.ShapeDtypeStruct((bsz, seq, heads * HEAD_DIM), BF16),
        scratch_shapes=[pltpu.VMEM((nb, seq, HEAD_DIM), F32), pltpu.VMEM((nb, seq, HEAD_DIM), F32)],
        compiler_params=_params("parallel", "parallel"),
        name="dilated_attention",
    )(view, view, view, buckets, rel_bias.astype(F32).T)
    return o.reshape(bsz * seq, heads * HEAD_DIM)


def _hgrn_levels(t):
    cs = []
    c = t // 2
    while c >= 1:
        cs.append(c)
        c //= 2
    return cs


def _hgrn_consts(t):
    idx = np.arange(t)
    ltri = (idx[:, None] >= idx[None, :]).astype(np.float32)
    sels = []
    for c in _hgrn_levels(t):
        if c < HGRN_BCAST_MIN:
            ref = (idx // (2 * c)) * (2 * c) + c - 1
            sels.append((idx[None, :] == ref[:, None]).astype(np.float32))
    return jnp.asarray(ltri, BF16), jnp.asarray(np.concatenate(sels, axis=0), BF16)


def _hgrn_kernel(q_ref, f_ref, i_ref, g_ref, lb_ref, gn_ref, ltri_ref, sel_ref, o_ref, state_ref,
                 *, n_chunks):
    t = HGRN_CHUNK
    dk = HEAD_DIM
    levels = _hgrn_levels(t)
    lb = lb_ref[0]
    gn = gn_ref[0]
    ltri = ltri_ref[...]
    sel = sel_ref[...]
    row = lax.broadcasted_iota(jnp.int32, (t, dk), 0)
    ti = lax.broadcasted_iota(jnp.int32, (t, t), 0)
    si = lax.broadcasted_iota(jnp.int32, (t, t), 1)
    state_ref[...] = jnp.zeros_like(state_ref)

    def sum3(y):
        return y[:, :dk] + y[:, dk:2 * dk] + y[:, 2 * dk:]

    def chunk(n, carry):
        st = pl.multiple_of(n * t, t)
        fq = _silu(q_ref[0, pl.ds(st, t), :])
        forget = lb + (1.0 - lb) * _sigmoid(f_ref[0, pl.ds(st, t), :])
        logf = jnp.log(forget)
        key = 1.0 - forget
        val = i_ref[0, pl.ds(st, t), :].astype(BF16)
        a = sum3(_dot(ltri, _split3(logf, 1)))
        aref_fine = sum3(_dot(sel, _split3(a, 1)))
        fq_b = fq.astype(BF16)
        key_b = key.astype(BF16)
        scores = jnp.where(ti == si, _dot_nt(fq_b, key_b), 0.0)
        n_fine = 0
        for c in levels:
            if c >= HGRN_BCAST_MIN:
                aref = jnp.concatenate(
                    [jnp.broadcast_to(a[b0 + c - 1:b0 + c, :], (2 * c, dk)) for b0 in range(0, t, 2 * c)],
                    axis=0)
            else:
                aref = aref_fine[n_fine * t:(n_fine + 1) * t, :]
                n_fine += 1
            upper = (row & (2 * c - 1)) >= c
            decayed = jnp.where(upper, fq, key) * jnp.exp(-jnp.abs(a - aref))
            ql = jnp.where(upper, decayed, 0.0).astype(BF16)
            kl = jnp.where(upper, 0.0, decayed).astype(BF16)
            s_l = _dot_nt(ql, kl)
            if 2 * c < t:
                sh = (2 * c).bit_length() - 1
                s_l = jnp.where((ti >> sh) == (si >> sh), s_l, 0.0)
            scores = scores + s_l
        state_t = state_ref[...]
        inter = _dot_nt((fq * jnp.exp(a)).astype(BF16), state_t.astype(BF16))
        o = inter + _dot(scores.astype(BF16), val)
        a_last = a[t - 1:t, :]
        k_dec = (key * jnp.exp(a_last - a)).astype(BF16)
        state_ref[...] = state_t * jnp.exp(a_last) + _dot_tn(val, k_dec)
        o = o * lax.rsqrt(jnp.mean(o * o, axis=-1, keepdims=True) + RMS_EPS) * gn
        o = o * _silu(g_ref[0, pl.ds(st, t), :])
        o_ref[0, pl.ds(st, t), :] = o.astype(o_ref.dtype)
        return carry

    lax.fori_loop(0, n_chunks, chunk, 0, unroll=8)


def _hgrn2(proj, lb, gn_g, bsz, seq, heads, proj_width, first_group):
    assert seq % HGRN_CHUNK == 0
    view = proj.reshape(bsz, seq, proj_width)
    ltri, sel = _hgrn_consts(HGRN_CHUNK)

    def in_spec(group):
        return pl.BlockSpec((1, seq, HEAD_DIM), lambda b, h, g=group: (b, 0, (first_group + g) * heads + h))

    head_vec = pl.BlockSpec((1, 1, HEAD_DIM), lambda b, h: (h, 0, 0))
    o = pl.pallas_call(
        functools.partial(_hgrn_kernel, n_chunks=seq // HGRN_CHUNK),
        grid=(bsz, heads),
        in_specs=[in_spec(0), in_spec(1), in_spec(2), in_spec(3), head_vec, head_vec,
                  pl.BlockSpec(ltri.shape, lambda b, h: (0, 0)),
                  pl.BlockSpec(sel.shape, lambda b, h: (0, 0))],
        out_specs=pl.BlockSpec((1, seq, HEAD_DIM), lambda b, h: (b, 0, h)),
        out_shape=jax.ShapeDtypeStruct((bsz, seq, heads * HEAD_DIM), BF16),
        scratch_shapes=[pltpu.VMEM((HEAD_DIM, HEAD_DIM), F32)],
        compiler_params=_params("parallel", "parallel"),
        name="hgrn2",
    )(view, view, view, view, lb.reshape(heads, 1, HEAD_DIM).astype(F32),
      gn_g.reshape(heads, 1, HEAD_DIM).astype(F32), ltri, sel)
    return o.reshape(bsz * seq, heads * HEAD_DIM)


def _fox_gate_kernel(x_ref, w_ref, b_ref, ltri_ref, place_q_ref, place_k_ref, ones_q_ref, ones_k_ref,
                     qx_ref, kx_ref, carry):
    heads = w_ref.shape[1]
    ts = x_ref.shape[1]

    @pl.when(pl.program_id(1) == 0)
    def _():
        carry[...] = jnp.zeros_like(carry)

    def log_sigmoid(z):
        return jnp.minimum(z, 0.0) - jnp.log(1.0 + jnp.exp(-jnp.abs(z)))

    x = x_ref[0]
    xh = x.astype(BF16)
    xl = (x - xh.astype(F32)).astype(BF16)
    w = w_ref[...]
    wh = w.astype(BF16)
    wl = (w - wh.astype(F32)).astype(BF16)
    f = _dot(xh, wh) + _dot(xh, wl) + _dot(xl, wh) + b_ref[...]
    c = _dot(ltri_ref[...], _split3(log_sigmoid(f), 1))
    c = c[:, :heads] + c[:, heads:2 * heads] + c[:, 2 * heads:] + carry[...]
    carry[...] = c[ts - 1:ts, :]
    terms = _split3(c * LOG2_E, 1)
    qx_ref[0] = (_dot(terms, place_q_ref[...]) + ones_q_ref[...]).astype(BF16)
    kx_ref[0] = (_dot(terms, place_k_ref[...]) + ones_k_ref[...]).astype(BF16)


def _fox_gate_placement(heads):
    place_q = np.zeros((3 * heads, heads * HEAD_DIM), np.float32)
    place_k = np.zeros((3 * heads, heads * HEAD_DIM), np.float32)
    ones_q = np.zeros((1, heads * HEAD_DIM), np.float32)
    ones_k = np.zeros((1, heads * HEAD_DIM), np.float32)
    for h in range(heads):
        for j in range(3):
            place_q[j * heads + h, h * HEAD_DIM + j] = 1.0
            place_k[j * heads + h, h * HEAD_DIM + 3 + j] = -1.0
            ones_q[0, h * HEAD_DIM + 3 + j] = 1.0
            ones_k[0, h * HEAD_DIM + j] = 1.0
    return jnp.asarray(place_q, BF16), jnp.asarray(place_k, BF16), jnp.asarray(ones_q), jnp.asarray(ones_k)


def _fox_gate_terms(x, w_f, b_f, bsz, seq, ts=512):
    d = x.shape[-1]
    heads = w_f.shape[1]
    ts = min(ts, seq)
    idx = np.arange(ts)
    ltri = jnp.asarray(idx[:, None] >= idx[None, :], BF16)
    consts = _fox_gate_placement(heads)
    const = lambda shape: pl.BlockSpec(shape, lambda b, i: (0,) * len(shape))
    out_spec = pl.BlockSpec((1, ts, heads * HEAD_DIM), lambda b, i: (b, i, 0))
    out_sds = jax.ShapeDtypeStruct((bsz, seq, heads * HEAD_DIM), BF16)
    return pl.pallas_call(
        _fox_gate_kernel,
        grid=(bsz, seq // ts),
        in_specs=[pl.BlockSpec((1, ts, d), lambda b, i: (b, i, 0)),
                  const((d, heads)), const((1, heads)), const((ts, ts))] + [const(a.shape) for a in consts],
        out_specs=[out_spec, out_spec],
        out_shape=[out_sds, out_sds],
        scratch_shapes=[pltpu.VMEM((1, heads), F32)],
        compiler_params=_params("parallel", "arbitrary"),
        name="fox_gate_cumsum",
    )(x.reshape(bsz, seq, d), w_f.astype(F32), b_f.astype(F32).reshape(1, heads), ltri, *consts)


def _fox_kernel(q_ref, qx_ref, k_ref, kx_ref, vt_ref, o_ref, s_a, s_b, p_a, p_b, acc_ref):
    tq, tk = FOX_Q_TILE, FOX_K_TILE
    lanes = HEAD_DIM
    i = pl.program_id(2)
    q = jnp.concatenate([q_ref[0], qx_ref[0]], axis=1)

    def logits_into(j, s_ref):
        st = pl.multiple_of(j * tk, tk)
        kk = jnp.concatenate([k_ref[0, pl.ds(st, tk), :], kx_ref[0, pl.ds(st, tk), :]], axis=1)
        s_ref[...] = _dot_nt(kk, q)

    def softmax_pv(j, s_ref, p_ref, m, l, diag):
        m_out, l_out, alphas = [], [], []
        for c in range(tq // lanes):
            cols = slice(c * lanes, (c + 1) * lanes)
            if diag is not None and (c + 1) * lanes - 1 < diag * tk:
                m_out.append(m[:, cols])
                l_out.append(l[:, cols])
                alphas.append(jnp.ones((1, lanes), F32))
                p_ref[:, cols] = jnp.zeros((tk, lanes), BF16)
                continue
            s = s_ref[:, cols]
            if diag is not None and c * lanes < diag * tk + tk - 1:
                key = lax.broadcasted_iota(jnp.int32, (tk, lanes), 0) + diag * tk
                qry = lax.broadcasted_iota(jnp.int32, (tk, lanes), 1) + c * lanes
                s = jnp.where(key <= qry, s, NEG_INF)
            m_new = jnp.maximum(m[:, cols], jnp.max(s, axis=0, keepdims=True))
            alpha = jnp.exp2(m[:, cols] - m_new)
            p = jnp.exp2(s - m_new)
            l_out.append(alpha * l[:, cols] + jnp.sum(p, axis=0, keepdims=True))
            m_out.append(m_new)
            alphas.append(alpha)
            p_ref[:, cols] = p.astype(BF16)
        vt = vt_ref[0, :, pl.ds(pl.multiple_of(j * tk, tk), tk)]
        acc_ref[...] = jnp.concatenate(alphas, axis=1) * acc_ref[...] + _dot(vt, p_ref[...])
        return jnp.concatenate(m_out, axis=1), jnp.concatenate(l_out, axis=1)

    def pair(jp, carry):
        m, l = carry
        t0 = 2 * jp
        logits_into(t0 + 1, s_b)
        m, l = softmax_pv(t0, s_a, p_a, m, l, None)
        logits_into(t0 + 2, s_a)
        return softmax_pv(t0 + 1, s_b, p_b, m, l, None)

    assert tq == 2 * tk
    acc_ref[...] = jnp.zeros_like(acc_ref)
    logits_into(0, s_a)
    m, l = lax.fori_loop(0, i, pair, (jnp.full((1, tq), NEG_INF, F32), jnp.zeros((1, tq), F32)))
    t0 = 2 * i
    logits_into(t0 + 1, s_b)
    m, l = softmax_pv(t0, s_a, p_a, m, l, 0)
    m, l = softmax_pv(t0 + 1, s_b, p_b, m, l, 1)
    o_ref[0] = jnp.transpose(acc_ref[...] / l).astype(o_ref.dtype)


def _fox_attention(qkv, qx, kx, bsz, seq, heads):
    tq = FOX_Q_TILE
    assert seq % tq == 0 and tq % FOX_K_TILE == 0
    width = heads * HEAD_DIM
    view = qkv.reshape(bsz, seq, 3 * width)
    v_t = jnp.swapaxes(view[:, :, 2 * width:], 1, 2)
    q_spec = lambda off: pl.BlockSpec((1, tq, HEAD_DIM), lambda b, h, i: (b, i, off + h))
    kv_spec = lambda off: pl.BlockSpec((1, seq, HEAD_DIM), lambda b, h, i: (b, 0, off + h))
    o = pl.pallas_call(
        _fox_kernel,
        grid=(bsz, heads, seq // tq),
        in_specs=[q_spec(0), q_spec(0), kv_spec(heads), kv_spec(0),
                  pl.BlockSpec((1, HEAD_DIM, seq), lambda b, h, i: (b, h, 0))],
        out_specs=pl.BlockSpec((1, tq, HEAD_DIM), lambda b, h, i: (b, i, h)),
        out_shape=jax.ShapeDtypeStruct((bsz, seq, width), BF16),
        scratch_shapes=[pltpu.VMEM((FOX_K_TILE, tq), F32), pltpu.VMEM((FOX_K_TILE, tq), F32),
                        pltpu.VMEM((FOX_K_TILE, tq), BF16), pltpu.VMEM((FOX_K_TILE, tq), BF16),
                        pltpu.VMEM((HEAD_DIM, tq), F32)],
        compiler_params=_params("parallel", "parallel", "parallel"),
        name="fox_attention",
    )(view, qx, view, kx, v_t)
    return o.reshape(bsz * seq, width)


def _router_kernel(x_ref, wt_ref, bias_ref, utri_ref, eidx_ref, rank_ref, w_ref, cnt_ref, carry):
    gsz = GROUP_SIZE
    tt = x_ref.shape[0]

    @pl.when(pl.program_id(0) == 0)
    def _():
        carry[...] = jnp.zeros_like(carry)

    x = x_ref[...]
    xh = x.astype(BF16)
    xl = (x - xh.astype(F32)).astype(BF16)
    wt = wt_ref[...]
    wh = wt.astype(BF16)
    wl = (wt - wh.astype(F32)).astype(BF16)
    logits = _dot_nt(wh, xh) + _dot_nt(wh, xl) + _dot_nt(wl, xh)
    scores = _sigmoid(logits)
    sel = scores + bias_ref[...]
    sub = lax.broadcasted_iota(jnp.int32, (gsz, tt), 0)
    neg = -jnp.inf

    group_score = []
    for g in range(N_GROUPS):
        blk = sel[g * gsz:(g + 1) * gsz, :]
        m1 = jnp.max(blk, axis=0, keepdims=True)
        first = jnp.min(jnp.where(blk == m1, sub, gsz), axis=0, keepdims=True)
        m2 = jnp.max(jnp.where(sub == first, neg, blk), axis=0, keepdims=True)
        group_score.append(m1 + m2)

    chosen = [jnp.zeros((1, tt), F32) for _ in range(N_GROUPS)]
    for _ in range(TOPK_GROUPS):
        m = functools.reduce(jnp.maximum, group_score)
        gi = jnp.full((1, tt), N_GROUPS, jnp.int32)
        for g in reversed(range(N_GROUPS)):
            gi = jnp.where(group_score[g] == m, g, gi)
        for g in range(N_GROUPS):
            hit = gi == g
            chosen[g] = jnp.where(hit, 1.0, chosen[g])
            group_score[g] = jnp.where(hit, neg, group_score[g])

    cand = [jnp.where(jnp.broadcast_to(chosen[g], (gsz, tt)) > 0.5, sel[g * gsz:(g + 1) * gsz, :], NEG_INF)
            for g in range(N_GROUPS)]
    score_g = [scores[g * gsz:(g + 1) * gsz, :] for g in range(N_GROUPS)]
    eiota = [sub + g * gsz for g in range(N_GROUPS)]
    picked = [jnp.zeros((gsz, tt), F32) for _ in range(N_GROUPS)]
    top_idx, top_score = [], []
    for _ in range(TOP_K):
        m = functools.reduce(jnp.maximum, [jnp.max(c, axis=0, keepdims=True) for c in cand])
        ei = functools.reduce(jnp.minimum, [
            jnp.min(jnp.where(cand[g] == m, eiota[g], N_EXPERTS), axis=0, keepdims=True)
            for g in range(N_GROUPS)])
        sc = jnp.zeros((1, tt), F32)
        for g in range(N_GROUPS):
            hit = eiota[g] == ei
            picked[g] = jnp.where(hit, 1.0, picked[g])
            cand[g] = jnp.where(hit, neg, cand[g])
            sc = sc + jnp.sum(jnp.where(hit, score_g[g], 0.0), axis=0, keepdims=True)
        top_idx.append(ei)
        top_score.append(sc)

    total = functools.reduce(lambda a, b: a + b, top_score)
    mask = jnp.concatenate(picked, axis=0)
    incl = _dot(mask.astype(BF16), utri_ref[...])
    rank = carry[...] + incl - mask
    for r in range(TOP_K):
        ei = top_idx[r]
        rk = jnp.zeros((1, tt), F32)
        for g in range(N_GROUPS):
            rk = rk + jnp.sum(jnp.where(eiota[g] == ei, rank[g * gsz:(g + 1) * gsz, :], 0.0),
                              axis=0, keepdims=True)
        eidx_ref[r:r + 1, :] = ei
        rank_ref[r:r + 1, :] = rk.astype(jnp.int32)
        w_ref[r:r + 1, :] = top_score[r] / total * ROUTED_SCALE
    new_carry = carry[...] + incl[:, tt - 1:tt]
    carry[...] = new_carry
    cnt_ref[...] = jnp.broadcast_to(new_carry, cnt_ref.shape)


def _router(x, router_w, router_bias):
    n, d = x.shape
    tt = min(ROUTER_TILE, n)
    idx = np.arange(tt)
    utri = jnp.asarray(idx[:, None] <= idx[None, :], BF16)
    eidx, rank, w8, cnt = pl.pallas_call(
        _router_kernel,
        grid=(n // tt,),
        in_specs=[pl.BlockSpec((tt, d), lambda i: (i, 0)),
                  pl.BlockSpec((N_EXPERTS, d), lambda i: (0, 0)),
                  pl.BlockSpec((N_EXPERTS, 1), lambda i: (0, 0)),
                  pl.BlockSpec((tt, tt), lambda i: (0, 0))],
        out_specs=[pl.BlockSpec((TOP_K, tt), lambda i: (0, i)),
                   pl.BlockSpec((TOP_K, tt), lambda i: (0, i)),
                   pl.BlockSpec((TOP_K, tt), lambda i: (0, i)),
                   pl.BlockSpec((N_EXPERTS, 128), lambda i: (0, 0))],
        out_shape=[jax.ShapeDtypeStruct((TOP_K, n), jnp.int32),
                   jax.ShapeDtypeStruct((TOP_K, n), jnp.int32),
                   jax.ShapeDtypeStruct((TOP_K, n), F32),
                   jax.ShapeDtypeStruct((N_EXPERTS, 128), F32)],
        scratch_shapes=[pltpu.VMEM((N_EXPERTS, 1), F32)],
        compiler_params=_params("arbitrary"),
        name="moe_router",
    )(x, router_w.astype(F32).T, router_bias.astype(F32).reshape(N_EXPERTS, 1), utri)
    return eidx, rank, w8, cnt[:, 0].astype(jnp.int32)


def _expert_kernel(te_ref, tv_ref, nt_ref, src_ref, x_hbm, wg_ref, wu_ref, wd_ref, y_ref,
                   xbuf_a, xbuf_b, sem_a, sem_b, wg_bf, wu_bf, wd_bf):
    i = pl.program_id(0)
    n_live = nt_ref[0]
    tm, half = xbuf_a.shape
    bufs = ((xbuf_a, sem_a), (xbuf_b, sem_b))
    prev = jnp.maximum(i - 1, 0)

    def copy(buf, sem, r):
        return pltpu.make_async_copy(x_hbm.at[pl.ds(src_ref[0, 0, r], 1), :], buf.at[pl.ds(r, 1), :], sem)

    def ffn(buf, sem, gather=None):
        pltpu.make_async_copy(buf, buf, sem).wait()
        if gather is not None:
            for r in range(tm):
                copy(*gather, r).start()
        row = lax.broadcasted_iota(jnp.int32, (tm, 1), 0)
        x_lo, x_hi = _unpack_bf16_pair(jnp.where(row < tv_ref[prev], buf[...], 0))
        x_lo, x_hi = x_lo.astype(BF16), x_hi.astype(BF16)
        gate = _dot(x_lo, wg_bf[:half, :]) + _dot(x_hi, wg_bf[half:, :])
        up = _dot(x_lo, wu_bf[:half, :]) + _dot(x_hi, wu_bf[half:, :])
        hidden = (_silu(gate) * up).astype(BF16)
        y = _dot(hidden, wd_bf[...])
        y_ref[...] = _pack_bf16_pair(y[:, :half], y[:, half:])

    @pl.when(i == 0)
    def _():
        def issue(r, carry):
            copy(*bufs[0], r).start()
            return carry
        lax.fori_loop(0, tm, issue, 0)

    computing = jnp.logical_and(i >= 1, i <= n_live)
    new_expert = jnp.logical_or(i == 1, te_ref[prev] != te_ref[jnp.maximum(i - 2, 0)])

    @pl.when(jnp.logical_and(computing, new_expert))
    def _():
        wg_bf[...] = wg_ref[0, 0].astype(BF16)
        wu_bf[...] = wu_ref[0, 0].astype(BF16)
        wd_bf[...] = wd_ref[0, 0].astype(BF16)

    for parity in range(2):
        @pl.when(jnp.logical_and(jnp.logical_and(i >= 1, i < n_live), i % 2 == parity))
        def _(parity=parity):
            ffn(*bufs[1 - parity], gather=bufs[parity])

        @pl.when(jnp.logical_and(i == n_live, (i - 1) % 2 == parity))
        def _(parity=parity):
            ffn(*bufs[parity])


def _expert_ffn(x_packed, src_table, w_gate, w_up, w_down, layer, tile_expert, tile_valid, num_tiles):
    n, half = x_packed.shape
    d = 2 * half
    max_tiles, _, tm = src_table.shape
    e_dim = w_gate.shape[3]
    prev = lambda i, nt: jnp.minimum(jnp.maximum(i - 1, 0), nt[0] - 1)
    w_map = lambda i, te, tv, nt: (layer, te[prev(i, nt)], 0, 0)
    xbuf = pltpu.VMEM((tm, half), jnp.int32)
    return pl.pallas_call(
        _expert_kernel,
        grid_spec=pltpu.PrefetchScalarGridSpec(
            num_scalar_prefetch=3,
            grid=(max_tiles + 1,),
            in_specs=[pl.BlockSpec((1, 1, tm), lambda i, te, tv, nt: (jnp.minimum(i, nt[0] - 1), 0, 0),
                                   memory_space=pltpu.SMEM),
                      pl.BlockSpec(memory_space=pl.ANY),
                      pl.BlockSpec((1, 1, d, e_dim), w_map),
                      pl.BlockSpec((1, 1, d, e_dim), w_map),
                      pl.BlockSpec((1, 1, e_dim, d), w_map)],
            out_specs=pl.BlockSpec((tm, half), lambda i, te, tv, nt: (prev(i, nt), 0)),
            scratch_shapes=[xbuf, xbuf, pltpu.SemaphoreType.DMA(()), pltpu.SemaphoreType.DMA(()),
                            pltpu.VMEM((d, e_dim), BF16), pltpu.VMEM((d, e_dim), BF16),
                            pltpu.VMEM((e_dim, d), BF16)]),
        out_shape=jax.ShapeDtypeStruct((max_tiles * tm, half), jnp.int32),
        compiler_params=_params("arbitrary"),
        name="moe_expert_ffn",
    )(tile_expert, tile_valid, num_tiles, src_table, x_packed, w_gate, w_up, w_down)


def _combine_kernel(dest_ref, x_ref, w8_ref, sg_ref, su_ref, sd_ref, g_ref, b_ref, y_hbm,
                    o_ref, obf_ref, ybuf_a, ybuf_b, sem_a, sem_b):
    tc = x_ref.shape[0]
    i = pl.program_id(0)
    n_tiles = pl.num_programs(0) - 1
    bufs = ((ybuf_a, sem_a), (ybuf_b, sem_b))

    def copy(buf, sem, t, k):
        return pltpu.make_async_copy(y_hbm.at[pl.ds(dest_ref[0, k, t], 1), :],
                                     buf.at[k, pl.ds(t, 1), :], sem)

    def finish(buf, sem, gather=None):
        x = x_ref[...]
        xb = x.astype(BF16)
        hidden = (_silu(_dot(xb, sg_ref[...])) * _dot(xb, su_ref[...])).astype(BF16)
        acc = _dot(hidden, sd_ref[...])
        pltpu.make_async_copy(buf, buf, sem).wait()
        if gather is not None:
            for t in range(tc):
                for k in range(TOP_K):
                    copy(*gather, t, k).start()
        w8 = w8_ref[...]
        half = buf.shape[2]
        routed_lo = jnp.zeros((tc, half), F32)
        routed_hi = jnp.zeros((tc, half), F32)
        for k in range(TOP_K):
            y_lo, y_hi = _unpack_bf16_pair(buf[k])
            routed_lo = routed_lo + w8[:, k:k + 1] * y_lo
            routed_hi = routed_hi + w8[:, k:k + 1] * y_hi
        acc = acc + jnp.concatenate([routed_lo, routed_hi], axis=1)
        y = _layer_norm_rows(DN_ALPHA * x + acc, g_ref[...], b_ref[...])
        o_ref[...] = y
        obf_ref[...] = y.astype(BF16)

    @pl.when(i == 0)
    def _():
        def issue(t, carry):
            for k in range(TOP_K):
                copy(*bufs[0], t, k).start()
            return carry
        lax.fori_loop(0, tc, issue, 0)

    for parity in range(2):
        @pl.when(jnp.logical_and(jnp.logical_and(i >= 1, i < n_tiles), i % 2 == parity))
        def _(parity=parity):
            finish(*bufs[1 - parity], gather=bufs[parity])

        @pl.when(jnp.logical_and(i == n_tiles, (i - 1) % 2 == parity))
        def _(parity=parity):
            finish(*bufs[parity])


def _combine(x, w8_rows, dest_tiles, y_sorted, s_gate, s_up, s_down, g, b):
    n, d = x.shape
    tc = dest_tiles.shape[2]
    sdim = s_gate.shape[1]
    const = lambda shape: pl.BlockSpec(shape, lambda i: (0,) * len(shape))
    n_tiles = n // tc
    prev = lambda i: (jnp.maximum(i - 1, 0), 0)
    row = pl.BlockSpec((tc, d), prev)
    ybuf = pltpu.VMEM((TOP_K, tc, d // 2), jnp.int32)
    return pl.pallas_call(
        _combine_kernel,
        grid=(n_tiles + 1,),
        in_specs=[pl.BlockSpec((1, TOP_K, tc), lambda i: (jnp.minimum(i, n_tiles - 1), 0, 0),
                               memory_space=pltpu.SMEM),
                  row, pl.BlockSpec((tc, TOP_K), prev),
                  const((d, sdim)), const((d, sdim)), const((sdim, d)), const((1, d)), const((1, d)),
                  pl.BlockSpec(memory_space=pl.ANY)],
        out_specs=[row, row],
        out_shape=[jax.ShapeDtypeStruct((n, d), F32), jax.ShapeDtypeStruct((n, d), BF16)],
        scratch_shapes=[ybuf, ybuf, pltpu.SemaphoreType.DMA(()), pltpu.SemaphoreType.DMA(())],
        compiler_params=_params("arbitrary"),
        name="moe_combine",
    )(dest_tiles, x, w8_rows, s_gate, s_up, s_down, g.reshape(1, d), b.reshape(1, d), y_sorted)


def _tile_major(a, tile):
    k, n = a.shape
    return a.reshape(k, n // tile, tile).transpose(1, 0, 2)


def _moe(x, x_packed, router_w, router_bias, w_gate, w_up, w_down, layer, s_gate, s_up, s_down, ln_g, ln_b):
    n, d = x.shape
    tm = EXPERT_TILE
    eidx, rank, w8, cnt = _router(x, router_w, router_bias)
    tiles_e = (cnt + tm - 1) // tm
    tile_end = jnp.cumsum(tiles_e)
    tile_start = tile_end - tiles_e
    row_start = tile_start * tm
    e_ids = jnp.arange(N_EXPERTS, dtype=jnp.int32)
    dest = jnp.sum(jnp.where(eidx[None] == e_ids[:, None, None], row_start[:, None, None], 0), axis=0) + rank
    max_tiles = (n * TOP_K) // tm + N_EXPERTS
    num_tiles = tile_end[-1]
    tile_ids = jnp.arange(max_tiles, dtype=jnp.int32)
    t_clamped = jnp.minimum(tile_ids, num_tiles - 1)
    tile_expert = jnp.sum(t_clamped[:, None] >= tile_end[None, :], axis=1).astype(jnp.int32)
    tile_valid = jnp.clip(cnt[tile_expert] - (t_clamped - tile_start[tile_expert]) * tm, 0, tm).astype(jnp.int32)

    tokens = jnp.tile(jnp.arange(n, dtype=jnp.int32), TOP_K)
    _, tok_sorted = lax.sort_key_val(dest.reshape(-1), tokens)
    tok_sorted = jnp.concatenate([tok_sorted, jnp.zeros((tm,), jnp.int32)])
    cnt_before = jnp.cumsum(cnt) - cnt
    first_slot = cnt_before[tile_expert] + (t_clamped - tile_start[tile_expert]) * tm
    window = first_slot[:, None] + jnp.arange(tm, dtype=jnp.int32)[None, :]
    src_table = tok_sorted.at[window].get(mode="promise_in_bounds")
    ys = _expert_ffn(x_packed, src_table.reshape(max_tiles, 1, tm), w_gate, w_up, w_down, layer,
                     tile_expert, tile_valid, num_tiles.reshape(1).astype(jnp.int32))
    return _combine(x, w8.T, _tile_major(dest, min(COMBINE_TILE, n)), ys, s_gate, s_up, s_down, ln_g, ln_b)


def _even_layer(x, x_bf, bsz, seq, w_in, gn_g, w_out, rel_bias, lb, ln_g, ln_b):
    heads = w_out.shape[0] // (2 * HEAD_DIM)
    width = heads * HEAD_DIM
    proj_width = w_in.shape[1]
    col_scale = jnp.concatenate([jnp.full((1, width), HEAD_DIM ** -0.5, F32),
                                 jnp.ones((1, proj_width - width), F32)], axis=1)
    proj = _matmul(x_bf, w_in, col_scale, F32)
    o_a = _dilated_mixture(proj, rel_bias, bsz, seq, heads, proj_width)
    o_b = _hgrn2(proj, lb, gn_g, bsz, seq, heads, proj_width, first_group=3)
    return _proj_residual_ln([o_a, o_b], [w_out[:width], w_out[width:]], x, ln_g, ln_b)


def _odd_layer(x, x_bf, bsz, seq, w_qkv, w_f, b_f, w_out, ln_g, ln_b):
    heads = w_f.shape[1]
    width = heads * HEAD_DIM
    col_scale = jnp.concatenate([jnp.full((1, width), LOG2_E * HEAD_DIM ** -0.5, F32),
                                 jnp.ones((1, 2 * width), F32)], axis=1)
    qkv = _matmul(x_bf, w_qkv, col_scale, BF16)
    qx, kx = _fox_gate_terms(x, w_f, b_f, bsz, seq)
    o = _fox_attention(qkv, qx, kx, bsz, seq, heads)
    return _proj_residual_ln([o], [w_out], x, ln_g, ln_b)


def kernel(x, rel_bias, hgrn_lb_logits, a_w_in, a_gn_g, a_w_out, c_w_in, c_b_f, c_w_out, ln_mix_g, ln_mix_b, ln_ffn_g, ln_ffn_b, router_w, router_bias, exp_w_gate, exp_w_up, exp_w_down, sh_w_gate, sh_w_up, sh_w_down):
    bsz, seq, d = x.shape
    depth = ln_mix_g.shape[0]
    lb_all = jnp.cumsum(jax.nn.softmax(hgrn_lb_logits.astype(F32), axis=0), axis=0)
    h = x.reshape(bsz * seq, d).astype(F32)
    h_bf = h.astype(BF16)
    for layer in range(depth):
        j = layer // 2
        if layer % 2 == 0:
            h, h_packed = _even_layer(h, h_bf, bsz, seq, a_w_in[j].astype(BF16), a_gn_g[j],
                                  a_w_out[j].astype(BF16), rel_bias, lb_all[layer],
                                  ln_mix_g[layer], ln_mix_b[layer])
        else:
            c_width = c_w_out.shape[1]
            w_in = c_w_in[j]
            h, h_packed = _odd_layer(h, h_bf, bsz, seq, w_in[:, :3 * c_width].astype(BF16),
                                 w_in[:, 3 * c_width:], c_b_f[j], c_w_out[j].astype(BF16),
                                 ln_mix_g[layer], ln_mix_b[layer])
        h, h_bf = _moe(h, h_packed, router_w[layer], router_bias[layer], exp_w_gate, exp_w_up, exp_w_down, layer,
                       sh_w_gate[layer].astype(BF16), sh_w_up[layer].astype(BF16),
                       sh_w_down[layer].astype(BF16), ln_ffn_g[layer], ln_ffn_b[layer])
    return h.reshape(bsz, seq, d)
```

```python
import functools

import numpy as np
import jax
import jax.numpy as jnp
from jax import lax
from jax.experimental import pallas as pl
from jax.experimental.pallas import tpu as pltpu

F32 = jnp.float32
BF16 = jnp.bfloat16

HEAD_DIM = 128
A_PATTERNS = ((128, 1), (512, 4), (2048, 16))
A_BLOCK = 128
REL_BUCKETS = 32
REL_MAX_DIST = 2048
N_EXPERTS = 64
TOP_K = 8
N_GROUPS = 8
GROUP_SIZE = N_EXPERTS // N_GROUPS
TOPK_GROUPS = 4
ROUTED_SCALE = 2.5
LN_EPS = 1e-5
RMS_EPS = 1e-6
DEPTH = 2
DN_ALPHA = (2 * DEPTH) ** 0.25
NEG_INF = -1e30
LOG2_E = 1.4426950408889634

DILATED_UNROLL = 16
HGRN_CHUNK = 128
HGRN_BCAST_MIN = 8
FOX_Q_TILE = 512
FOX_K_TILE = 256
FOX_PAIR_UNROLL = 4
ROUTER_TILE = 512
EXPERT_TILE = 512
COMBINE_TILE = 128
VMEM_LIMIT = 56 * 1024 * 1024

_NT = (((1,), (1,)), ((), ()))
_TN = (((0,), (0,)), ((), ()))


def _dot(a, b):
    return jnp.dot(a, b, preferred_element_type=F32)


def _dot_nt(a, b):
    return lax.dot_general(a, b, _NT, preferred_element_type=F32)


def _dot_tn(a, b):
    return lax.dot_general(a, b, _TN, preferred_element_type=F32)


def _sigmoid(x):
    return 1.0 / (1.0 + jnp.exp(-x))


def _silu(x):
    return x * _sigmoid(x)


def _split3(x, axis):
    hi = x.astype(BF16)
    r1 = x - hi.astype(F32)
    mid = r1.astype(BF16)
    lo = (r1 - mid.astype(F32)).astype(BF16)
    return jnp.concatenate([hi, mid, lo], axis=axis)


def _layer_norm_rows(y, g, b):
    mu = jnp.mean(y, axis=-1, keepdims=True)
    yc = y - mu
    var = jnp.mean(yc * yc, axis=-1, keepdims=True)
    return yc * lax.rsqrt(var + LN_EPS) * g + b


def _params(*sem):
    return pltpu.CompilerParams(dimension_semantics=sem, vmem_limit_bytes=VMEM_LIMIT)


def _mm_kernel(x_ref, w_ref, cs_ref, o_ref):
    o_ref[...] = (_dot(x_ref[...], w_ref[...]) * cs_ref[...]).astype(o_ref.dtype)


def _matmul(x, w, col_scale, out_dtype, tm=1024, tn=1024):
    m, k = x.shape
    nw = w.shape[1]
    tm, tn = min(tm, m), min(tn, nw)
    assert m % tm == 0 and nw % tn == 0
    return pl.pallas_call(
        _mm_kernel,
        grid=(nw // tn, m // tm),
        in_specs=[pl.BlockSpec((tm, k), lambda j, i: (i, 0)),
                  pl.BlockSpec((k, tn), lambda j, i: (0, j)),
                  pl.BlockSpec((1, tn), lambda j, i: (0, j))],
        out_specs=pl.BlockSpec((tm, tn), lambda j, i: (i, j)),
        out_shape=jax.ShapeDtypeStruct((m, nw), out_dtype),
        compiler_params=_params("parallel", "parallel"),
        name="proj_matmul",
    )(x, w, col_scale)


def _proj_ln_kernel(*refs, n_in):
    a_refs, w_refs = refs[:n_in], refs[n_in:2 * n_in]
    r_ref, g_ref, b_ref, o_ref, obf_ref = refs[2 * n_in:]
    acc = _dot(a_refs[0][...], w_refs[0][...])
    for a_ref, w_ref in zip(a_refs[1:], w_refs[1:]):
        acc = acc + _dot(a_ref[...], w_ref[...])
    y = _layer_norm_rows(DN_ALPHA * r_ref[...] + acc, g_ref[...], b_ref[...])
    o_ref[...] = y
    half = y.shape[1] // 2
    obf_ref[...] = _pack_bf16_pair(y[:, :half], y[:, half:])


def _pack_bf16_pair(lo, hi):
    lo_bits = lax.shift_right_logical(lax.bitcast_convert_type(lo.astype(BF16).astype(F32), jnp.int32), 16)
    hi_bits = lax.bitcast_convert_type(hi.astype(BF16).astype(F32), jnp.int32) & jnp.int32(-65536)
    return lo_bits | hi_bits


def _unpack_bf16_pair(words):
    lo = lax.bitcast_convert_type(lax.shift_left(words, 16), F32)
    hi = lax.bitcast_convert_type(words & jnp.int32(-65536), F32)
    return lo, hi


def _proj_residual_ln(acts, weights, resid, g, b, tm=256):
    m, d = resid.shape
    tm = min(tm, m)
    n_in = len(acts)
    in_specs = [pl.BlockSpec((tm, a.shape[1]), lambda i: (i, 0)) for a in acts]
    in_specs += [pl.BlockSpec(w.shape, lambda i: (0, 0)) for w in weights]
    in_specs += [pl.BlockSpec((tm, d), lambda i: (i, 0)),
                 pl.BlockSpec((1, d), lambda i: (0, 0)),
                 pl.BlockSpec((1, d), lambda i: (0, 0))]
    return pl.pallas_call(
        functools.partial(_proj_ln_kernel, n_in=n_in),
        grid=(m // tm,),
        in_specs=in_specs,
        out_specs=[pl.BlockSpec((tm, d), lambda i: (i, 0)), pl.BlockSpec((tm, d // 2), lambda i: (i, 0))],
        out_shape=[jax.ShapeDtypeStruct((m, d), F32), jax.ShapeDtypeStruct((m, d // 2), jnp.int32)],
        compiler_params=_params("parallel"),
        name="out_proj_ln",
    )(*acts, *weights, resid, g.reshape(1, d), b.reshape(1, d))


def _t5_bucket(dist):
    max_exact = REL_BUCKETS // 2
    d = np.maximum(dist, 1).astype(np.float64)
    large = max_exact + (np.log(d / max_exact) / np.log(REL_MAX_DIST / max_exact)
                         * (REL_BUCKETS - max_exact)).astype(np.int64)
    large = np.minimum(large, REL_BUCKETS - 1)
    return np.where(dist < max_exact, dist, large).astype(np.int32)


def _dilated_kernel(q_ref, k_ref, v_ref, bucket_ref, rel_bias_ref, out_ref, o_scr, lse_scr):
    blk = A_BLOCK
    seq = q_ref.shape[1]
    head = pl.program_id(1)
    qi = lax.broadcasted_iota(jnp.int32, (blk, 2 * blk), 0)
    ki = lax.broadcasted_iota(jnp.int32, (blk, 2 * blk), 1)
    rel = qi + blk - ki

    for bi, (window, dil) in enumerate(A_PATTERNS):
        band = window // dil
        n_blocks = seq // (dil * blk)
        valid = jnp.abs(2 * rel - band) <= band
        bucket = bucket_ref[bi]
        bias = jnp.zeros((blk, 2 * blk), F32)
        for b in range(REL_BUCKETS):
            bias = jnp.where(bucket == b, rel_bias_ref[head, b], bias)

        def attend(q_start, k_start, n_keys, bias_blk, valid_blk, bi=bi, dil=dil):
            q_rows = pl.ds(q_start, blk, stride=dil)
            k_rows = pl.ds(k_start, n_keys, stride=dil)
            q = q_ref[0, q_rows, :].astype(BF16)
            kk = k_ref[0, k_rows, :].astype(BF16)
            vv = v_ref[0, k_rows, :].astype(BF16)
            s = _dot_nt(q, kk) + bias_blk
            s = jnp.where(valid_blk, s, NEG_INF)
            m = jnp.max(s, axis=-1, keepdims=True)
            p = jnp.exp(s - m)
            l = jnp.sum(p, axis=-1, keepdims=True)
            o_scr[bi, q_rows, :] = _dot(p.astype(BF16), vv) / l
            lse_scr[bi, q_rows, :] = jnp.broadcast_to(m + jnp.log(l), (blk, HEAD_DIM))

        def first(r, carry, attend=attend, bias=bias, valid=valid):
            attend(r, r, blk, bias[:, blk:], valid[:, blk:])
            return carry

        lax.fori_loop(0, dil, first, 0, unroll=min(dil, DILATED_UNROLL))

        def rest(t, carry, attend=attend, bias=bias, valid=valid, dil=dil):
            r = t & (dil - 1)
            n = (t >> (dil.bit_length() - 1)) + 1
            attend(r + n * blk * dil, r + (n - 1) * blk * dil, 2 * blk, bias, valid)
            return carry

        lax.fori_loop(0, dil * (n_blocks - 1), rest, 0, unroll=DILATED_UNROLL)

    chunk = 2 * blk
    nb = len(A_PATTERNS)

    def mix(c, carry):
        rows = pl.ds(pl.multiple_of(c * chunk, chunk), chunk)
        lses = [lse_scr[b, rows, :] for b in range(nb)]
        m = functools.reduce(jnp.maximum, lses)
        es = [jnp.exp(x - m) for x in lses]
        den = functools.reduce(lambda a, b: a + b, es)
        num = functools.reduce(lambda a, b: a + b, [e * o_scr[b, rows, :] for b, e in enumerate(es)])
        out_ref[0, rows, :] = (num / den).astype(out_ref.dtype)
        return carry

    lax.fori_loop(0, seq // chunk, mix, 0)


def _dilated_bucket_tables():
    qi = np.arange(A_BLOCK)[:, None]
    ki = np.arange(2 * A_BLOCK)[None, :]
    rel = qi + A_BLOCK - ki
    return np.stack([_t5_bucket(np.clip(rel, 0, None) * dilation) for _, dilation in A_PATTERNS], axis=0)


def _dilated_mixture(proj, rel_bias, bsz, seq, heads, proj_width):
    for window, dilation in A_PATTERNS:
        assert seq % (dilation * A_BLOCK) == 0 and window // dilation <= A_BLOCK
    nb = len(A_PATTERNS)
    view = proj.reshape(bsz, seq, proj_width)
    buckets = jnp.asarray(_dilated_bucket_tables())

    def in_spec(group):
        return pl.BlockSpec((1, seq, HEAD_DIM), lambda b, h, g=group: (b, 0, g * heads + h))

    o = pl.pallas_call(
        _dilated_kernel,
        grid=(bsz, heads),
        in_specs=[in_spec(0), in_spec(1), in_spec(2),
                  pl.BlockSpec(buckets.shape, lambda b, h: (0, 0, 0)),
                  pl.BlockSpec(memory_space=pltpu.SMEM)],
        out_specs=pl.BlockSpec((1, seq, HEAD_DIM), lambda b, h: (b, 0, h)),
        out_shape=jax.ShapeDtypeStruct((bsz, seq, heads * HEAD_DIM), BF16),
        scratch_shapes=[pltpu.VMEM((nb, seq, HEAD_DIM), F32), pltpu.VMEM((nb, seq, HEAD_DIM), F32)],
        compiler_params=_params("parallel", "parallel"),
        name="dilated_attention",
    )(view, view, view, buckets, rel_bias.astype(F32).T)
    return o.reshape(bsz * seq, heads * HEAD_DIM)


def _hgrn_levels(t):
    cs = []
    c = t // 2
    while c >= 1:
        cs.append(c)
        c //= 2
    return cs


def _hgrn_consts(t):
    idx = np.arange(t)
    ltri = (idx[:, None] >= idx[None, :]).astype(np.float32)
    sels = []
    for c in _hgrn_levels(t):
        if c < HGRN_BCAST_MIN:
            ref = (idx // (2 * c)) * (2 * c) + c - 1
            sels.append((idx[None, :] == ref[:, None]).astype(np.float32))
    return jnp.asarray(ltri, BF16), jnp.asarray(np.concatenate(sels, axis=0), BF16)


def _hgrn_kernel(q_ref, f_ref, i_ref, g_ref, lb_ref, gn_ref, ltri_ref, sel_ref, o_ref, state_ref,
                 *, n_chunks):
    t = HGRN_CHUNK
    dk = HEAD_DIM
    levels = _hgrn_levels(t)
    lb = lb_ref[0]
    gn = gn_ref[0]
    ltri = ltri_ref[...]
    sel = sel_ref[...]
    row = lax.broadcasted_iota(jnp.int32, (t, dk), 0)
    ti = lax.broadcasted_iota(jnp.int32, (t, t), 0)
    si = lax.broadcasted_iota(jnp.int32, (t, t), 1)
    state_ref[...] = jnp.zeros_like(state_ref)

    def sum3(y):
        return y[:, :dk] + y[:, dk:2 * dk] + y[:, 2 * dk:]

    def chunk(n, carry):
        st = pl.multiple_of(n * t, t)
        fq = _silu(q_ref[0, pl.ds(st, t), :])
        forget = lb + (1.0 - lb) * _sigmoid(f_ref[0, pl.ds(st, t), :])
        logf = jnp.log(forget)
        key = 1.0 - forget
        val = i_ref[0, pl.ds(st, t), :].astype(BF16)
        a = sum3(_dot(ltri, _split3(logf, 1)))
        aref_fine = sum3(_dot(sel, _split3(a, 1)))
        fq_b = fq.astype(BF16)
        key_b = key.astype(BF16)
        scores = jnp.where(ti == si, _dot_nt(fq_b, key_b), 0.0)
        n_fine = 0
        for c in levels:
            if c >= HGRN_BCAST_MIN:
                aref = jnp.concatenate(
                    [jnp.broadcast_to(a[b0 + c - 1:b0 + c, :], (2 * c, dk)) for b0 in range(0, t, 2 * c)],
                    axis=0)
            else:
                aref = aref_fine[n_fine * t:(n_fine + 1) * t, :]
                n_fine += 1
            upper = (row & (2 * c - 1)) >= c
            decayed = jnp.where(upper, fq, key) * jnp.exp(-jnp.abs(a - aref))
            ql = jnp.where(upper, decayed, 0.0).astype(BF16)
            kl = jnp.where(upper, 0.0, decayed).astype(BF16)
            s_l = _dot_nt(ql, kl)
            if 2 * c < t:
                sh = (2 * c).bit_length() - 1
                s_l = jnp.where((ti >> sh) == (si >> sh), s_l, 0.0)
            scores = scores + s_l
        state_t = state_ref[...]
        inter = _dot_nt((fq * jnp.exp(a)).astype(BF16), state_t.astype(BF16))
        o = inter + _dot(scores.astype(BF16), val)
        a_last = a[t - 1:t, :]
        k_dec = (key * jnp.exp(a_last - a)).astype(BF16)
        state_ref[...] = state_t * jnp.exp(a_last) + _dot_tn(val, k_dec)
        o = o * lax.rsqrt(jnp.mean(o * o, axis=-1, keepdims=True) + RMS_EPS) * gn
        o = o * _silu(g_ref[0, pl.ds(st, t), :])
        o_ref[0, pl.ds(st, t), :] = o.astype(o_ref.dtype)
        return carry

    lax.fori_loop(0, n_chunks, chunk, 0, unroll=8)


def _hgrn2(proj, lb, gn_g, bsz, seq, heads, proj_width, first_group):
    assert seq % HGRN_CHUNK == 0
    view = proj.reshape(bsz, seq, proj_width)
    ltri, sel = _hgrn_consts(HGRN_CHUNK)

    def in_spec(group):
        return pl.BlockSpec((1, seq, HEAD_DIM), lambda b, h, g=group: (b, 0, (first_group + g) * heads + h))

    head_vec = pl.BlockSpec((1, 1, HEAD_DIM), lambda b, h: (h, 0, 0))
    o = pl.pallas_call(
        functools.partial(_hgrn_kernel, n_chunks=seq // HGRN_CHUNK),
        grid=(bsz, heads),
        in_specs=[in_spec(0), in_spec(1), in_spec(2), in_spec(3), head_vec, head_vec,
                  pl.BlockSpec(ltri.shape, lambda b, h: (0, 0)),
                  pl.BlockSpec(sel.shape, lambda b, h: (0, 0))],
        out_specs=pl.BlockSpec((1, seq, HEAD_DIM), lambda b, h: (b, 0, h)),
        out_shape=jax.ShapeDtypeStruct((bsz, seq, heads * HEAD_DIM), BF16),
        scratch_shapes=[pltpu.VMEM((HEAD_DIM, HEAD_DIM), F32)],
        compiler_params=_params("parallel", "parallel"),
        name="hgrn2",
    )(view, view, view, view, lb.reshape(heads, 1, HEAD_DIM).astype(F32),
      gn_g.reshape(heads, 1, HEAD_DIM).astype(F32), ltri, sel)
    return o.reshape(bsz * seq, heads * HEAD_DIM)


def _fox_gate_kernel(x_ref, w_ref, b_ref, ltri_ref, place_q_ref, place_k_ref, ones_q_ref, ones_k_ref,
                     qx_ref, kx_ref, carry):
    heads = w_ref.shape[1]
    ts = x_ref.shape[1]

    @pl.when(pl.program_id(1) == 0)
    def _():
        carry[...] = jnp.zeros_like(carry)

    def log_sigmoid(z):
        return jnp.minimum(z, 0.0) - jnp.log(1.0 + jnp.exp(-jnp.abs(z)))

    x = x_ref[0]
    xh = x.astype(BF16)
    xl = (x - xh.astype(F32)).astype(BF16)
    w = w_ref[...]
    wh = w.astype(BF16)
    wl = (w - wh.astype(F32)).astype(BF16)
    f = _dot(xh, wh) + _dot(xh, wl) + _dot(xl, wh) + b_ref[...]
    c = _dot(ltri_ref[...], _split3(log_sigmoid(f), 1))
    c = c[:, :heads] + c[:, heads:2 * heads] + c[:, 2 * heads:] + carry[...]
    carry[...] = c[ts - 1:ts, :]
    terms = _split3(c * LOG2_E, 1)
    qx_ref[0] = (_dot(terms, place_q_ref[...]) + ones_q_ref[...]).astype(BF16)
    kx_ref[0] = (_dot(terms, place_k_ref[...]) + ones_k_ref[...]).astype(BF16)


def _fox_gate_placement(heads):
    place_q = np.zeros((3 * heads, heads * HEAD_DIM), np.float32)
    place_k = np.zeros((3 * heads, heads * HEAD_DIM), np.float32)
    ones_q = np.zeros((1, heads * HEAD_DIM), np.float32)
    ones_k = np.zeros((1, heads * HEAD_DIM), np.float32)
    for h in range(heads):
        for j in range(3):
            place_q[j * heads + h, h * HEAD_DIM + j] = 1.0
            place_k[j * heads + h, h * HEAD_DIM + 3 + j] = -1.0
            ones_q[0, h * HEAD_DIM + 3 + j] = 1.0
            ones_k[0, h * HEAD_DIM + j] = 1.0
    return jnp.asarray(place_q, BF16), jnp.asarray(place_k, BF16), jnp.asarray(ones_q), jnp.asarray(ones_k)


def _fox_gate_terms(x, w_f, b_f, bsz, seq, ts=512):
    d = x.shape[-1]
    heads = w_f.shape[1]
    ts = min(ts, seq)
    idx = np.arange(ts)
    ltri = jnp.asarray(idx[:, None] >= idx[None, :], BF16)
    consts = _fox_gate_placement(heads)
    const = lambda shape: pl.BlockSpec(shape, lambda b, i: (0,) * len(shape))
    out_spec = pl.BlockSpec((1, ts, heads * HEAD_DIM), lambda b, i: (b, i, 0))
    out_sds = jax.ShapeDtypeStruct((bsz, seq, heads * HEAD_DIM), BF16)
    return pl.pallas_call(
        _fox_gate_kernel,
        grid=(bsz, seq // ts),
        in_specs=[pl.BlockSpec((1, ts, d), lambda b, i: (b, i, 0)),
                  const((d, heads)), const((1, heads)), const((ts, ts))] + [const(a.shape) for a in consts],
        out_specs=[out_spec, out_spec],
        out_shape=[out_sds, out_sds],
        scratch_shapes=[pltpu.VMEM((1, heads), F32)],
        compiler_params=_params("parallel", "arbitrary"),
        name="fox_gate_cumsum",
    )(x.reshape(bsz, seq, d), w_f.astype(F32), b_f.astype(F32).reshape(1, heads), ltri, *consts)


def _fox_kernel(q_ref, qx_ref, k_ref, kx_ref, vt_ref, o_ref, s_a, s_b, p_a, p_b, acc_ref):
    tq, tk = FOX_Q_TILE, FOX_K_TILE
    assert tq == 2 * tk
    for i in range(q_ref.shape[1] // tq):
        _fox_query_tile(i, q_ref, qx_ref, k_ref, kx_ref, vt_ref, o_ref, s_a, s_b, p_a, p_b, acc_ref)


def _fox_query_tile(i, q_ref, qx_ref, k_ref, kx_ref, vt_ref, o_ref, s_a, s_b, p_a, p_b, acc_ref):
    tq, tk = FOX_Q_TILE, FOX_K_TILE
    lanes = HEAD_DIM
    rows = slice(i * tq, (i + 1) * tq)
    q = jnp.concatenate([q_ref[0, rows, :], qx_ref[0, rows, :]], axis=1)

    def logits_into(j, s_ref):
        st = pl.multiple_of(j * tk, tk)
        kk = jnp.concatenate([k_ref[0, pl.ds(st, tk), :], kx_ref[0, pl.ds(st, tk), :]], axis=1)
        s_ref[...] = _dot_nt(kk, q)

    def softmax_pv(j, s_ref, p_ref, m, l, diag):
        m_out, l_out, alphas = [], [], []
        for c in range(tq // lanes):
            cols = slice(c * lanes, (c + 1) * lanes)
            if diag is not None and (c + 1) * lanes - 1 < diag * tk:
                m_out.append(m[:, cols])
                l_out.append(l[:, cols])
                alphas.append(jnp.ones((1, lanes), F32))
                p_ref[:, cols] = jnp.zeros((tk, lanes), BF16)
                continue
            s = s_ref[:, cols]
            if diag is not None and c * lanes < diag * tk + tk - 1:
                key = lax.broadcasted_iota(jnp.int32, (tk, lanes), 0) + diag * tk
                qry = lax.broadcasted_iota(jnp.int32, (tk, lanes), 1) + c * lanes
                s = jnp.where(key <= qry, s, NEG_INF)
            m_new = jnp.maximum(m[:, cols], jnp.max(s, axis=0, keepdims=True))
            alpha = jnp.exp2(m[:, cols] - m_new)
            p = jnp.exp2(s - m_new)
            l_out.append(alpha * l[:, cols] + jnp.sum(p, axis=0, keepdims=True))
            m_out.append(m_new)
            alphas.append(alpha)
            p_ref[:, cols] = p.astype(BF16)
        vt = vt_ref[0, :, pl.ds(pl.multiple_of(j * tk, tk), tk)]
        acc_ref[...] = jnp.concatenate(alphas, axis=1) * acc_ref[...] + _dot(vt, p_ref[...])
        return jnp.concatenate(m_out, axis=1), jnp.concatenate(l_out, axis=1)

    def pair(jp, carry):
        m, l = carry
        t0 = 2 * jp
        logits_into(t0 + 1, s_b)
        m, l = softmax_pv(t0, s_a, p_a, m, l, None)
        logits_into(t0 + 2, s_a)
        return softmax_pv(t0 + 1, s_b, p_b, m, l, None)

    acc_ref[...] = jnp.zeros_like(acc_ref)
    logits_into(0, s_a)
    m, l = lax.fori_loop(0, i, pair, (jnp.full((1, tq), NEG_INF, F32), jnp.zeros((1, tq), F32)),
                         unroll=min(max(i, 1), FOX_PAIR_UNROLL))
    t0 = 2 * i
    logits_into(t0 + 1, s_b)
    m, l = softmax_pv(t0, s_a, p_a, m, l, 0)
    m, l = softmax_pv(t0 + 1, s_b, p_b, m, l, 1)
    o_ref[0, rows, :] = jnp.transpose(acc_ref[...] / l).astype(o_ref.dtype)


def _fox_attention(qkv, qx, kx, bsz, seq, heads):
    tq = FOX_Q_TILE
    assert seq % tq == 0 and tq % FOX_K_TILE == 0
    width = heads * HEAD_DIM
    view = qkv.reshape(bsz, seq, 3 * width)
    v_t = jnp.swapaxes(view[:, :, 2 * width:], 1, 2)
    head_spec = lambda off: pl.BlockSpec((1, seq, HEAD_DIM), lambda b, h: (b, 0, off + h))
    o = pl.pallas_call(
        _fox_kernel,
        grid=(bsz, heads),
        in_specs=[head_spec(0), head_spec(0), head_spec(heads), head_spec(0),
                  pl.BlockSpec((1, HEAD_DIM, seq), lambda b, h: (b, h, 0))],
        out_specs=head_spec(0),
        out_shape=jax.ShapeDtypeStruct((bsz, seq, width), BF16),
        scratch_shapes=[pltpu.VMEM((FOX_K_TILE, tq), F32), pltpu.VMEM((FOX_K_TILE, tq), F32),
                        pltpu.VMEM((FOX_K_TILE, tq), BF16), pltpu.VMEM((FOX_K_TILE, tq), BF16),
                        pltpu.VMEM((HEAD_DIM, tq), F32)],
        compiler_params=_params("parallel", "parallel"),
        name="fox_attention",
    )(view, qx, view, kx, v_t)
    return o.reshape(bsz * seq, width)


def _router_kernel(x_ref, wt_ref, bias_ref, utri_ref, eidx_ref, rank_ref, w_ref, cnt_ref, carry):
    gsz = GROUP_SIZE
    tt = x_ref.shape[0]

    @pl.when(pl.program_id(0) == 0)
    def _():
        carry[...] = jnp.zeros_like(carry)

    x = x_ref[...]
    xh = x.astype(BF16)
    xl = (x - xh.astype(F32)).astype(BF16)
    wt = wt_ref[...]
    wh = wt.astype(BF16)
    wl = (wt - wh.astype(F32)).astype(BF16)
    logits = _dot_nt(wh, xh) + _dot_nt(wh, xl) + _dot_nt(wl, xh)
    scores = _sigmoid(logits)
    sel = scores + bias_ref[...]
    sub = lax.broadcasted_iota(jnp.int32, (gsz, tt), 0)
    neg = -jnp.inf

    group_score = []
    for g in range(N_GROUPS):
        blk = sel[g * gsz:(g + 1) * gsz, :]
        m1 = jnp.max(blk, axis=0, keepdims=True)
        first = jnp.min(jnp.where(blk == m1, sub, gsz), axis=0, keepdims=True)
        m2 = jnp.max(jnp.where(sub == first, neg, blk), axis=0, keepdims=True)
        group_score.append(m1 + m2)

    chosen = [jnp.zeros((1, tt), F32) for _ in range(N_GROUPS)]
    for _ in range(TOPK_GROUPS):
        m = functools.reduce(jnp.maximum, group_score)
        gi = jnp.full((1, tt), N_GROUPS, jnp.int32)
        for g in reversed(range(N_GROUPS)):
            gi = jnp.where(group_score[g] == m, g, gi)
        for g in range(N_GROUPS):
            hit = gi == g
            chosen[g] = jnp.where(hit, 1.0, chosen[g])
            group_score[g] = jnp.where(hit, neg, group_score[g])

    cand = [jnp.where(jnp.broadcast_to(chosen[g], (gsz, tt)) > 0.5, sel[g * gsz:(g + 1) * gsz, :], NEG_INF)
            for g in range(N_GROUPS)]
    score_g = [scores[g * gsz:(g + 1) * gsz, :] for g in range(N_GROUPS)]
    eiota = [sub + g * gsz for g in range(N_GROUPS)]
    picked = [jnp.zeros((gsz, tt), F32) for _ in range(N_GROUPS)]
    top_idx, top_score = [], []
    for _ in range(TOP_K):
        m = functools.reduce(jnp.maximum, [jnp.max(c, axis=0, keepdims=True) for c in cand])
        ei = functools.reduce(jnp.minimum, [
            jnp.min(jnp.where(cand[g] == m, eiota[g], N_EXPERTS), axis=0, keepdims=True)
            for g in range(N_GROUPS)])
        sc = jnp.zeros((1, tt), F32)
        for g in range(N_GROUPS):
            hit = eiota[g] == ei
            picked[g] = jnp.where(hit, 1.0, picked[g])
            cand[g] = jnp.where(hit, neg, cand[g])
            sc = sc + jnp.sum(jnp.where(hit, score_g[g], 0.0), axis=0, keepdims=True)
        top_idx.append(ei)
        top_score.append(sc)

    total = functools.reduce(lambda a, b: a + b, top_score)
    mask = jnp.concatenate(picked, axis=0)
    incl = _dot(mask.astype(BF16), utri_ref[...])
    rank = carry[...] + incl - mask
    for r in range(TOP_K):
        ei = top_idx[r]
        rk = jnp.zeros((1, tt), F32)
        for g in range(N_GROUPS):
            rk = rk + jnp.sum(jnp.where(eiota[g] == ei, rank[g * gsz:(g + 1) * gsz, :], 0.0),
                              axis=0, keepdims=True)
        eidx_ref[r:r + 1, :] = ei
        rank_ref[r:r + 1, :] = rk.astype(jnp.int32)
        w_ref[r:r + 1, :] = top_score[r] / total * ROUTED_SCALE
    new_carry = carry[...] + incl[:, tt - 1:tt]
    carry[...] = new_carry
    cnt_ref[...] = jnp.broadcast_to(new_carry, cnt_ref.shape)


def _router(x, router_w, router_bias):
    n, d = x.shape
    tt = min(ROUTER_TILE, n)
    idx = np.arange(tt)
    utri = jnp.asarray(idx[:, None] <= idx[None, :], BF16)
    eidx, rank, w8, cnt = pl.pallas_call(
        _router_kernel,
        grid=(n // tt,),
        in_specs=[pl.BlockSpec((tt, d), lambda i: (i, 0)),
                  pl.BlockSpec((N_EXPERTS, d), lambda i: (0, 0)),
                  pl.BlockSpec((N_EXPERTS, 1), lambda i: (0, 0)),
                  pl.BlockSpec((tt, tt), lambda i: (0, 0))],
        out_specs=[pl.BlockSpec((TOP_K, tt), lambda i: (0, i)),
                   pl.BlockSpec((TOP_K, tt), lambda i: (0, i)),
                   pl.BlockSpec((TOP_K, tt), lambda i: (0, i)),
                   pl.BlockSpec((N_EXPERTS, 128), lambda i: (0, 0))],
        out_shape=[jax.ShapeDtypeStruct((TOP_K, n), jnp.int32),
                   jax.ShapeDtypeStruct((TOP_K, n), jnp.int32),
                   jax.ShapeDtypeStruct((TOP_K, n), F32),
                   jax.ShapeDtypeStruct((N_EXPERTS, 128), F32)],
        scratch_shapes=[pltpu.VMEM((N_EXPERTS, 1), F32)],
        compiler_params=_params("arbitrary"),
        name="moe_router",
    )(x, router_w.astype(F32).T, router_bias.astype(F32).reshape(N_EXPERTS, 1), utri)
    return eidx, rank, w8, cnt[:, 0].astype(jnp.int32)


def _expert_kernel(te_ref, tv_ref, nt_ref, src_ref, x_hbm, wg_ref, wu_ref, wd_ref, y_ref,
                   xbuf_a, xbuf_b, sem_a, sem_b, wg_bf, wu_bf, wd_bf):
    i = pl.program_id(0)
    n_live = nt_ref[0]
    tm, half = xbuf_a.shape
    bufs = ((xbuf_a, sem_a), (xbuf_b, sem_b))
    prev = jnp.maximum(i - 1, 0)

    def copy(buf, sem, r):
        return pltpu.make_async_copy(x_hbm.at[pl.ds(src_ref[0, 0, r], 1), :], buf.at[pl.ds(r, 1), :], sem)

    def ffn(buf, sem, gather=None):
        pltpu.make_async_copy(buf, buf, sem).wait()
        if gather is not None:
            for r in range(tm):
                copy(*gather, r).start()
        row = lax.broadcasted_iota(jnp.int32, (tm, 1), 0)
        x_lo, x_hi = _unpack_bf16_pair(jnp.where(row < tv_ref[prev], buf[...], 0))
        x_lo, x_hi = x_lo.astype(BF16), x_hi.astype(BF16)
        gate = _dot(x_lo, wg_bf[:half, :]) + _dot(x_hi, wg_bf[half:, :])
        up = _dot(x_lo, wu_bf[:half, :]) + _dot(x_hi, wu_bf[half:, :])
        hidden = (_silu(gate) * up).astype(BF16)
        y = _dot(hidden, wd_bf[...])
        y_ref[...] = _pack_bf16_pair(y[:, :half], y[:, half:])

    @pl.when(i == 0)
    def _():
        def issue(r, carry):
            copy(*bufs[0], r).start()
            return carry
        lax.fori_loop(0, tm, issue, 0)

    computing = jnp.logical_and(i >= 1, i <= n_live)
    new_expert = jnp.logical_or(i == 1, te_ref[prev] != te_ref[jnp.maximum(i - 2, 0)])

    @pl.when(jnp.logical_and(computing, new_expert))
    def _():
        wg_bf[...] = wg_ref[0, 0].astype(BF16)
        wu_bf[...] = wu_ref[0, 0].astype(BF16)
        wd_bf[...] = wd_ref[0, 0].astype(BF16)

    for parity in range(2):
        @pl.when(jnp.logical_and(jnp.logical_and(i >= 1, i < n_live), i % 2 == parity))
        def _(parity=parity):
            ffn(*bufs[1 - parity], gather=bufs[parity])

        @pl.when(jnp.logical_and(i == n_live, (i - 1) % 2 == parity))
        def _(parity=parity):
            ffn(*bufs[parity])


def _expert_ffn(x_packed, src_table, w_gate, w_up, w_down, layer, tile_expert, tile_valid, num_tiles):
    n, half = x_packed.shape
    d = 2 * half
    max_tiles, _, tm = src_table.shape
    e_dim = w_gate.shape[3]
    prev = lambda i, nt: jnp.minimum(jnp.maximum(i - 1, 0), nt[0] - 1)
    w_map = lambda i, te, tv, nt: (layer, te[prev(i, nt)], 0, 0)
    xbuf = pltpu.VMEM((tm, half), jnp.int32)
    return pl.pallas_call(
        _expert_kernel,
        grid_spec=pltpu.PrefetchScalarGridSpec(
            num_scalar_prefetch=3,
            grid=(max_tiles + 1,),
            in_specs=[pl.BlockSpec((1, 1, tm), lambda i, te, tv, nt: (jnp.minimum(i, nt[0] - 1), 0, 0),
                                   memory_space=pltpu.SMEM),
                      pl.BlockSpec(memory_space=pl.ANY),
                      pl.BlockSpec((1, 1, d, e_dim), w_map),
                      pl.BlockSpec((1, 1, d, e_dim), w_map),
                      pl.BlockSpec((1, 1, e_dim, d), w_map)],
            out_specs=pl.BlockSpec((tm, half), lambda i, te, tv, nt: (prev(i, nt), 0)),
            scratch_shapes=[xbuf, xbuf, pltpu.SemaphoreType.DMA(()), pltpu.SemaphoreType.DMA(()),
                            pltpu.VMEM((d, e_dim), BF16), pltpu.VMEM((d, e_dim), BF16),
                            pltpu.VMEM((e_dim, d), BF16)]),
        out_shape=jax.ShapeDtypeStruct((max_tiles * tm, half), jnp.int32),
        compiler_params=_params("arbitrary"),
        name="moe_expert_ffn",
    )(tile_expert, tile_valid, num_tiles, src_table, x_packed, w_gate, w_up, w_down)


def _combine_kernel(dest_ref, x_ref, w8_ref, sg_ref, su_ref, sd_ref, g_ref, b_ref, y_hbm,
                    o_ref, obf_ref, ybuf_a, ybuf_b, sem_a, sem_b):
    tc = x_ref.shape[0]
    i = pl.program_id(0)
    n_tiles = pl.num_programs(0) - 1
    bufs = ((ybuf_a, sem_a), (ybuf_b, sem_b))

    def copy(buf, sem, t, k):
        return pltpu.make_async_copy(y_hbm.at[pl.ds(dest_ref[0, k, t], 1), :],
                                     buf.at[k, pl.ds(t, 1), :], sem)

    def finish(buf, sem, gather=None):
        x = x_ref[...]
        xb = x.astype(BF16)
        hidden = (_silu(_dot(xb, sg_ref[...])) * _dot(xb, su_ref[...])).astype(BF16)
        acc = _dot(hidden, sd_ref[...])
        pltpu.make_async_copy(buf, buf, sem).wait()
        if gather is not None:
            for t in range(tc):
                for k in range(TOP_K):
                    copy(*gather, t, k).start()
        w8 = w8_ref[...]
        half = buf.shape[2]
        routed_lo = jnp.zeros((tc, half), F32)
        routed_hi = jnp.zeros((tc, half), F32)
        for k in range(TOP_K):
            y_lo, y_hi = _unpack_bf16_pair(buf[k])
            routed_lo = routed_lo + w8[:, k:k + 1] * y_lo
            routed_hi = routed_hi + w8[:, k:k + 1] * y_hi
        acc = acc + jnp.concatenate([routed_lo, routed_hi], axis=1)
        y = _layer_norm_rows(DN_ALPHA * x + acc, g_ref[...], b_ref[...])
        o_ref[...] = y
        obf_ref[...] = y.astype(BF16)

    @pl.when(i == 0)
    def _():
        def issue(t, carry):
            for k in range(TOP_K):
                copy(*bufs[0], t, k).start()
            return carry
        lax.fori_loop(0, tc, issue, 0)

    for parity in range(2):
        @pl.when(jnp.logical_and(jnp.logical_and(i >= 1, i < n_tiles), i % 2 == parity))
        def _(parity=parity):
            finish(*bufs[1 - parity], gather=bufs[parity])

        @pl.when(jnp.logical_and(i == n_tiles, (i - 1) % 2 == parity))
        def _(parity=parity):
            finish(*bufs[parity])


def _combine(x, w8_rows, dest_tiles, y_sorted, s_gate, s_up, s_down, g, b):
    n, d = x.shape
    tc = dest_tiles.shape[2]
    sdim = s_gate.shape[1]
    const = lambda shape: pl.BlockSpec(shape, lambda i: (0,) * len(shape))
    n_tiles = n // tc
    prev = lambda i: (jnp.maximum(i - 1, 0), 0)
    row = pl.BlockSpec((tc, d), prev)
    ybuf = pltpu.VMEM((TOP_K, tc, d // 2), jnp.int32)
    return pl.pallas_call(
        _combine_kernel,
        grid=(n_tiles + 1,),
        in_specs=[pl.BlockSpec((1, TOP_K, tc), lambda i: (jnp.minimum(i, n_tiles - 1), 0, 0),
                               memory_space=pltpu.SMEM),
                  row, pl.BlockSpec((tc, TOP_K), prev),
                  const((d, sdim)), const((d, sdim)), const((sdim, d)), const((1, d)), const((1, d)),
                  pl.BlockSpec(memory_space=pl.ANY)],
        out_specs=[row, row],
        out_shape=[jax.ShapeDtypeStruct((n, d), F32), jax.ShapeDtypeStruct((n, d), BF16)],
        scratch_shapes=[ybuf, ybuf, pltpu.SemaphoreType.DMA(()), pltpu.SemaphoreType.DMA(())],
        compiler_params=_params("arbitrary"),
        name="moe_combine",
    )(dest_tiles, x, w8_rows, s_gate, s_up, s_down, g.reshape(1, d), b.reshape(1, d), y_sorted)


def _tile_major(a, tile):
    k, n = a.shape
    return a.reshape(k, n // tile, tile).transpose(1, 0, 2)


def _moe(x, x_packed, router_w, router_bias, w_gate, w_up, w_down, layer, s_gate, s_up, s_down, ln_g, ln_b):
    n, d = x.shape
    tm = EXPERT_TILE
    eidx, rank, w8, cnt = _router(x, router_w, router_bias)
    tiles_e = (cnt + tm - 1) // tm
    tile_end = jnp.cumsum(tiles_e)
    tile_start = tile_end - tiles_e
    row_start = tile_start * tm
    e_ids = jnp.arange(N_EXPERTS, dtype=jnp.int32)
    dest = jnp.sum(jnp.where(eidx[None] == e_ids[:, None, None], row_start[:, None, None], 0), axis=0) + rank
    max_tiles = (n * TOP_K) // tm + N_EXPERTS
    num_tiles = tile_end[-1]
    tile_ids = jnp.arange(max_tiles, dtype=jnp.int32)
    t_clamped = jnp.minimum(tile_ids, num_tiles - 1)
    tile_expert = jnp.sum(t_clamped[:, None] >= tile_end[None, :], axis=1).astype(jnp.int32)
    tile_valid = jnp.clip(cnt[tile_expert] - (t_clamped - tile_start[tile_expert]) * tm, 0, tm).astype(jnp.int32)

    tokens = jnp.tile(jnp.arange(n, dtype=jnp.int32), TOP_K)
    _, tok_sorted = lax.sort_key_val(dest.reshape(-1), tokens)
    tok_sorted = jnp.concatenate([tok_sorted, jnp.zeros((tm,), jnp.int32)])
    cnt_before = jnp.cumsum(cnt) - cnt
    first_slot = cnt_before[tile_expert] + (t_clamped - tile_start[tile_expert]) * tm
    window = first_slot[:, None] + jnp.arange(tm, dtype=jnp.int32)[None, :]
    src_table = tok_sorted.at[window].get(mode="promise_in_bounds")
    ys = _expert_ffn(x_packed, src_table.reshape(max_tiles, 1, tm), w_gate, w_up, w_down, layer,
                     tile_expert, tile_valid, num_tiles.reshape(1).astype(jnp.int32))
    return _combine(x, w8.T, _tile_major(dest, min(COMBINE_TILE, n)), ys, s_gate, s_up, s_down, ln_g, ln_b)


def _even_layer(x, x_bf, bsz, seq, w_in, gn_g, w_out, rel_bias, lb, ln_g, ln_b):
    heads = w_out.shape[0] // (2 * HEAD_DIM)
    width = heads * HEAD_DIM
    proj_width = w_in.shape[1]
    col_scale = jnp.concatenate([jnp.full((1, width), HEAD_DIM ** -0.5, F32),
                                 jnp.ones((1, proj_width - width), F32)], axis=1)
    proj = _matmul(x_bf, w_in, col_scale, F32)
    o_a = _dilated_mixture(proj, rel_bias, bsz, seq, heads, proj_width)
    o_b = _hgrn2(proj, lb, gn_g, bsz, seq, heads, proj_width, first_group=3)
    return _proj_residual_ln([o_a, o_b], [w_out[:width], w_out[width:]], x, ln_g, ln_b)


def _odd_layer(x, x_bf, bsz, seq, w_qkv, w_f, b_f, w_out, ln_g, ln_b):
    heads = w_f.shape[1]
    width = heads * HEAD_DIM
    col_scale = jnp.concatenate([jnp.full((1, width), LOG2_E * HEAD_DIM ** -0.5, F32),
                                 jnp.ones((1, 2 * width), F32)], axis=1)
    qkv = _matmul(x_bf, w_qkv, col_scale, BF16)
    qx, kx = _fox_gate_terms(x, w_f, b_f, bsz, seq)
    o = _fox_attention(qkv, qx, kx, bsz, seq, heads)
    return _proj_residual_ln([o], [w_out], x, ln_g, ln_b)


def kernel(x, rel_bias, hgrn_lb_logits, a_w_in, a_gn_g, a_w_out, c_w_in, c_b_f, c_w_out, ln_mix_g, ln_mix_b, ln_ffn_g, ln_ffn_b, router_w, router_bias, exp_w_gate, exp_w_up, exp_w_down, sh_w_gate, sh_w_up, sh_w_down):
    bsz, seq, d = x.shape
    depth = ln_mix_g.shape[0]
    lb_all = jnp.cumsum(jax.nn.softmax(hgrn_lb_logits.astype(F32), axis=0), axis=0)
    h = x.reshape(bsz * seq, d).astype(F32)
    h_bf = h.astype(BF16)
    for layer in range(depth):
        j = layer // 2
        if layer % 2 == 0:
            h, h_packed = _even_layer(h, h_bf, bsz, seq, a_w_in[j].astype(BF16), a_gn_g[j],
                                  a_w_out[j].astype(BF16), rel_bias, lb_all[layer],
                                  ln_mix_g[layer], ln_mix_b[layer])
        else:
            c_width = c_w_out.shape[1]
            w_in = c_w_in[j]
            h, h_packed = _odd_layer(h, h_bf, bsz, seq, w_in[:, :3 * c_width].astype(BF16),
                                 w_in[:, 3 * c_width:], c_b_f[j], c_w_out[j].astype(BF16),
                                 ln_mix_g[layer], ln_mix_b[layer])
        h, h_bf = _moe(h, h_packed, router_w[layer], router_bias[layer], exp_w_gate, exp_w_up, exp_w_down, layer,
                       sh_w_gate[layer].astype(BF16), sh_w_up[layer].astype(BF16),
                       sh_w_down[layer].astype(BF16), ln_ffn_g[layer], ln_ffn_b[layer])
    return h.reshape(bsz, seq, d)
```

```python
import functools

import numpy as np
import jax
import jax.numpy as jnp
from jax import lax
from jax.experimental import pallas as pl
from jax.experimental.pallas import tpu as pltpu

F32 = jnp.float32
BF16 = jnp.bfloat16

HEAD_DIM = 128
A_PATTERNS = ((128, 1), (512, 4), (2048, 16))
A_BLOCK = 128
REL_BUCKETS = 32
REL_MAX_DIST = 2048
N_EXPERTS = 64
TOP_K = 8
N_GROUPS = 8
GROUP_SIZE = N_EXPERTS // N_GROUPS
TOPK_GROUPS = 4
ROUTED_SCALE = 2.5
LN_EPS = 1e-5
RMS_EPS = 1e-6
DEPTH = 2
DN_ALPHA = (2 * DEPTH) ** 0.25
NEG_INF = -1e30
LOG2_E = 1.4426950408889634

DILATED_UNROLL = 16
HGRN_CHUNK = 128
HGRN_BCAST_MIN = 8
FOX_Q_TILE = 512
FOX_K_TILE = 256
FOX_PAIR_UNROLL = 4
ROUTER_TILE = 512
EXPERT_TILE = 512
COMBINE_TILE = 128
VMEM_LIMIT = 56 * 1024 * 1024

_NT = (((1,), (1,)), ((), ()))
_TN = (((0,), (0,)), ((), ()))


def _dot(a, b):
    return jnp.dot(a, b, preferred_element_type=F32)


def _dot_nt(a, b):
    return lax.dot_general(a, b, _NT, preferred_element_type=F32)


def _dot_tn(a, b):
    return lax.dot_general(a, b, _TN, preferred_element_type=F32)


def _sigmoid(x):
    return 1.0 / (1.0 + jnp.exp(-x))


def _silu(x):
    return x * _sigmoid(x)


def _split3(x, axis):
    hi = x.astype(BF16)
    r1 = x - hi.astype(F32)
    mid = r1.astype(BF16)
    lo = (r1 - mid.astype(F32)).astype(BF16)
    return jnp.concatenate([hi, mid, lo], axis=axis)


def _layer_norm_rows(y, g, b):
    mu = jnp.mean(y, axis=-1, keepdims=True)
    yc = y - mu
    var = jnp.mean(yc * yc, axis=-1, keepdims=True)
    return yc * lax.rsqrt(var + LN_EPS) * g + b


def _params(*sem):
    return pltpu.CompilerParams(dimension_semantics=sem, vmem_limit_bytes=VMEM_LIMIT)


def _mm_kernel(x_ref, w_ref, cs_ref, o_ref):
    o_ref[...] = (_dot(x_ref[...], w_ref[...]) * cs_ref[...]).astype(o_ref.dtype)


def _matmul(x, w, col_scale, out_dtype, tm=1024, tn=1024):
    m, k = x.shape
    nw = w.shape[1]
    tm, tn = min(tm, m), min(tn, nw)
    assert m % tm == 0 and nw % tn == 0
    return pl.pallas_call(
        _mm_kernel,
        grid=(nw // tn, m // tm),
        in_specs=[pl.BlockSpec((tm, k), lambda j, i: (i, 0)),
                  pl.BlockSpec((k, tn), lambda j, i: (0, j)),
                  pl.BlockSpec((1, tn), lambda j, i: (0, j))],
        out_specs=pl.BlockSpec((tm, tn), lambda j, i: (i, j)),
        out_shape=jax.ShapeDtypeStruct((m, nw), out_dtype),
        compiler_params=_params("parallel", "parallel"),
        name="proj_matmul",
    )(x, w, col_scale)


def _proj_ln_kernel(*refs, n_in):
    a_refs, w_refs = refs[:n_in], refs[n_in:2 * n_in]
    r_ref, g_ref, b_ref, o_ref, obf_ref = refs[2 * n_in:]
    acc = _dot(a_refs[0][...], w_refs[0][...])
    for a_ref, w_ref in zip(a_refs[1:], w_refs[1:]):
        acc = acc + _dot(a_ref[...], w_ref[...])
    y = _layer_norm_rows(DN_ALPHA * r_ref[...] + acc, g_ref[...], b_ref[...])
    o_ref[...] = y
    half = y.shape[1] // 2
    obf_ref[...] = _pack_bf16_pair(y[:, :half], y[:, half:])


def _pack_bf16_pair(lo, hi):
    lo_bits = lax.shift_right_logical(lax.bitcast_convert_type(lo.astype(BF16).astype(F32), jnp.int32), 16)
    hi_bits = lax.bitcast_convert_type(hi.astype(BF16).astype(F32), jnp.int32) & jnp.int32(-65536)
    return lo_bits | hi_bits


def _unpack_bf16_pair(words):
    lo = lax.bitcast_convert_type(lax.shift_left(words, 16), F32)
    hi = lax.bitcast_convert_type(words & jnp.int32(-65536), F32)
    return lo, hi


def _proj_residual_ln(acts, weights, resid, g, b, tm=256):
    m, d = resid.shape
    tm = min(tm, m)
    n_in = len(acts)
    in_specs = [pl.BlockSpec((tm, a.shape[1]), lambda i: (i, 0)) for a in acts]
    in_specs += [pl.BlockSpec(w.shape, lambda i: (0, 0)) for w in weights]
    in_specs += [pl.BlockSpec((tm, d), lambda i: (i, 0)),
                 pl.BlockSpec((1, d), lambda i: (0, 0)),
                 pl.BlockSpec((1, d), lambda i: (0, 0))]
    return pl.pallas_call(
        functools.partial(_proj_ln_kernel, n_in=n_in),
        grid=(m // tm,),
        in_specs=in_specs,
        out_specs=[pl.BlockSpec((tm, d), lambda i: (i, 0)), pl.BlockSpec((tm, d // 2), lambda i: (i, 0))],
        out_shape=[jax.ShapeDtypeStruct((m, d), F32), jax.ShapeDtypeStruct((m, d // 2), jnp.int32)],
        compiler_params=_params("parallel"),
        name="out_proj_ln",
    )(*acts, *weights, resid, g.reshape(1, d), b.reshape(1, d))


def _t5_bucket(dist):
    max_exact = REL_BUCKETS // 2
    d = np.maximum(dist, 1).astype(np.float64)
    large = max_exact + (np.log(d / max_exact) / np.log(REL_MAX_DIST / max_exact)
                         * (REL_BUCKETS - max_exact)).astype(np.int64)
    large = np.minimum(large, REL_BUCKETS - 1)
    return np.where(dist < max_exact, dist, large).astype(np.int32)


def _dilated_kernel(q_ref, k_ref, v_ref, bucket_ref, rel_bias_ref, out_ref, o_scr, lse_scr):
    blk = A_BLOCK
    seq = q_ref.shape[1]
    head = pl.program_id(1)
    qi = lax.broadcasted_iota(jnp.int32, (blk, 2 * blk), 0)
    ki = lax.broadcasted_iota(jnp.int32, (blk, 2 * blk), 1)
    rel = qi + blk - ki

    for bi, (window, dil) in enumerate(A_PATTERNS):
        band = window // dil
        n_blocks = seq // (dil * blk)
        valid = jnp.abs(2 * rel - band) <= band
        bucket = bucket_ref[bi]
        bias = jnp.zeros((blk, 2 * blk), F32)
        for b in range(REL_BUCKETS):
            bias = jnp.where(bucket == b, rel_bias_ref[head, b], bias)

        def attend(q_start, k_start, n_keys, bias_blk, valid_blk, bi=bi, dil=dil):
            q_rows = pl.ds(q_start, blk, stride=dil)
            k_rows = pl.ds(k_start, n_keys, stride=dil)
            q = q_ref[0, q_rows, :].astype(BF16)
            kk = k_ref[0, k_rows, :].astype(BF16)
            vv = v_ref[0, k_rows, :].astype(BF16)
            s = _dot_nt(q, kk) + bias_blk
            s = jnp.where(valid_blk, s, NEG_INF)
            m = jnp.max(s, axis=-1, keepdims=True)
            p = jnp.exp(s - m)
            l = jnp.sum(p, axis=-1, keepdims=True)
            o_scr[bi, q_rows, :] = _dot(p.astype(BF16), vv) / l
            lse_scr[bi, q_rows, :] = jnp.broadcast_to(m + jnp.log(l), (blk, HEAD_DIM))

        def first(r, carry, attend=attend, bias=bias, valid=valid):
            attend(r, r, blk, bias[:, blk:], valid[:, blk:])
            return carry

        lax.fori_loop(0, dil, first, 0, unroll=min(dil, DILATED_UNROLL))

        def rest(t, carry, attend=attend, bias=bias, valid=valid, dil=dil):
            r = t & (dil - 1)
            n = (t >> (dil.bit_length() - 1)) + 1
            attend(r + n * blk * dil, r + (n - 1) * blk * dil, 2 * blk, bias, valid)
            return carry

        lax.fori_loop(0, dil * (n_blocks - 1), rest, 0, unroll=DILATED_UNROLL)

    chunk = 2 * blk
    nb = len(A_PATTERNS)

    def mix(c, carry):
        rows = pl.ds(pl.multiple_of(c * chunk, chunk), chunk)
        lses = [lse_scr[b, rows, :] for b in range(nb)]
        m = functools.reduce(jnp.maximum, lses)
        es = [jnp.exp(x - m) for x in lses]
        den = functools.reduce(lambda a, b: a + b, es)
        num = functools.reduce(lambda a, b: a + b, [e * o_scr[b, rows, :] for b, e in enumerate(es)])
        out_ref[0, rows, :] = (num / den).astype(out_ref.dtype)
        return carry

    lax.fori_loop(0, seq // chunk, mix, 0)


def _dilated_bucket_tables():
    qi = np.arange(A_BLOCK)[:, None]
    ki = np.arange(2 * A_BLOCK)[None, :]
    rel = qi + A_BLOCK - ki
    return np.stack([_t5_bucket(np.clip(rel, 0, None) * dilation) for _, dilation in A_PATTERNS], axis=0)


def _dilated_mixture(proj, rel_bias, bsz, seq, heads, proj_width):
    for window, dilation in A_PATTERNS:
        assert seq % (dilation * A_BLOCK) == 0 and window // dilation <= A_BLOCK
    nb = len(A_PATTERNS)
    view = proj.reshape(bsz, seq, proj_width)
    buckets = jnp.asarray(_dilated_bucket_tables())

    def in_spec(group):
        return pl.BlockSpec((1, seq, HEAD_DIM), lambda b, h, g=group: (b, 0, g * heads + h))

    o = pl.pallas_call(
        _dilated_kernel,
        grid=(bsz, heads),
        in_specs=[in_spec(0), in_spec(1), in_spec(2),
                  pl.BlockSpec(buckets.shape, lambda b, h: (0, 0, 0)),
                  pl.BlockSpec(memory_space=pltpu.SMEM)],
        out_specs=pl.BlockSpec((1, seq, HEAD_DIM), lambda b, h: (b, 0, h)),
        out_shape=jax.ShapeDtypeStruct((bsz, seq, heads * HEAD_DIM), BF16),
        scratch_shapes=[pltpu.VMEM((nb, seq, HEAD_DIM), F32), pltpu.VMEM((nb, seq, HEAD_DIM), F32)],
        compiler_params=_params("parallel", "parallel"),
        name="dilated_attention",
    )(view, view, view, buckets, rel_bias.astype(F32).T)
    return o.reshape(bsz * seq, heads * HEAD_DIM)


def _hgrn_levels(t):
    cs = []
    c = t // 2
    while c >= 1:
        cs.append(c)
        c //= 2
    return cs


def _hgrn_consts(t):
    idx = np.arange(t)
    ltri = (idx[:, None] >= idx[None, :]).astype(np.float32)
    sels = []
    for c in _hgrn_levels(t):
        if c < HGRN_BCAST_MIN:
            ref = (idx // (2 * c)) * (2 * c) + c - 1
            sels.append((idx[None, :] == ref[:, None]).astype(np.float32))
    return jnp.asarray(ltri, BF16), jnp.asarray(np.concatenate(sels, axis=0), BF16)


def _hgrn_kernel(q_ref, f_ref, i_ref, g_ref, lb_ref, gn_ref, ltri_ref, sel_ref, o_ref, state_ref,
                 *, n_chunks):
    t = HGRN_CHUNK
    dk = HEAD_DIM
    levels = _hgrn_levels(t)
    lb = lb_ref[0]
    gn = gn_ref[0]
    ltri = ltri_ref[...]
    sel = sel_ref[...]
    row = lax.broadcasted_iota(jnp.int32, (t, dk), 0)
    ti = lax.broadcasted_iota(jnp.int32, (t, t), 0)
    si = lax.broadcasted_iota(jnp.int32, (t, t), 1)
    state_ref[...] = jnp.zeros_like(state_ref)

    def sum3(y):
        return y[:, :dk] + y[:, dk:2 * dk] + y[:, 2 * dk:]

    def chunk(n, carry):
        st = pl.multiple_of(n * t, t)
        fq = _silu(q_ref[0, pl.ds(st, t), :])
        forget = lb + (1.0 - lb) * _sigmoid(f_ref[0, pl.ds(st, t), :])
        logf = jnp.log(forget)
        key = 1.0 - forget
        val = i_ref[0, pl.ds(st, t), :].astype(BF16)
        a = sum3(_dot(ltri, _split3(logf, 1)))
        aref_fine = sum3(_dot(sel, _split3(a, 1)))
        fq_b = fq.astype(BF16)
        key_b = key.astype(BF16)
        scores = jnp.where(ti == si, _dot_nt(fq_b, key_b), 0.0)
        n_fine = 0
        for c in levels:
            if c >= HGRN_BCAST_MIN:
                aref = jnp.concatenate(
                    [jnp.broadcast_to(a[b0 + c - 1:b0 + c, :], (2 * c, dk)) for b0 in range(0, t, 2 * c)],
                    axis=0)
            else:
                aref = aref_fine[n_fine * t:(n_fine + 1) * t, :]
                n_fine += 1
            upper = (row & (2 * c - 1)) >= c
            decayed = jnp.where(upper, fq, key) * jnp.exp(-jnp.abs(a - aref))
            ql = jnp.where(upper, decayed, 0.0).astype(BF16)
            kl = jnp.where(upper, 0.0, decayed).astype(BF16)
            s_l = _dot_nt(ql, kl)
            if 2 * c < t:
                sh = (2 * c).bit_length() - 1
                s_l = jnp.where((ti >> sh) == (si >> sh), s_l, 0.0)
            scores = scores + s_l
        state_t = state_ref[...]
        inter = _dot_nt((fq * jnp.exp(a)).astype(BF16), state_t.astype(BF16))
        o = inter + _dot(scores.astype(BF16), val)
        a_last = a[t - 1:t, :]
        k_dec = (key * jnp.exp(a_last - a)).astype(BF16)
        state_ref[...] = state_t * jnp.exp(a_last) + _dot_tn(val, k_dec)
        o = o * lax.rsqrt(jnp.mean(o * o, axis=-1, keepdims=True) + RMS_EPS) * gn
        o = o * _silu(g_ref[0, pl.ds(st, t), :])
        o_ref[0, pl.ds(st, t), :] = o.astype(o_ref.dtype)
        return carry

    lax.fori_loop(0, n_chunks, chunk, 0, unroll=16)


def _hgrn2(proj, lb, gn_g, bsz, seq, heads, proj_width, first_group):
    assert seq % HGRN_CHUNK == 0
    view = proj.reshape(bsz, seq, proj_width)
    ltri, sel = _hgrn_consts(HGRN_CHUNK)

    def in_spec(group):
        return pl.BlockSpec((1, seq, HEAD_DIM), lambda b, h, g=group: (b, 0, (first_group + g) * heads + h))

    head_vec = pl.BlockSpec((1, 1, HEAD_DIM), lambda b, h: (h, 0, 0))
    o = pl.pallas_call(
        functools.partial(_hgrn_kernel, n_chunks=seq // HGRN_CHUNK),
        grid=(bsz, heads),
        in_specs=[in_spec(0), in_spec(1), in_spec(2), in_spec(3), head_vec, head_vec,
                  pl.BlockSpec(ltri.shape, lambda b, h: (0, 0)),
                  pl.BlockSpec(sel.shape, lambda b, h: (0, 0))],
        out_specs=pl.BlockSpec((1, seq, HEAD_DIM), lambda b, h: (b, 0, h)),
        out_shape=jax.ShapeDtypeStruct((bsz, seq, heads * HEAD_DIM), BF16),
        scratch_shapes=[pltpu.VMEM((HEAD_DIM, HEAD_DIM), F32)],
        compiler_params=_params("parallel", "parallel"),
        name="hgrn2",
    )(view, view, view, view, lb.reshape(heads, 1, HEAD_DIM).astype(F32),
      gn_g.reshape(heads, 1, HEAD_DIM).astype(F32), ltri, sel)
    return o.reshape(bsz * seq, heads * HEAD_DIM)


def _fox_gate_kernel(x_ref, w_ref, b_ref, ltri_ref, place_q_ref, place_k_ref, ones_q_ref, ones_k_ref,
                     qx_ref, kx_ref, carry):
    heads = w_ref.shape[1]
    ts = x_ref.shape[1]

    @pl.when(pl.program_id(1) == 0)
    def _():
        carry[...] = jnp.zeros_like(carry)

    def log_sigmoid(z):
        return jnp.minimum(z, 0.0) - jnp.log(1.0 + jnp.exp(-jnp.abs(z)))

    x = x_ref[0]
    xh = x.astype(BF16)
    xl = (x - xh.astype(F32)).astype(BF16)
    w = w_ref[...]
    wh = w.astype(BF16)
    wl = (w - wh.astype(F32)).astype(BF16)
    f = _dot(xh, wh) + _dot(xh, wl) + _dot(xl, wh) + b_ref[...]
    c = _dot(ltri_ref[...], _split3(log_sigmoid(f), 1))
    c = c[:, :heads] + c[:, heads:2 * heads] + c[:, 2 * heads:] + carry[...]
    carry[...] = c[ts - 1:ts, :]
    terms = _split3(c * LOG2_E, 1)
    qx_ref[0] = (_dot(terms, place_q_ref[...]) + ones_q_ref[...]).astype(BF16)
    kx_ref[0] = (_dot(terms, place_k_ref[...]) + ones_k_ref[...]).astype(BF16)


def _fox_gate_placement(heads):
    place_q = np.zeros((3 * heads, heads * HEAD_DIM), np.float32)
    place_k = np.zeros((3 * heads, heads * HEAD_DIM), np.float32)
    ones_q = np.zeros((1, heads * HEAD_DIM), np.float32)
    ones_k = np.zeros((1, heads * HEAD_DIM), np.float32)
    for h in range(heads):
        for j in range(3):
            place_q[j * heads + h, h * HEAD_DIM + j] = 1.0
            place_k[j * heads + h, h * HEAD_DIM + 3 + j] = -1.0
            ones_q[0, h * HEAD_DIM + 3 + j] = 1.0
            ones_k[0, h * HEAD_DIM + j] = 1.0
    return jnp.asarray(place_q, BF16), jnp.asarray(place_k, BF16), jnp.asarray(ones_q), jnp.asarray(ones_k)


def _fox_gate_terms(x, w_f, b_f, bsz, seq, ts=512):
    d = x.shape[-1]
    heads = w_f.shape[1]
    ts = min(ts, seq)
    idx = np.arange(ts)
    ltri = jnp.asarray(idx[:, None] >= idx[None, :], BF16)
    consts = _fox_gate_placement(heads)
    const = lambda shape: pl.BlockSpec(shape, lambda b, i: (0,) * len(shape))
    out_spec = pl.BlockSpec((1, ts, heads * HEAD_DIM), lambda b, i: (b, i, 0))
    out_sds = jax.ShapeDtypeStruct((bsz, seq, heads * HEAD_DIM), BF16)
    return pl.pallas_call(
        _fox_gate_kernel,
        grid=(bsz, seq // ts),
        in_specs=[pl.BlockSpec((1, ts, d), lambda b, i: (b, i, 0)),
                  const((d, heads)), const((1, heads)), const((ts, ts))] + [const(a.shape) for a in consts],
        out_specs=[out_spec, out_spec],
        out_shape=[out_sds, out_sds],
        scratch_shapes=[pltpu.VMEM((1, heads), F32)],
        compiler_params=_params("parallel", "arbitrary"),
        name="fox_gate_cumsum",
    )(x.reshape(bsz, seq, d), w_f.astype(F32), b_f.astype(F32).reshape(1, heads), ltri, *consts)


def _fox_kernel(q_ref, qx_ref, k_ref, kx_ref, vt_ref, o_ref, s_a, s_b, p_a, p_b, acc_ref):
    tq, tk = FOX_Q_TILE, FOX_K_TILE
    assert tq == 2 * tk
    for i in range(q_ref.shape[1] // tq):
        _fox_query_tile(i, q_ref, qx_ref, k_ref, kx_ref, vt_ref, o_ref, s_a, s_b, p_a, p_b, acc_ref)


def _fox_query_tile(i, q_ref, qx_ref, k_ref, kx_ref, vt_ref, o_ref, s_a, s_b, p_a, p_b, acc_ref):
    tq, tk = FOX_Q_TILE, FOX_K_TILE
    lanes = HEAD_DIM
    rows = slice(i * tq, (i + 1) * tq)
    q = jnp.concatenate([q_ref[0, rows, :], qx_ref[0, rows, :]], axis=1)

    def logits_into(j, s_ref):
        st = pl.multiple_of(j * tk, tk)
        kk = jnp.concatenate([k_ref[0, pl.ds(st, tk), :], kx_ref[0, pl.ds(st, tk), :]], axis=1)
        s_ref[...] = _dot_nt(kk, q)

    def softmax_pv(j, s_ref, p_ref, m, l, diag):
        m_out, l_out, alphas = [], [], []
        for c in range(tq // lanes):
            cols = slice(c * lanes, (c + 1) * lanes)
            if diag is not None and (c + 1) * lanes - 1 < diag * tk:
                m_out.append(m[:, cols])
                l_out.append(l[:, cols])
                alphas.append(jnp.ones((1, lanes), F32))
                p_ref[:, cols] = jnp.zeros((tk, lanes), BF16)
                continue
            s = s_ref[:, cols]
            if diag is not None and c * lanes < diag * tk + tk - 1:
                key = lax.broadcasted_iota(jnp.int32, (tk, lanes), 0) + diag * tk
                qry = lax.broadcasted_iota(jnp.int32, (tk, lanes), 1) + c * lanes
                s = jnp.where(key <= qry, s, NEG_INF)
            m_new = jnp.maximum(m[:, cols], jnp.max(s, axis=0, keepdims=True))
            alpha = jnp.exp2(m[:, cols] - m_new)
            p = jnp.exp2(s - m_new)
            l_out.append(alpha * l[:, cols] + jnp.sum(p, axis=0, keepdims=True))
            m_out.append(m_new)
            alphas.append(alpha)
            p_ref[:, cols] = p.astype(BF16)
        vt = vt_ref[0, :, pl.ds(pl.multiple_of(j * tk, tk), tk)]
        acc_ref[...] = jnp.concatenate(alphas, axis=1) * acc_ref[...] + _dot(vt, p_ref[...])
        return jnp.concatenate(m_out, axis=1), jnp.concatenate(l_out, axis=1)

    def pair(jp, carry):
        m, l = carry
        t0 = 2 * jp
        logits_into(t0 + 1, s_b)
        m, l = softmax_pv(t0, s_a, p_a, m, l, None)
        logits_into(t0 + 2, s_a)
        return softmax_pv(t0 + 1, s_b, p_b, m, l, None)

    acc_ref[...] = jnp.zeros_like(acc_ref)
    logits_into(0, s_a)
    m, l = lax.fori_loop(0, i, pair, (jnp.full((1, tq), NEG_INF, F32), jnp.zeros((1, tq), F32)),
                         unroll=min(max(i, 1), FOX_PAIR_UNROLL))
    t0 = 2 * i
    logits_into(t0 + 1, s_b)
    m, l = softmax_pv(t0, s_a, p_a, m, l, 0)
    m, l = softmax_pv(t0 + 1, s_b, p_b, m, l, 1)
    o_ref[0, rows, :] = jnp.transpose(acc_ref[...] / l).astype(o_ref.dtype)


def _fox_attention(qkv, qx, kx, bsz, seq, heads):
    tq = FOX_Q_TILE
    assert seq % tq == 0 and tq % FOX_K_TILE == 0
    width = heads * HEAD_DIM
    view = qkv.reshape(bsz, seq, 3 * width)
    v_t = jnp.swapaxes(view[:, :, 2 * width:], 1, 2)
    head_spec = lambda off: pl.BlockSpec((1, seq, HEAD_DIM), lambda b, h: (b, 0, off + h))
    o = pl.pallas_call(
        _fox_kernel,
        grid=(bsz, heads),
        in_specs=[head_spec(0), head_spec(0), head_spec(heads), head_spec(0),
                  pl.BlockSpec((1, HEAD_DIM, seq), lambda b, h: (b, h, 0))],
        out_specs=head_spec(0),
        out_shape=jax.ShapeDtypeStruct((bsz, seq, width), BF16),
        scratch_shapes=[pltpu.VMEM((FOX_K_TILE, tq), F32), pltpu.VMEM((FOX_K_TILE, tq), F32),
                        pltpu.VMEM((FOX_K_TILE, tq), BF16), pltpu.VMEM((FOX_K_TILE, tq), BF16),
                        pltpu.VMEM((HEAD_DIM, tq), F32)],
        compiler_params=_params("parallel", "parallel"),
        name="fox_attention",
    )(view, qx, view, kx, v_t)
    return o.reshape(bsz * seq, width)


def _router_kernel(x_ref, wt_ref, bias_ref, utri_ref, eidx_ref, rank_ref, w_ref, cnt_ref, carry):
    gsz = GROUP_SIZE
    tt = x_ref.shape[0]

    @pl.when(pl.program_id(0) == 0)
    def _():
        carry[...] = jnp.zeros_like(carry)

    x = x_ref[...]
    xh = x.astype(BF16)
    xl = (x - xh.astype(F32)).astype(BF16)
    wt = wt_ref[...]
    wh = wt.astype(BF16)
    wl = (wt - wh.astype(F32)).astype(BF16)
    logits = _dot_nt(wh, xh) + _dot_nt(wh, xl) + _dot_nt(wl, xh)
    scores = _sigmoid(logits)
    sel = scores + bias_ref[...]
    sub = lax.broadcasted_iota(jnp.int32, (gsz, tt), 0)
    neg = -jnp.inf

    group_score = []
    for g in range(N_GROUPS):
        blk = sel[g * gsz:(g + 1) * gsz, :]
        m1 = jnp.max(blk, axis=0, keepdims=True)
        first = jnp.min(jnp.where(blk == m1, sub, gsz), axis=0, keepdims=True)
        m2 = jnp.max(jnp.where(sub == first, neg, blk), axis=0, keepdims=True)
        group_score.append(m1 + m2)

    chosen = [jnp.zeros((1, tt), F32) for _ in range(N_GROUPS)]
    for _ in range(TOPK_GROUPS):
        m = functools.reduce(jnp.maximum, group_score)
        gi = jnp.full((1, tt), N_GROUPS, jnp.int32)
        for g in reversed(range(N_GROUPS)):
            gi = jnp.where(group_score[g] == m, g, gi)
        for g in range(N_GROUPS):
            hit = gi == g
            chosen[g] = jnp.where(hit, 1.0, chosen[g])
            group_score[g] = jnp.where(hit, neg, group_score[g])

    cand = [jnp.where(jnp.broadcast_to(chosen[g], (gsz, tt)) > 0.5, sel[g * gsz:(g + 1) * gsz, :], NEG_INF)
            for g in range(N_GROUPS)]
    score_g = [scores[g * gsz:(g + 1) * gsz, :] for g in range(N_GROUPS)]
    eiota = [sub + g * gsz for g in range(N_GROUPS)]
    picked = [jnp.zeros((gsz, tt), F32) for _ in range(N_GROUPS)]
    top_idx, top_score = [], []
    for _ in range(TOP_K):
        m = functools.reduce(jnp.maximum, [jnp.max(c, axis=0, keepdims=True) for c in cand])
        ei = functools.reduce(jnp.minimum, [
            jnp.min(jnp.where(cand[g] == m, eiota[g], N_EXPERTS), axis=0, keepdims=True)
            for g in range(N_GROUPS)])
        sc = jnp.zeros((1, tt), F32)
        for g in range(N_GROUPS):
            hit = eiota[g] == ei
            picked[g] = jnp.where(hit, 1.0, picked[g])
            cand[g] = jnp.where(hit, neg, cand[g])
            sc = sc + jnp.sum(jnp.where(hit, score_g[g], 0.0), axis=0, keepdims=True)
        top_idx.append(ei)
        top_score.append(sc)

    total = functools.reduce(lambda a, b: a + b, top_score)
    mask = jnp.concatenate(picked, axis=0)
    incl = _dot(mask.astype(BF16), utri_ref[...])
    rank = carry[...] + incl - mask
    for r in range(TOP_K):
        ei = top_idx[r]
        rk = jnp.zeros((1, tt), F32)
        for g in range(N_GROUPS):
            rk = rk + jnp.sum(jnp.where(eiota[g] == ei, rank[g * gsz:(g + 1) * gsz, :], 0.0),
                              axis=0, keepdims=True)
        eidx_ref[r:r + 1, :] = ei
        rank_ref[r:r + 1, :] = rk.astype(jnp.int32)
        w_ref[r:r + 1, :] = top_score[r] / total * ROUTED_SCALE
    new_carry = carry[...] + incl[:, tt - 1:tt]
    carry[...] = new_carry
    cnt_ref[...] = jnp.broadcast_to(new_carry, cnt_ref.shape)


def _router(x, router_w, router_bias):
    n, d = x.shape
    tt = min(ROUTER_TILE, n)
    idx = np.arange(tt)
    utri = jnp.asarray(idx[:, None] <= idx[None, :], BF16)
    eidx, rank, w8, cnt = pl.pallas_call(
        _router_kernel,
        grid=(n // tt,),
        in_specs=[pl.BlockSpec((tt, d), lambda i: (i, 0)),
                  pl.BlockSpec((N_EXPERTS, d), lambda i: (0, 0)),
                  pl.BlockSpec((N_EXPERTS, 1), lambda i: (0, 0)),
                  pl.BlockSpec((tt, tt), lambda i: (0, 0))],
        out_specs=[pl.BlockSpec((TOP_K, tt), lambda i: (0, i)),
                   pl.BlockSpec((TOP_K, tt), lambda i: (0, i)),
                   pl.BlockSpec((TOP_K, tt), lambda i: (0, i)),
                   pl.BlockSpec((N_EXPERTS, 128), lambda i: (0, 0))],
        out_shape=[jax.ShapeDtypeStruct((TOP_K, n), jnp.int32),
                   jax.ShapeDtypeStruct((TOP_K, n), jnp.int32),
                   jax.ShapeDtypeStruct((TOP_K, n), F32),
                   jax.ShapeDtypeStruct((N_EXPERTS, 128), F32)],
        scratch_shapes=[pltpu.VMEM((N_EXPERTS, 1), F32)],
        compiler_params=_params("arbitrary"),
        name="moe_router",
    )(x, router_w.astype(F32).T, router_bias.astype(F32).reshape(N_EXPERTS, 1), utri)
    return eidx, rank, w8, cnt[:, 0].astype(jnp.int32)


def _expert_kernel(te_ref, tv_ref, nt_ref, src_ref, x_hbm, wg_ref, wu_ref, wd_ref, y_ref,
                   xbuf_a, xbuf_b, sem_a, sem_b, wg_bf, wu_bf, wd_bf):
    i = pl.program_id(0)
    n_live = nt_ref[0]
    tm, half = xbuf_a.shape
    bufs = ((xbuf_a, sem_a), (xbuf_b, sem_b))
    prev = jnp.maximum(i - 1, 0)

    def copy(buf, sem, r):
        return pltpu.make_async_copy(x_hbm.at[pl.ds(src_ref[0, 0, r], 1), :], buf.at[pl.ds(r, 1), :], sem)

    def ffn(buf, sem, gather=None):
        pltpu.make_async_copy(buf, buf, sem).wait()
        if gather is not None:
            for r in range(tm):
                copy(*gather, r).start()
        row = lax.broadcasted_iota(jnp.int32, (tm, 1), 0)
        x_lo, x_hi = _unpack_bf16_pair(jnp.where(row < tv_ref[prev], buf[...], 0))
        x_lo, x_hi = x_lo.astype(BF16), x_hi.astype(BF16)
        gate = _dot(x_lo, wg_bf[:half, :]) + _dot(x_hi, wg_bf[half:, :])
        up = _dot(x_lo, wu_bf[:half, :]) + _dot(x_hi, wu_bf[half:, :])
        hidden = (_silu(gate) * up).astype(BF16)
        y = _dot(hidden, wd_bf[...])
        y_ref[...] = _pack_bf16_pair(y[:, :half], y[:, half:])

    @pl.when(i == 0)
    def _():
        def issue(r, carry):
            copy(*bufs[0], r).start()
            return carry
        lax.fori_loop(0, tm, issue, 0)

    computing = jnp.logical_and(i >= 1, i <= n_live)
    new_expert = jnp.logical_or(i == 1, te_ref[prev] != te_ref[jnp.maximum(i - 2, 0)])

    @pl.when(jnp.logical_and(computing, new_expert))
    def _():
        wg_bf[...] = wg_ref[0, 0].astype(BF16)
        wu_bf[...] = wu_ref[0, 0].astype(BF16)
        wd_bf[...] = wd_ref[0, 0].astype(BF16)

    for parity in range(2):
        @pl.when(jnp.logical_and(jnp.logical_and(i >= 1, i < n_live), i % 2 == parity))
        def _(parity=parity):
            ffn(*bufs[1 - parity], gather=bufs[parity])

        @pl.when(jnp.logical_and(i == n_live, (i - 1) % 2 == parity))
        def _(parity=parity):
            ffn(*bufs[parity])


def _expert_ffn(x_packed, src_table, w_gate, w_up, w_down, layer, tile_expert, tile_valid, num_tiles):
    n, half = x_packed.shape
    d = 2 * half
    max_tiles, _, tm = src_table.shape
    e_dim = w_gate.shape[3]
    prev = lambda i, nt: jnp.minimum(jnp.maximum(i - 1, 0), nt[0] - 1)
    w_map = lambda i, te, tv, nt: (layer, te[prev(i, nt)], 0, 0)
    xbuf = pltpu.VMEM((tm, half), jnp.int32)
    return pl.pallas_call(
        _expert_kernel,
        grid_spec=pltpu.PrefetchScalarGridSpec(
            num_scalar_prefetch=3,
            grid=(max_tiles + 1,),
            in_specs=[pl.BlockSpec((1, 1, tm), lambda i, te, tv, nt: (jnp.minimum(i, nt[0] - 1), 0, 0),
                                   memory_space=pltpu.SMEM),
                      pl.BlockSpec(memory_space=pl.ANY),
                      pl.BlockSpec((1, 1, d, e_dim), w_map),
                      pl.BlockSpec((1, 1, d, e_dim), w_map),
                      pl.BlockSpec((1, 1, e_dim, d), w_map)],
            out_specs=pl.BlockSpec((tm, half), lambda i, te, tv, nt: (prev(i, nt), 0)),
            scratch_shapes=[xbuf, xbuf, pltpu.SemaphoreType.DMA(()), pltpu.SemaphoreType.DMA(()),
                            pltpu.VMEM((d, e_dim), BF16), pltpu.VMEM((d, e_dim), BF16),
                            pltpu.VMEM((e_dim, d), BF16)]),
        out_shape=jax.ShapeDtypeStruct((max_tiles * tm, half), jnp.int32),
        compiler_params=_params("arbitrary"),
        name="moe_expert_ffn",
    )(tile_expert, tile_valid, num_tiles, src_table, x_packed, w_gate, w_up, w_down)


def _combine_kernel(dest_ref, x_ref, w8_ref, sg_ref, su_ref, sd_ref, g_ref, b_ref, y_hbm,
                    o_ref, obf_ref, ybuf_a, ybuf_b, sem_a, sem_b):
    tc = x_ref.shape[0]
    i = pl.program_id(0)
    n_tiles = pl.num_programs(0) - 1
    bufs = ((ybuf_a, sem_a), (ybuf_b, sem_b))

    def copy(buf, sem, t, k):
        return pltpu.make_async_copy(y_hbm.at[pl.ds(dest_ref[0, k, t], 1), :],
                                     buf.at[k, pl.ds(t, 1), :], sem)

    def finish(buf, sem, gather=None):
        x = x_ref[...]
        xb = x.astype(BF16)
        hidden = (_silu(_dot(xb, sg_ref[...])) * _dot(xb, su_ref[...])).astype(BF16)
        acc = _dot(hidden, sd_ref[...])
        pltpu.make_async_copy(buf, buf, sem).wait()
        if gather is not None:
            for t in range(tc):
                for k in range(TOP_K):
                    copy(*gather, t, k).start()
        w8 = w8_ref[...]
        half = buf.shape[2]
        routed_lo = jnp.zeros((tc, half), F32)
        routed_hi = jnp.zeros((tc, half), F32)
        for k in range(TOP_K):
            y_lo, y_hi = _unpack_bf16_pair(buf[k])
            routed_lo = routed_lo + w8[:, k:k + 1] * y_lo
            routed_hi = routed_hi + w8[:, k:k + 1] * y_hi
        acc = acc + jnp.concatenate([routed_lo, routed_hi], axis=1)
        y = _layer_norm_rows(DN_ALPHA * x + acc, g_ref[...], b_ref[...])
        o_ref[...] = y
        obf_ref[...] = y.astype(BF16)

    @pl.when(i == 0)
    def _():
        def issue(t, carry):
            for k in range(TOP_K):
                copy(*bufs[0], t, k).start()
            return carry
        lax.fori_loop(0, tc, issue, 0)

    for parity in range(2):
        @pl.when(jnp.logical_and(jnp.logical_and(i >= 1, i < n_tiles), i % 2 == parity))
        def _(parity=parity):
            finish(*bufs[1 - parity], gather=bufs[parity])

        @pl.when(jnp.logical_and(i == n_tiles, (i - 1) % 2 == parity))
        def _(parity=parity):
            finish(*bufs[parity])


def _combine(x, w8_rows, dest_tiles, y_sorted, s_gate, s_up, s_down, g, b):
    n, d = x.shape
    tc = dest_tiles.shape[2]
    sdim = s_gate.shape[1]
    const = lambda shape: pl.BlockSpec(shape, lambda i: (0,) * len(shape))
    n_tiles = n // tc
    prev = lambda i: (jnp.maximum(i - 1, 0), 0)
    row = pl.BlockSpec((tc, d), prev)
    ybuf = pltpu.VMEM((TOP_K, tc, d // 2), jnp.int32)
    return pl.pallas_call(
        _combine_kernel,
        grid=(n_tiles + 1,),
        in_specs=[pl.BlockSpec((1, TOP_K, tc), lambda i: (jnp.minimum(i, n_tiles - 1), 0, 0),
                               memory_space=pltpu.SMEM),
                  row, pl.BlockSpec((tc, TOP_K), prev),
                  const((d, sdim)), const((d, sdim)), const((sdim, d)), const((1, d)), const((1, d)),
                  pl.BlockSpec(memory_space=pl.ANY)],
        out_specs=[row, row],
        out_shape=[jax.ShapeDtypeStruct((n, d), F32), jax.ShapeDtypeStruct((n, d), BF16)],
        scratch_shapes=[ybuf, ybuf, pltpu.SemaphoreType.DMA(()), pltpu.SemaphoreType.DMA(())],
        compiler_params=_params("arbitrary"),
        name="moe_combine",
    )(dest_tiles, x, w8_rows, s_gate, s_up, s_down, g.reshape(1, d), b.reshape(1, d), y_sorted)


def _tile_major(a, tile):
    k, n = a.shape
    return a.reshape(k, n // tile, tile).transpose(1, 0, 2)


def _moe(x, x_packed, router_w, router_bias, w_gate, w_up, w_down, layer, s_gate, s_up, s_down, ln_g, ln_b):
    n, d = x.shape
    tm = EXPERT_TILE
    eidx, rank, w8, cnt = _router(x, router_w, router_bias)
    tiles_e = (cnt + tm - 1) // tm
    tile_end = jnp.cumsum(tiles_e)
    tile_start = tile_end - tiles_e
    row_start = tile_start * tm
    e_ids = jnp.arange(N_EXPERTS, dtype=jnp.int32)
    dest = jnp.sum(jnp.where(eidx[None] == e_ids[:, None, None], row_start[:, None, None], 0), axis=0) + rank
    max_tiles = (n * TOP_K) // tm + N_EXPERTS
    num_tiles = tile_end[-1]
    tile_ids = jnp.arange(max_tiles, dtype=jnp.int32)
    t_clamped = jnp.minimum(tile_ids, num_tiles - 1)
    tile_expert = jnp.sum(t_clamped[:, None] >= tile_end[None, :], axis=1).astype(jnp.int32)
    tile_valid = jnp.clip(cnt[tile_expert] - (t_clamped - tile_start[tile_expert]) * tm, 0, tm).astype(jnp.int32)

    tokens = jnp.tile(jnp.arange(n, dtype=jnp.int32), TOP_K)
    _, tok_sorted = lax.sort_key_val(dest.reshape(-1), tokens)
    tok_sorted = jnp.concatenate([tok_sorted, jnp.zeros((tm,), jnp.int32)])
    cnt_before = jnp.cumsum(cnt) - cnt
    first_slot = cnt_before[tile_expert] + (t_clamped - tile_start[tile_expert]) * tm
    window = first_slot[:, None] + jnp.arange(tm, dtype=jnp.int32)[None, :]
    src_table = tok_sorted.at[window].get(mode="promise_in_bounds")
    ys = _expert_ffn(x_packed, src_table.reshape(max_tiles, 1, tm), w_gate, w_up, w_down, layer,
                     tile_expert, tile_valid, num_tiles.reshape(1).astype(jnp.int32))
    return _combine(x, w8.T, _tile_major(dest, min(COMBINE_TILE, n)), ys, s_gate, s_up, s_down, ln_g, ln_b)


def _even_layer(x, x_bf, bsz, seq, w_in, gn_g, w_out, rel_bias, lb, ln_g, ln_b):
    heads = w_out.shape[0] // (2 * HEAD_DIM)
    width = heads * HEAD_DIM
    proj_width = w_in.shape[1]
    col_scale = jnp.concatenate([jnp.full((1, width), HEAD_DIM ** -0.5, F32),
                                 jnp.ones((1, proj_width - width), F32)], axis=1)
    proj = _matmul(x_bf, w_in, col_scale, F32)
    o_a = _dilated_mixture(proj, rel_bias, bsz, seq, heads, proj_width)
    o_b = _hgrn2(proj, lb, gn_g, bsz, seq, heads, proj_width, first_group=3)
    return _proj_residual_ln([o_a, o_b], [w_out[:width], w_out[width:]], x, ln_g, ln_b)


def _odd_layer(x, x_bf, bsz, seq, w_qkv, w_f, b_f, w_out, ln_g, ln_b):
    heads = w_f.shape[1]
    width = heads * HEAD_DIM
    col_scale = jnp.concatenate([jnp.full((1, width), LOG2_E * HEAD_DIM ** -0.5, F32),
                                 jnp.ones((1, 2 * width), F32)], axis=1)
    qkv = _matmul(x_bf, w_qkv, col_scale, BF16)
    qx, kx = _fox_gate_terms(x, w_f, b_f, bsz, seq)
    o = _fox_attention(qkv, qx, kx, bsz, seq, heads)
    return _proj_residual_ln([o], [w_out], x, ln_g, ln_b)


def kernel(x, rel_bias, hgrn_lb_logits, a_w_in, a_gn_g, a_w_out, c_w_in, c_b_f, c_w_out, ln_mix_g, ln_mix_b, ln_ffn_g, ln_ffn_b, router_w, router_bias, exp_w_gate, exp_w_up, exp_w_down, sh_w_gate, sh_w_up, sh_w_down):
    bsz, seq, d = x.shape
    depth = ln_mix_g.shape[0]
    lb_all = jnp.cumsum(jax.nn.softmax(hgrn_lb_logits.astype(F32), axis=0), axis=0)
    h = x.reshape(bsz * seq, d).astype(F32)
    h_bf = h.astype(BF16)
    for layer in range(depth):
        j = layer // 2
        if layer % 2 == 0:
            h, h_packed = _even_layer(h, h_bf, bsz, seq, a_w_in[j].astype(BF16), a_gn_g[j],
                                  a_w_out[j].astype(BF16), rel_bias, lb_all[layer],
                                  ln_mix_g[layer], ln_mix_b[layer])
        else:
            c_width = c_w_out.shape[1]
            w_in = c_w_in[j]
            h, h_packed = _odd_layer(h, h_bf, bsz, seq, w_in[:, :3 * c_width].astype(BF16),
                                 w_in[:, 3 * c_width:], c_b_f[j], c_w_out[j].astype(BF16),
                                 ln_mix_g[layer], ln_mix_b[layer])
        h, h_bf = _moe(h, h_packed, router_w[layer], router_bias[layer], exp_w_gate, exp_w_up, exp_w_down, layer,
                       sh_w_gate[layer].astype(BF16), sh_w_up[layer].astype(BF16),
                       sh_w_down[layer].astype(BF16), ln_ffn_g[layer], ln_ffn_b[layer])
    return h.reshape(bsz, seq, d)
```

```python
import functools

import numpy as np
import jax
import jax.numpy as jnp
from jax import lax
from jax.experimental import pallas as pl
from jax.experimental.pallas import tpu as pltpu

F32 = jnp.float32
BF16 = jnp.bfloat16

HEAD_DIM = 128
A_PATTERNS = ((128, 1), (512, 4), (2048, 16))
A_BLOCK = 128
REL_BUCKETS = 32
REL_MAX_DIST = 2048
N_EXPERTS = 64
TOP_K = 8
N_GROUPS = 8
GROUP_SIZE = N_EXPERTS // N_GROUPS
TOPK_GROUPS = 4
ROUTED_SCALE = 2.5
LN_EPS = 1e-5
RMS_EPS = 1e-6
DEPTH = 2
DN_ALPHA = (2 * DEPTH) ** 0.25
NEG_INF = -1e30
LOG2_E = 1.4426950408889634

DILATED_UNROLL = 16
HGRN_CHUNK = 128
HGRN_BCAST_MIN = 8
FOX_Q_TILE = 512
FOX_K_TILE = 256
FOX_PAIR_UNROLL = 4
ROUTER_TILE = 512
EXPERT_TILE = 512
COMBINE_TILE = 128
VMEM_LIMIT = 56 * 1024 * 1024

_NT = (((1,), (1,)), ((), ()))
_TN = (((0,), (0,)), ((), ()))


def _dot(a, b):
    return jnp.dot(a, b, preferred_element_type=F32)


def _dot_nt(a, b):
    return lax.dot_general(a, b, _NT, preferred_element_type=F32)


def _dot_tn(a, b):
    return lax.dot_general(a, b, _TN, preferred_element_type=F32)


def _sigmoid(x):
    return 1.0 / (1.0 + jnp.exp(-x))


def _silu(x):
    return x * _sigmoid(x)


def _split3(x, axis):
    hi = x.astype(BF16)
    r1 = x - hi.astype(F32)
    mid = r1.astype(BF16)
    lo = (r1 - mid.astype(F32)).astype(BF16)
    return jnp.concatenate([hi, mid, lo], axis=axis)


def _layer_norm_rows(y, g, b):
    mu = jnp.mean(y, axis=-1, keepdims=True)
    yc = y - mu
    var = jnp.mean(yc * yc, axis=-1, keepdims=True)
    return yc * lax.rsqrt(var + LN_EPS) * g + b


def _params(*sem):
    return pltpu.CompilerParams(dimension_semantics=sem, vmem_limit_bytes=VMEM_LIMIT)


def _mm_kernel(x_ref, w_ref, cs_ref, o_ref):
    o_ref[...] = (_dot(x_ref[...], w_ref[...]) * cs_ref[...]).astype(o_ref.dtype)


def _matmul(x, w, col_scale, out_dtype, tm=1024, tn=1024):
    m, k = x.shape
    nw = w.shape[1]
    tm, tn = min(tm, m), min(tn, nw)
    assert m % tm == 0 and nw % tn == 0
    return pl.pallas_call(
        _mm_kernel,
        grid=(nw // tn, m // tm),
        in_specs=[pl.BlockSpec((tm, k), lambda j, i: (i, 0)),
                  pl.BlockSpec((k, tn), lambda j, i: (0, j)),
                  pl.BlockSpec((1, tn), lambda j, i: (0, j))],
        out_specs=pl.BlockSpec((tm, tn), lambda j, i: (i, j)),
        out_shape=jax.ShapeDtypeStruct((m, nw), out_dtype),
        compiler_params=_params("parallel", "parallel"),
        name="proj_matmul",
    )(x, w, col_scale)


def _proj_ln_kernel(*refs, n_in):
    a_refs, w_refs = refs[:n_in], refs[n_in:2 * n_in]
    r_ref, g_ref, b_ref, o_ref, obf_ref = refs[2 * n_in:]
    acc = _dot(a_refs[0][...], w_refs[0][...])
    for a_ref, w_ref in zip(a_refs[1:], w_refs[1:]):
        acc = acc + _dot(a_ref[...], w_ref[...])
    y = _layer_norm_rows(DN_ALPHA * r_ref[...] + acc, g_ref[...], b_ref[...])
    o_ref[...] = y
    half = y.shape[1] // 2
    obf_ref[...] = _pack_bf16_pair(y[:, :half], y[:, half:])


def _pack_bf16_pair(lo, hi):
    lo_bits = lax.shift_right_logical(lax.bitcast_convert_type(lo.astype(BF16).astype(F32), jnp.int32), 16)
    hi_bits = lax.bitcast_convert_type(hi.astype(BF16).astype(F32), jnp.int32) & jnp.int32(-65536)
    return lo_bits | hi_bits


def _unpack_bf16_pair(words):
    lo = lax.bitcast_convert_type(lax.shift_left(words, 16), F32)
    hi = lax.bitcast_convert_type(words & jnp.int32(-65536), F32)
    return lo, hi


def _proj_residual_ln(acts, weights, resid, g, b, tm=256):
    m, d = resid.shape
    tm = min(tm, m)
    n_in = len(acts)
    in_specs = [pl.BlockSpec((tm, a.shape[1]), lambda i: (i, 0)) for a in acts]
    in_specs += [pl.BlockSpec(w.shape, lambda i: (0, 0)) for w in weights]
    in_specs += [pl.BlockSpec((tm, d), lambda i: (i, 0)),
                 pl.BlockSpec((1, d), lambda i: (0, 0)),
                 pl.BlockSpec((1, d), lambda i: (0, 0))]
    return pl.pallas_call(
        functools.partial(_proj_ln_kernel, n_in=n_in),
        grid=(m // tm,),
        in_specs=in_specs,
        out_specs=[pl.BlockSpec((tm, d), lambda i: (i, 0)), pl.BlockSpec((tm, d // 2), lambda i: (i, 0))],
        out_shape=[jax.ShapeDtypeStruct((m, d), F32), jax.ShapeDtypeStruct((m, d // 2), jnp.int32)],
        compiler_params=_params("parallel"),
        name="out_proj_ln",
    )(*acts, *weights, resid, g.reshape(1, d), b.reshape(1, d))


def _t5_bucket(dist):
    max_exact = REL_BUCKETS // 2
    d = np.maximum(dist, 1).astype(np.float64)
    large = max_exact + (np.log(d / max_exact) / np.log(REL_MAX_DIST / max_exact)
                         * (REL_BUCKETS - max_exact)).astype(np.int64)
    large = np.minimum(large, REL_BUCKETS - 1)
    return np.where(dist < max_exact, dist, large).astype(np.int32)


def _dilated_kernel(q_ref, k_ref, v_ref, bucket_ref, rel_bias_ref, out_ref, o_scr, lse_scr):
    blk = A_BLOCK
    seq = q_ref.shape[1]
    head = pl.program_id(1)
    qi = lax.broadcasted_iota(jnp.int32, (blk, 2 * blk), 0)
    ki = lax.broadcasted_iota(jnp.int32, (blk, 2 * blk), 1)
    rel = qi + blk - ki

    for bi, (window, dil) in enumerate(A_PATTERNS):
        band = window // dil
        n_blocks = seq // (dil * blk)
        valid = jnp.abs(2 * rel - band) <= band
        bucket = bucket_ref[bi]
        bias = jnp.zeros((blk, 2 * blk), F32)
        for b in range(REL_BUCKETS):
            bias = jnp.where(bucket == b, rel_bias_ref[head, b], bias)

        def attend(q_start, k_start, n_keys, bias_blk, valid_blk, bi=bi, dil=dil):
            q_rows = pl.ds(q_start, blk, stride=dil)
            k_rows = pl.ds(k_start, n_keys, stride=dil)
            q = q_ref[0, q_rows, :].astype(BF16)
            kk = k_ref[0, k_rows, :].astype(BF16)
            vv = v_ref[0, k_rows, :].astype(BF16)
            s = _dot_nt(q, kk) + bias_blk
            s = jnp.where(valid_blk, s, NEG_INF)
            m = jnp.max(s, axis=-1, keepdims=True)
            p = jnp.exp(s - m)
            l = jnp.sum(p, axis=-1, keepdims=True)
            o_scr[bi, q_rows, :] = _dot(p.astype(BF16), vv) / l
            lse_scr[bi, q_rows, :] = jnp.broadcast_to(m + jnp.log(l), (blk, HEAD_DIM))

        def first(r, carry, attend=attend, bias=bias, valid=valid):
            attend(r, r, blk, bias[:, blk:], valid[:, blk:])
            return carry

        lax.fori_loop(0, dil, first, 0, unroll=min(dil, DILATED_UNROLL))

        def rest(t, carry, attend=attend, bias=bias, valid=valid, dil=dil):
            r = t & (dil - 1)
            n = (t >> (dil.bit_length() - 1)) + 1
            attend(r + n * blk * dil, r + (n - 1) * blk * dil, 2 * blk, bias, valid)
            return carry

        lax.fori_loop(0, dil * (n_blocks - 1), rest, 0, unroll=DILATED_UNROLL)

    chunk = 2 * blk
    nb = len(A_PATTERNS)

    def mix(c, carry):
        rows = pl.ds(pl.multiple_of(c * chunk, chunk), chunk)
        lses = [lse_scr[b, rows, :] for b in range(nb)]
        m = functools.reduce(jnp.maximum, lses)
        es = [jnp.exp(x - m) for x in lses]
        den = functools.reduce(lambda a, b: a + b, es)
        num = functools.reduce(lambda a, b: a + b, [e * o_scr[b, rows, :] for b, e in enumerate(es)])
        out_ref[0, rows, :] = (num / den).astype(out_ref.dtype)
        return carry

    lax.fori_loop(0, seq // chunk, mix, 0)


def _dilated_bucket_tables():
    qi = np.arange(A_BLOCK)[:, None]
    ki = np.arange(2 * A_BLOCK)[None, :]
    rel = qi + A_BLOCK - ki
    return np.stack([_t5_bucket(np.clip(rel, 0, None) * dilation) for _, dilation in A_PATTERNS], axis=0)


def _dilated_mixture(proj, rel_bias, bsz, seq, heads, proj_width):
    for window, dilation in A_PATTERNS:
        assert seq % (dilation * A_BLOCK) == 0 and window // dilation <= A_BLOCK
    nb = len(A_PATTERNS)
    view = proj.reshape(bsz, seq, proj_width)
    buckets = jnp.asarray(_dilated_bucket_tables())

    def in_spec(group):
        return pl.BlockSpec((1, seq, HEAD_DIM), lambda b, h, g=group: (b, 0, g * heads + h))

    o = pl.pallas_call(
        _dilated_kernel,
        grid=(bsz, heads),
        in_specs=[in_spec(0), in_spec(1), in_spec(2),
                  pl.BlockSpec(buckets.shape, lambda b, h: (0, 0, 0)),
                  pl.BlockSpec(memory_space=pltpu.SMEM)],
        out_specs=pl.BlockSpec((1, seq, HEAD_DIM), lambda b, h: (b, 0, h)),
        out_shape=jax.ShapeDtypeStruct((bsz, seq, heads * HEAD_DIM), BF16),
        scratch_shapes=[pltpu.VMEM((nb, seq, HEAD_DIM), F32), pltpu.VMEM((nb, seq, HEAD_DIM), F32)],
        compiler_params=_params("parallel", "parallel"),
        name="dilated_attention",
    )(view, view, view, buckets, rel_bias.astype(F32).T)
    return o.reshape(bsz * seq, heads * HEAD_DIM)


def _hgrn_levels(t):
    cs = []
    c = t // 2
    while c >= 1:
        cs.append(c)
        c //= 2
    return cs


def _hgrn_consts(t):
    idx = np.arange(t)
    ltri = (idx[:, None] >= idx[None, :]).astype(np.float32)
    sels = []
    for c in _hgrn_levels(t):
        if c < HGRN_BCAST_MIN:
            ref = (idx // (2 * c)) * (2 * c) + c - 1
            sels.append((idx[None, :] == ref[:, None]).astype(np.float32))
    return jnp.asarray(ltri, BF16), jnp.asarray(np.concatenate(sels, axis=0), BF16)


def _hgrn_kernel(q_ref, f_ref, i_ref, g_ref, lb_ref, gn_ref, ltri_ref, sel_ref, o_ref, state_ref,
                 *, n_chunks):
    t = HGRN_CHUNK
    dk = HEAD_DIM
    levels = _hgrn_levels(t)
    lb = lb_ref[0]
    gn = gn_ref[0]
    ltri = ltri_ref[...]
    sel = sel_ref[...]
    row = lax.broadcasted_iota(jnp.int32, (t, dk), 0)
    ti = lax.broadcasted_iota(jnp.int32, (t, t), 0)
    si = lax.broadcasted_iota(jnp.int32, (t, t), 1)
    state_ref[...] = jnp.zeros_like(state_ref)

    def sum3(y):
        return y[:, :dk] + y[:, dk:2 * dk] + y[:, 2 * dk:]

    def chunk(n, carry):
        st = pl.multiple_of(n * t, t)
        fq = _silu(q_ref[0, pl.ds(st, t), :])
        forget = lb + (1.0 - lb) * _sigmoid(f_ref[0, pl.ds(st, t), :])
        logf = jnp.log(forget)
        key = 1.0 - forget
        val = i_ref[0, pl.ds(st, t), :].astype(BF16)
        a = sum3(_dot(ltri, _split3(logf, 1)))
        aref_fine = sum3(_dot(sel, _split3(a, 1)))
        fq_b = fq.astype(BF16)
        key_b = key.astype(BF16)
        scores = jnp.where(ti == si, _dot_nt(fq_b, key_b), 0.0)
        n_fine = 0
        for c in levels:
            if c >= HGRN_BCAST_MIN:
                aref = jnp.concatenate(
                    [jnp.broadcast_to(a[b0 + c - 1:b0 + c, :], (2 * c, dk)) for b0 in range(0, t, 2 * c)],
                    axis=0)
            else:
                aref = aref_fine[n_fine * t:(n_fine + 1) * t, :]
                n_fine += 1
            upper = (row & (2 * c - 1)) >= c
            decayed = jnp.where(upper, fq, key) * jnp.exp(-jnp.abs(a - aref))
            ql = jnp.where(upper, decayed, 0.0).astype(BF16)
            kl = jnp.where(upper, 0.0, decayed).astype(BF16)
            s_l = _dot_nt(ql, kl)
            if 2 * c < t:
                sh = (2 * c).bit_length() - 1
                s_l = jnp.where((ti >> sh) == (si >> sh), s_l, 0.0)
            scores = scores + s_l
        state_t = state_ref[...]
        inter = _dot_nt((fq * jnp.exp(a)).astype(BF16), state_t.astype(BF16))
        o = inter + _dot(scores.astype(BF16), val)
        a_last = a[t - 1:t, :]
        k_dec = (key * jnp.exp(a_last - a)).astype(BF16)
        state_ref[...] = state_t * jnp.exp(a_last) + _dot_tn(val, k_dec)
        o = o * lax.rsqrt(jnp.mean(o * o, axis=-1, keepdims=True) + RMS_EPS) * gn
        o = o * _silu(g_ref[0, pl.ds(st, t), :])
        o_ref[0, pl.ds(st, t), :] = o.astype(o_ref.dtype)
        return carry

    lax.fori_loop(0, n_chunks, chunk, 0, unroll=16)


def _hgrn2(proj, lb, gn_g, bsz, seq, heads, proj_width, first_group):
    assert seq % HGRN_CHUNK == 0
    view = proj.reshape(bsz, seq, proj_width)
    ltri, sel = _hgrn_consts(HGRN_CHUNK)

    def in_spec(group):
        return pl.BlockSpec((1, seq, HEAD_DIM), lambda b, h, g=group: (b, 0, (first_group + g) * heads + h))

    head_vec = pl.BlockSpec((1, 1, HEAD_DIM), lambda b, h: (h, 0, 0))
    o = pl.pallas_call(
        functools.partial(_hgrn_kernel, n_chunks=seq // HGRN_CHUNK),
        grid=(bsz, heads),
        in_specs=[in_spec(0), in_spec(1), in_spec(2), in_spec(3), head_vec, head_vec,
                  pl.BlockSpec(ltri.shape, lambda b, h: (0, 0)),
                  pl.BlockSpec(sel.shape, lambda b, h: (0, 0))],
        out_specs=pl.BlockSpec((1, seq, HEAD_DIM), lambda b, h: (b, 0, h)),
        out_shape=jax.ShapeDtypeStruct((bsz, seq, heads * HEAD_DIM), BF16),
        scratch_shapes=[pltpu.VMEM((HEAD_DIM, HEAD_DIM), F32)],
        compiler_params=_params("parallel", "parallel"),
        name="hgrn2",
    )(view, view, view, view, lb.reshape(heads, 1, HEAD_DIM).astype(F32),
      gn_g.reshape(heads, 1, HEAD_DIM).astype(F32), ltri, sel)
    return o.reshape(bsz * seq, heads * HEAD_DIM)


def _fox_gate_kernel(x_ref, w_ref, b_ref, ltri_ref, place_q_ref, place_k_ref, ones_q_ref, ones_k_ref,
                     qx_ref, kx_ref, carry):
    heads = w_ref.shape[1]
    ts = x_ref.shape[1]

    @pl.when(pl.program_id(1) == 0)
    def _():
        carry[...] = jnp.zeros_like(carry)

    def log_sigmoid(z):
        return jnp.minimum(z, 0.0) - jnp.log(1.0 + jnp.exp(-jnp.abs(z)))

    x = x_ref[0]
    xh = x.astype(BF16)
    xl = (x - xh.astype(F32)).astype(BF16)
    w = w_ref[...]
    wh = w.astype(BF16)
    wl = (w - wh.astype(F32)).astype(BF16)
    f = _dot(xh, wh) + _dot(xh, wl) + _dot(xl, wh) + b_ref[...]
    c = _dot(ltri_ref[...], _split3(log_sigmoid(f), 1))
    c = c[:, :heads] + c[:, heads:2 * heads] + c[:, 2 * heads:] + carry[...]
    carry[...] = c[ts - 1:ts, :]
    terms = _split3(c * LOG2_E, 1)
    qx_ref[0] = (_dot(terms, place_q_ref[...]) + ones_q_ref[...]).astype(BF16)
    kx_ref[0] = (_dot(terms, place_k_ref[...]) + ones_k_ref[...]).astype(BF16)


def _fox_gate_placement(heads):
    place_q = np.zeros((3 * heads, heads * HEAD_DIM), np.float32)
    place_k = np.zeros((3 * heads, heads * HEAD_DIM), np.float32)
    ones_q = np.zeros((1, heads * HEAD_DIM), np.float32)
    ones_k = np.zeros((1, heads * HEAD_DIM), np.float32)
    for h in range(heads):
        for j in range(3):
            place_q[j * heads + h, h * HEAD_DIM + j] = 1.0
            place_k[j * heads + h, h * HEAD_DIM + 3 + j] = -1.0
            ones_q[0, h * HEAD_DIM + 3 + j] = 1.0
            ones_k[0, h * HEAD_DIM + j] = 1.0
    return jnp.asarray(place_q, BF16), jnp.asarray(place_k, BF16), jnp.asarray(ones_q), jnp.asarray(ones_k)


def _fox_gate_terms(x, w_f, b_f, bsz, seq, ts=512):
    d = x.shape[-1]
    heads = w_f.shape[1]
    ts = min(ts, seq)
    idx = np.arange(ts)
    ltri = jnp.asarray(idx[:, None] >= idx[None, :], BF16)
    consts = _fox_gate_placement(heads)
    const = lambda shape: pl.BlockSpec(shape, lambda b, i: (0,) * len(shape))
    out_spec = pl.BlockSpec((1, ts, heads * HEAD_DIM), lambda b, i: (b, i, 0))
    out_sds = jax.ShapeDtypeStruct((bsz, seq, heads * HEAD_DIM), BF16)
    return pl.pallas_call(
        _fox_gate_kernel,
        grid=(bsz, seq // ts),
        in_specs=[pl.BlockSpec((1, ts, d), lambda b, i: (b, i, 0)),
                  const((d, heads)), const((1, heads)), const((ts, ts))] + [const(a.shape) for a in consts],
        out_specs=[out_spec, out_spec],
        out_shape=[out_sds, out_sds],
        scratch_shapes=[pltpu.VMEM((1, heads), F32)],
        compiler_params=_params("parallel", "arbitrary"),
        name="fox_gate_cumsum",
    )(x.reshape(bsz, seq, d), w_f.astype(F32), b_f.astype(F32).reshape(1, heads), ltri, *consts)


def _fox_kernel(q_ref, qx_ref, k_ref, kx_ref, vt_ref, o_ref, s_a, s_b, p_a, p_b, acc_ref):
    tq, tk = FOX_Q_TILE, FOX_K_TILE
    assert tq == 2 * tk
    for i in range(q_ref.shape[1] // tq):
        _fox_query_tile(i, q_ref, qx_ref, k_ref, kx_ref, vt_ref, o_ref, s_a, s_b, p_a, p_b, acc_ref)


def _fox_query_tile(i, q_ref, qx_ref, k_ref, kx_ref, vt_ref, o_ref, s_a, s_b, p_a, p_b, acc_ref):
    tq, tk = FOX_Q_TILE, FOX_K_TILE
    lanes = HEAD_DIM
    rows = slice(i * tq, (i + 1) * tq)
    q = jnp.concatenate([q_ref[0, rows, :], qx_ref[0, rows, :]], axis=1)

    def logits_into(j, s_ref):
        st = pl.multiple_of(j * tk, tk)
        kk = jnp.concatenate([k_ref[0, pl.ds(st, tk), :], kx_ref[0, pl.ds(st, tk), :]], axis=1)
        s_ref[...] = _dot_nt(kk, q)

    def softmax_pv(j, s_ref, p_ref, m, l, diag):
        m_out, l_out, alphas = [], [], []
        for c in range(tq // lanes):
            cols = slice(c * lanes, (c + 1) * lanes)
            if diag is not None and (c + 1) * lanes - 1 < diag * tk:
                m_out.append(m[:, cols])
                l_out.append(l[:, cols])
                alphas.append(jnp.ones((1, lanes), F32))
                p_ref[:, cols] = jnp.zeros((tk, lanes), BF16)
                continue
            s = s_ref[:, cols]
            if diag is not None and c * lanes < diag * tk + tk - 1:
                key = lax.broadcasted_iota(jnp.int32, (tk, lanes), 0) + diag * tk
                qry = lax.broadcasted_iota(jnp.int32, (tk, lanes), 1) + c * lanes
                s = jnp.where(key <= qry, s, NEG_INF)
            m_new = jnp.maximum(m[:, cols], jnp.max(s, axis=0, keepdims=True))
            alpha = jnp.exp2(m[:, cols] - m_new)
            p = jnp.exp2(s - m_new)
            l_out.append(alpha * l[:, cols] + jnp.sum(p, axis=0, keepdims=True))
            m_out.append(m_new)
            alphas.append(alpha)
            p_ref[:, cols] = p.astype(BF16)
        vt = vt_ref[0, :, pl.ds(pl.multiple_of(j * tk, tk), tk)]
        acc_ref[...] = jnp.concatenate(alphas, axis=1) * acc_ref[...] + _dot(vt, p_ref[...])
        return jnp.concatenate(m_out, axis=1), jnp.concatenate(l_out, axis=1)

    def pair(jp, carry):
        m, l = carry
        t0 = 2 * jp
        logits_into(t0 + 1, s_b)
        m, l = softmax_pv(t0, s_a, p_a, m, l, None)
        logits_into(t0 + 2, s_a)
        return softmax_pv(t0 + 1, s_b, p_b, m, l, None)

    acc_ref[...] = jnp.zeros_like(acc_ref)
    logits_into(0, s_a)
    m, l = lax.fori_loop(0, i, pair, (jnp.full((1, tq), NEG_INF, F32), jnp.zeros((1, tq), F32)),
                         unroll=min(max(i, 1), FOX_PAIR_UNROLL))
    t0 = 2 * i
    logits_into(t0 + 1, s_b)
    m, l = softmax_pv(t0, s_a, p_a, m, l, 0)
    m, l = softmax_pv(t0 + 1, s_b, p_b, m, l, 1)
    o_ref[0, rows, :] = jnp.transpose(acc_ref[...] / l).astype(o_ref.dtype)


def _fox_attention(qkv, qx, kx, bsz, seq, heads):
    tq = FOX_Q_TILE
    assert seq % tq == 0 and tq % FOX_K_TILE == 0
    width = heads * HEAD_DIM
    view = qkv.reshape(bsz, seq, 3 * width)
    v_t = jnp.swapaxes(view[:, :, 2 * width:], 1, 2)
    head_spec = lambda off: pl.BlockSpec((1, seq, HEAD_DIM), lambda b, h: (b, 0, off + h))
    o = pl.pallas_call(
        _fox_kernel,
        grid=(bsz, heads),
        in_specs=[head_spec(0), head_spec(0), head_spec(heads), head_spec(0),
                  pl.BlockSpec((1, HEAD_DIM, seq), lambda b, h: (b, h, 0))],
        out_specs=head_spec(0),
        out_shape=jax.ShapeDtypeStruct((bsz, seq, width), BF16),
        scratch_shapes=[pltpu.VMEM((FOX_K_TILE, tq), F32), pltpu.VMEM((FOX_K_TILE, tq), F32),
                        pltpu.VMEM((FOX_K_TILE, tq), BF16), pltpu.VMEM((FOX_K_TILE, tq), BF16),
                        pltpu.VMEM((HEAD_DIM, tq), F32)],
        compiler_params=_params("parallel", "parallel"),
        name="fox_attention",
    )(view, qx, view, kx, v_t)
    return o.reshape(bsz * seq, width)


def _router_kernel(x_ref, wt_ref, bias_ref, utri_ref, eidx_ref, rank_ref, w_ref, cnt_ref, carry):
    gsz = GROUP_SIZE
    tt = x_ref.shape[0]

    @pl.when(pl.program_id(0) == 0)
    def _():
        carry[...] = jnp.zeros_like(carry)

    x = x_ref[...]
    xh = x.astype(BF16)
    xl = (x - xh.astype(F32)).astype(BF16)
    wt = wt_ref[...]
    wh = wt.astype(BF16)
    wl = (wt - wh.astype(F32)).astype(BF16)
    logits = _dot_nt(wh, xh) + _dot_nt(wh, xl) + _dot_nt(wl, xh)
    scores = _sigmoid(logits)
    sel = scores + bias_ref[...]
    sub = lax.broadcasted_iota(jnp.int32, (gsz, tt), 0)
    neg = -jnp.inf

    group_score = []
    for g in range(N_GROUPS):
        blk = sel[g * gsz:(g + 1) * gsz, :]
        m1 = jnp.max(blk, axis=0, keepdims=True)
        first = jnp.min(jnp.where(blk == m1, sub, gsz), axis=0, keepdims=True)
        m2 = jnp.max(jnp.where(sub == first, neg, blk), axis=0, keepdims=True)
        group_score.append(m1 + m2)

    chosen = [jnp.zeros((1, tt), F32) for _ in range(N_GROUPS)]
    for _ in range(TOPK_GROUPS):
        m = functools.reduce(jnp.maximum, group_score)
        gi = jnp.full((1, tt), N_GROUPS, jnp.int32)
        for g in reversed(range(N_GROUPS)):
            gi = jnp.where(group_score[g] == m, g, gi)
        for g in range(N_GROUPS):
            hit = gi == g
            chosen[g] = jnp.where(hit, 1.0, chosen[g])
            group_score[g] = jnp.where(hit, neg, group_score[g])

    cand = [jnp.where(jnp.broadcast_to(chosen[g], (gsz, tt)) > 0.5, sel[g * gsz:(g + 1) * gsz, :], NEG_INF)
            for g in range(N_GROUPS)]
    score_g = [scores[g * gsz:(g + 1) * gsz, :] for g in range(N_GROUPS)]
    eiota = [sub + g * gsz for g in range(N_GROUPS)]
    picked = [jnp.zeros((gsz, tt), F32) for _ in range(N_GROUPS)]
    top_idx, top_score = [], []
    for _ in range(TOP_K):
        m = functools.reduce(jnp.maximum, [jnp.max(c, axis=0, keepdims=True) for c in cand])
        ei = functools.reduce(jnp.minimum, [
            jnp.min(jnp.where(cand[g] == m, eiota[g], N_EXPERTS), axis=0, keepdims=True)
            for g in range(N_GROUPS)])
        sc = jnp.zeros((1, tt), F32)
        for g in range(N_GROUPS):
            hit = eiota[g] == ei
            picked[g] = jnp.where(hit, 1.0, picked[g])
            cand[g] = jnp.where(hit, neg, cand[g])
            sc = sc + jnp.sum(jnp.where(hit, score_g[g], 0.0), axis=0, keepdims=True)
        top_idx.append(ei)
        top_score.append(sc)

    total = functools.reduce(lambda a, b: a + b, top_score)
    mask = jnp.concatenate(picked, axis=0)
    incl = _dot(mask.astype(BF16), utri_ref[...])
    rank = carry[...] + incl - mask
    for r in range(TOP_K):
        ei = top_idx[r]
        rk = jnp.zeros((1, tt), F32)
        for g in range(N_GROUPS):
            rk = rk + jnp.sum(jnp.where(eiota[g] == ei, rank[g * gsz:(g + 1) * gsz, :], 0.0),
                              axis=0, keepdims=True)
        eidx_ref[r:r + 1, :] = ei
        rank_ref[r:r + 1, :] = rk.astype(jnp.int32)
        w_ref[r:r + 1, :] = top_score[r] / total * ROUTED_SCALE
    new_carry = carry[...] + incl[:, tt - 1:tt]
    carry[...] = new_carry
    cnt_ref[...] = jnp.broadcast_to(new_carry, cnt_ref.shape)


def _router(x, router_w, router_bias):
    n, d = x.shape
    tt = min(ROUTER_TILE, n)
    idx = np.arange(tt)
    utri = jnp.asarray(idx[:, None] <= idx[None, :], BF16)
    eidx, rank, w8, cnt = pl.pallas_call(
        _router_kernel,
        grid=(n // tt,),
        in_specs=[pl.BlockSpec((tt, d), lambda i: (i, 0)),
                  pl.BlockSpec((N_EXPERTS, d), lambda i: (0, 0)),
                  pl.BlockSpec((N_EXPERTS, 1), lambda i: (0, 0)),
                  pl.BlockSpec((tt, tt), lambda i: (0, 0))],
        out_specs=[pl.BlockSpec((TOP_K, tt), lambda i: (0, i)),
                   pl.BlockSpec((TOP_K, tt), lambda i: (0, i)),
                   pl.BlockSpec((TOP_K, tt), lambda i: (0, i)),
                   pl.BlockSpec((N_EXPERTS, 128), lambda i: (0, 0))],
        out_shape=[jax.ShapeDtypeStruct((TOP_K, n), jnp.int32),
                   jax.ShapeDtypeStruct((TOP_K, n), jnp.int32),
                   jax.ShapeDtypeStruct((TOP_K, n), F32),
                   jax.ShapeDtypeStruct((N_EXPERTS, 128), F32)],
        scratch_shapes=[pltpu.VMEM((N_EXPERTS, 1), F32)],
        compiler_params=_params("arbitrary"),
        name="moe_router",
    )(x, router_w.astype(F32).T, router_bias.astype(F32).reshape(N_EXPERTS, 1), utri)
    return eidx, rank, w8, cnt[:, 0].astype(jnp.int32)


def _expert_kernel(te_ref, tv_ref, nt_ref, src_ref, x_hbm, wg_ref, wu_ref, wd_ref, y_ref,
                   xbuf_a, xbuf_b, sem_a, sem_b, wg_bf, wu_bf, wd_bf):
    i = pl.program_id(0)
    n_live = nt_ref[0]
    tm, half = xbuf_a.shape
    bufs = ((xbuf_a, sem_a), (xbuf_b, sem_b))
    prev = jnp.maximum(i - 1, 0)

    def copy(buf, sem, r):
        return pltpu.make_async_copy(x_hbm.at[pl.ds(src_ref[0, 0, r], 1), :], buf.at[pl.ds(r, 1), :], sem)

    def ffn(buf, sem, gather=None):
        pltpu.make_async_copy(buf, buf, sem).wait()
        if gather is not None:
            for r in range(tm):
                copy(*gather, r).start()
        row = lax.broadcasted_iota(jnp.int32, (tm, 1), 0)
        x_lo, x_hi = _unpack_bf16_pair(jnp.where(row < tv_ref[prev], buf[...], 0))
        x_lo, x_hi = x_lo.astype(BF16), x_hi.astype(BF16)
        gate = _dot(x_lo, wg_bf[:half, :]) + _dot(x_hi, wg_bf[half:, :])
        up = _dot(x_lo, wu_bf[:half, :]) + _dot(x_hi, wu_bf[half:, :])
        hidden = (_silu(gate) * up).astype(BF16)
        y = _dot(hidden, wd_bf[...])
        y_ref[...] = _pack_bf16_pair(y[:, :half], y[:, half:])

    @pl.when(i == 0)
    def _():
        def issue(r, carry):
            copy(*bufs[0], r).start()
            return carry
        lax.fori_loop(0, tm, issue, 0)

    computing = jnp.logical_and(i >= 1, i <= n_live)
    new_expert = jnp.logical_or(i == 1, te_ref[prev] != te_ref[jnp.maximum(i - 2, 0)])

    @pl.when(jnp.logical_and(computing, new_expert))
    def _():
        wg_bf[...] = wg_ref[0, 0].astype(BF16)
        wu_bf[...] = wu_ref[0, 0].astype(BF16)
        wd_bf[...] = wd_ref[0, 0].astype(BF16)

    for parity in range(2):
        @pl.when(jnp.logical_and(jnp.logical_and(i >= 1, i < n_live), i % 2 == parity))
        def _(parity=parity):
            ffn(*bufs[1 - parity], gather=bufs[parity])

        @pl.when(jnp.logical_and(i == n_live, (i - 1) % 2 == parity))
        def _(parity=parity):
            ffn(*bufs[parity])


def _expert_ffn(x_packed, src_table, w_gate, w_up, w_down, layer, tile_expert, tile_valid, num_tiles):
    n, half = x_packed.shape
    d = 2 * half
    max_tiles, _, tm = src_table.shape
    e_dim = w_gate.shape[3]
    prev = lambda i, nt: jnp.minimum(jnp.maximum(i - 1, 0), nt[0] - 1)
    w_map = lambda i, te, tv, nt: (layer, te[prev(i, nt)], 0, 0)
    xbuf = pltpu.VMEM((tm, half), jnp.int32)
    return pl.pallas_call(
        _expert_kernel,
        grid_spec=pltpu.PrefetchScalarGridSpec(
            num_scalar_prefetch=3,
            grid=(max_tiles + 1,),
            in_specs=[pl.BlockSpec((1, 1, tm), lambda i, te, tv, nt: (jnp.minimum(i, nt[0] - 1), 0, 0),
                                   memory_space=pltpu.SMEM),
                      pl.BlockSpec(memory_space=pl.ANY),
                      pl.BlockSpec((1, 1, d, e_dim), w_map),
                      pl.BlockSpec((1, 1, d, e_dim), w_map),
                      pl.BlockSpec((1, 1, e_dim, d), w_map)],
            out_specs=pl.BlockSpec((tm, half), lambda i, te, tv, nt: (prev(i, nt), 0)),
            scratch_shapes=[xbuf, xbuf, pltpu.SemaphoreType.DMA(()), pltpu.SemaphoreType.DMA(()),
                            pltpu.VMEM((d, e_dim), BF16), pltpu.VMEM((d, e_dim), BF16),
                            pltpu.VMEM((e_dim, d), BF16)]),
        out_shape=jax.ShapeDtypeStruct((max_tiles * tm, half), jnp.int32),
        compiler_params=_params("arbitrary"),
        name="moe_expert_ffn",
    )(tile_expert, tile_valid, num_tiles, src_table, x_packed, w_gate, w_up, w_down)


def _combine_kernel(dest_ref, x_ref, w8_ref, sg_ref, su_ref, sd_ref, g_ref, b_ref, y_hbm,
                    o_ref, obf_ref, ybuf_a, ybuf_b, sem_a, sem_b):
    tc = x_ref.shape[0]
    i = pl.program_id(0)
    n_tiles = pl.num_programs(0) - 1
    bufs = ((ybuf_a, sem_a), (ybuf_b, sem_b))

    def copy(buf, sem, t, k):
        return pltpu.make_async_copy(y_hbm.at[pl.ds(dest_ref[0, k, t], 1), :],
                                     buf.at[k, pl.ds(t, 1), :], sem)

    def finish(buf, sem, gather=None):
        x = x_ref[...]
        xb = x.astype(BF16)
        hidden = (_silu(_dot(xb, sg_ref[...])) * _dot(xb, su_ref[...])).astype(BF16)
        acc = _dot(hidden, sd_ref[...])
        pltpu.make_async_copy(buf, buf, sem).wait()
        if gather is not None:
            for t in range(tc):
                for k in range(TOP_K):
                    copy(*gather, t, k).start(priority=k % 2)
        w8 = w8_ref[...]
        half = buf.shape[2]
        routed_lo = jnp.zeros((tc, half), F32)
        routed_hi = jnp.zeros((tc, half), F32)
        for k in range(TOP_K):
            y_lo, y_hi = _unpack_bf16_pair(buf[k])
            routed_lo = routed_lo + w8[:, k:k + 1] * y_lo
            routed_hi = routed_hi + w8[:, k:k + 1] * y_hi
        acc = acc + jnp.concatenate([routed_lo, routed_hi], axis=1)
        y = _layer_norm_rows(DN_ALPHA * x + acc, g_ref[...], b_ref[...])
        o_ref[...] = y
        obf_ref[...] = y.astype(BF16)

    @pl.when(i == 0)
    def _():
        def issue(t, carry):
            for k in range(TOP_K):
                copy(*bufs[0], t, k).start()
            return carry
        lax.fori_loop(0, tc, issue, 0)

    for parity in range(2):
        @pl.when(jnp.logical_and(jnp.logical_and(i >= 1, i < n_tiles), i % 2 == parity))
        def _(parity=parity):
            finish(*bufs[1 - parity], gather=bufs[parity])

        @pl.when(jnp.logical_and(i == n_tiles, (i - 1) % 2 == parity))
        def _(parity=parity):
            finish(*bufs[parity])


def _combine(x, w8_rows, dest_tiles, y_sorted, s_gate, s_up, s_down, g, b):
    n, d = x.shape
    tc = dest_tiles.shape[2]
    sdim = s_gate.shape[1]
    const = lambda shape: pl.BlockSpec(shape, lambda i: (0,) * len(shape))
    n_tiles = n // tc
    prev = lambda i: (jnp.maximum(i - 1, 0), 0)
    row = pl.BlockSpec((tc, d), prev)
    ybuf = pltpu.VMEM((TOP_K, tc, d // 2), jnp.int32)
    return pl.pallas_call(
        _combine_kernel,
        grid=(n_tiles + 1,),
        in_specs=[pl.BlockSpec((1, TOP_K, tc), lambda i: (jnp.minimum(i, n_tiles - 1), 0, 0),
                               memory_space=pltpu.SMEM),
                  row, pl.BlockSpec((tc, TOP_K), prev),
                  const((d, sdim)), const((d, sdim)), const((sdim, d)), const((1, d)), const((1, d)),
                  pl.BlockSpec(memory_space=pl.ANY)],
        out_specs=[row, row],
        out_shape=[jax.ShapeDtypeStruct((n, d), F32), jax.ShapeDtypeStruct((n, d), BF16)],
        scratch_shapes=[ybuf, ybuf, pltpu.SemaphoreType.DMA(()), pltpu.SemaphoreType.DMA(())],
        compiler_params=_params("arbitrary"),
        name="moe_combine",
    )(dest_tiles, x, w8_rows, s_gate, s_up, s_down, g.reshape(1, d), b.reshape(1, d), y_sorted)


def _tile_major(a, tile):
    k, n = a.shape
    return a.reshape(k, n // tile, tile).transpose(1, 0, 2)


def _moe(x, x_packed, router_w, router_bias, w_gate, w_up, w_down, layer, s_gate, s_up, s_down, ln_g, ln_b):
    n, d = x.shape
    tm = EXPERT_TILE
    eidx, rank, w8, cnt = _router(x, router_w, router_bias)
    tiles_e = (cnt + tm - 1) // tm
    tile_end = jnp.cumsum(tiles_e)
    tile_start = tile_end - tiles_e
    row_start = tile_start * tm
    e_ids = jnp.arange(N_EXPERTS, dtype=jnp.int32)
    dest = jnp.sum(jnp.where(eidx[None] == e_ids[:, None, None], row_start[:, None, None], 0), axis=0) + rank
    max_tiles = (n * TOP_K) // tm + N_EXPERTS
    num_tiles = tile_end[-1]
    tile_ids = jnp.arange(max_tiles, dtype=jnp.int32)
    t_clamped = jnp.minimum(tile_ids, num_tiles - 1)
    tile_expert = jnp.sum(t_clamped[:, None] >= tile_end[None, :], axis=1).astype(jnp.int32)
    tile_valid = jnp.clip(cnt[tile_expert] - (t_clamped - tile_start[tile_expert]) * tm, 0, tm).astype(jnp.int32)

    tokens = jnp.tile(jnp.arange(n, dtype=jnp.int32), TOP_K)
    _, tok_sorted = lax.sort_key_val(dest.reshape(-1), tokens)
    tok_sorted = jnp.concatenate([tok_sorted, jnp.zeros((tm,), jnp.int32)])
    cnt_before = jnp.cumsum(cnt) - cnt
    first_slot = cnt_before[tile_expert] + (t_clamped - tile_start[tile_expert]) * tm
    window = first_slot[:, None] + jnp.arange(tm, dtype=jnp.int32)[None, :]
    src_table = tok_sorted.at[window].get(mode="promise_in_bounds")
    ys = _expert_ffn(x_packed, src_table.reshape(max_tiles, 1, tm), w_gate, w_up, w_down, layer,
                     tile_expert, tile_valid, num_tiles.reshape(1).astype(jnp.int32))
    return _combine(x, w8.T, _tile_major(dest, min(COMBINE_TILE, n)), ys, s_gate, s_up, s_down, ln_g, ln_b)


def _even_layer(x, x_bf, bsz, seq, w_in, gn_g, w_out, rel_bias, lb, ln_g, ln_b):
    heads = w_out.shape[0] // (2 * HEAD_DIM)
    width = heads * HEAD_DIM
    proj_width = w_in.shape[1]
    col_scale = jnp.concatenate([jnp.full((1, width), HEAD_DIM ** -0.5, F32),
                                 jnp.ones((1, proj_width - width), F32)], axis=1)
    proj = _matmul(x_bf, w_in, col_scale, F32)
    o_a = _dilated_mixture(proj, rel_bias, bsz, seq, heads, proj_width)
    o_b = _hgrn2(proj, lb, gn_g, bsz, seq, heads, proj_width, first_group=3)
    return _proj_residual_ln([o_a, o_b], [w_out[:width], w_out[width:]], x, ln_g, ln_b)


def _odd_layer(x, x_bf, bsz, seq, w_qkv, w_f, b_f, w_out, ln_g, ln_b):
    heads = w_f.shape[1]
    width = heads * HEAD_DIM
    col_scale = jnp.concatenate([jnp.full((1, width), LOG2_E * HEAD_DIM ** -0.5, F32),
                                 jnp.ones((1, 2 * width), F32)], axis=1)
    qkv = _matmul(x_bf, w_qkv, col_scale, BF16)
    qx, kx = _fox_gate_terms(x, w_f, b_f, bsz, seq)
    o = _fox_attention(qkv, qx, kx, bsz, seq, heads)
    return _proj_residual_ln([o], [w_out], x, ln_g, ln_b)


def kernel(x, rel_bias, hgrn_lb_logits, a_w_in, a_gn_g, a_w_out, c_w_in, c_b_f, c_w_out, ln_mix_g, ln_mix_b, ln_ffn_g, ln_ffn_b, router_w, router_bias, exp_w_gate, exp_w_up, exp_w_down, sh_w_gate, sh_w_up, sh_w_down):
    bsz, seq, d = x.shape
    depth = ln_mix_g.shape[0]
    lb_all = jnp.cumsum(jax.nn.softmax(hgrn_lb_logits.astype(F32), axis=0), axis=0)
    h = x.reshape(bsz * seq, d).astype(F32)
    h_bf = h.astype(BF16)
    for layer in range(depth):
        j = layer // 2
        if layer % 2 == 0:
            h, h_packed = _even_layer(h, h_bf, bsz, seq, a_w_in[j].astype(BF16), a_gn_g[j],
                                  a_w_out[j].astype(BF16), rel_bias, lb_all[layer],
                                  ln_mix_g[layer], ln_mix_b[layer])
        else:
            c_width = c_w_out.shape[1]
            w_in = c_w_in[j]
            h, h_packed = _odd_layer(h, h_bf, bsz, seq, w_in[:, :3 * c_width].astype(BF16),
                                 w_in[:, 3 * c_width:], c_b_f[j], c_w_out[j].astype(BF16),
                                 ln_mix_g[layer], ln_mix_b[layer])
        h, h_bf = _moe(h, h_packed, router_w[layer], router_bias[layer], exp_w_gate, exp_w_up, exp_w_down, layer,
                       sh_w_gate[layer].astype(BF16), sh_w_up[layer].astype(BF16),
                       sh_w_down[layer].astype(BF16), ln_ffn_g[layer], ln_ffn_b[layer])
    return h.reshape(bsz, seq, d)
```
